```python
import math
import jax, jax.numpy as jnp
from jax import lax
import numpy as np

D_MODEL = 1024
BATCH = 4
SEQ = 8192
DEPTH = 1

HEAD_DIM = 64
MIX_WIDTH = D_MODEL
WIDTH_A = MIX_WIDTH // 2
WIDTH_B = MIX_WIDTH - WIDTH_A
N_HEADS_A = WIDTH_A // HEAD_DIM
DIFF_V_DIM = 2 * HEAD_DIM
N_HEADS_B = WIDTH_B // DIFF_V_DIM
DILATED_PATTERNS = ((128, 1), (512, 4), (2048, 16))
D_FF = 2816
CONV_WIDTH = 3
NUM_BUCKETS = 32
MAX_DISTANCE = 2048
N_BIAS_HEADS = N_HEADS_A + N_HEADS_B
Q_BLOCK = 128
EPS = 1e-6
NEG = -1e30
PROJ_WIDTH = 3 * WIDTH_A + 2 * N_HEADS_B * 2 * HEAD_DIM + WIDTH_B

kernel_name = 'hybrid_dilated_diff_attn_convffn_adaln'


def rms_norm(x, g):
    xf = x.astype(jnp.float32)
    return xf * lax.rsqrt(jnp.mean(xf * xf, axis=-1, keepdims=True) + EPS) * g


def modulate(h, shift, scale):
    return h * (1.0 + scale) + shift


def t5_bucket(rel):
    nb = NUM_BUCKETS // 2
    max_exact = nb // 2
    base = jnp.where(rel > 0, nb, 0)
    n = jnp.abs(rel)
    nf = jnp.maximum(n, 1).astype(jnp.float32)
    large = max_exact + (jnp.log(nf / max_exact) / math.log(MAX_DISTANCE / max_exact)
                         * (nb - max_exact)).astype(jnp.int32)
    large = jnp.minimum(large, nb - 1)
    return base + jnp.where(n < max_exact, n, large)


def dilated_window_attention(q, k, v, table, window, dilation):
    B, H, S, Dh = q.shape
    half = window // (2 * dilation)
    L = S // dilation
    nblk = -(-L // half)
    pad = nblk * half - L

    def by_residue(t):
        return t.reshape(B, H, L, dilation, Dh).transpose(0, 1, 3, 2, 4)

    qs = jnp.pad(by_residue(q), ((0, 0),) * 3 + ((0, pad), (0, 0))).reshape(B, H, dilation, nblk, half, Dh)

    def band(t):
        tp = jnp.pad(by_residue(t), ((0, 0),) * 3 + ((half, half + pad), (0, 0)))
        tp = tp.reshape(B, H, dilation, nblk + 2, half, Dh)
        return jnp.concatenate([tp[:, :, :, :-2], tp[:, :, :, 1:-1], tp[:, :, :, 2:]], axis=4)

    kb, vb = band(k), band(v)
    qi = jnp.arange(half)
    kt = jnp.arange(3 * half)
    rel = kt[None, :] - half - qi[:, None]
    bias = table[t5_bucket(rel * dilation)].transpose(2, 0, 1)
    kpos = jnp.arange(nblk)[:, None] * half + kt[None, :] - half
    valid = (jnp.abs(rel) <= half)[None] & ((kpos >= 0) & (kpos < L))[:, None, :]
    s = jnp.einsum('bhrnqd,bhrnkd->bhrnqk', qs, kb) * (Dh ** -0.5) + bias[None, :, None, None]
    s = jnp.where(valid, s, NEG)
    lse = jax.nn.logsumexp(s, axis=-1)
    p = jnp.exp(s - lse[..., None])
    o = jnp.einsum('bhrnqk,bhrnkd->bhrnqd', p, vb)
    o = o.reshape(B, H, dilation, nblk * half, Dh)[:, :, :, :L].transpose(0, 1, 3, 2, 4).reshape(B, H, S, Dh)
    lse = lse.reshape(B, H, dilation, nblk * half)[:, :, :, :L].transpose(0, 1, 3, 2).reshape(B, H, S)
    return o, lse


def differential_attention(q1, q2, k1, k2, v, table, lam):
    B, H, S, Dh = q1.shape
    nqb = S // Q_BLOCK
    kpos = jnp.arange(S)
    scale = Dh ** -0.5

    def block(args):
        q1b, q2b, start = args
        rel = kpos[None, :] - (start + jnp.arange(Q_BLOCK))[:, None]
        bias = table[t5_bucket(rel)].transpose(2, 0, 1)[None]
        p1 = jax.nn.softmax(jnp.einsum('bhqd,bhkd->bhqk', q1b, k1) * scale + bias, axis=-1)
        p2 = jax.nn.softmax(jnp.einsum('bhqd,bhkd->bhqk', q2b, k2) * scale + bias, axis=-1)
        return jnp.einsum('bhqk,bhkd->bhqd', p1 - lam * p2, v)

    q1s = q1.reshape(B, H, nqb, Q_BLOCK, Dh).transpose(2, 0, 1, 3, 4)
    q2s = q2.reshape(B, H, nqb, Q_BLOCK, Dh).transpose(2, 0, 1, 3, 4)
    starts = jnp.arange(nqb, dtype=jnp.int32) * Q_BLOCK
    o = lax.map(block, (q1s, q2s, starts))
    return o.transpose(1, 2, 0, 3, 4).reshape(B, H, S, v.shape[-1])


def depthwise_conv_centred(u, w, b):
    C = u.shape[-1]
    y = lax.conv_general_dilated(u, w[:, None, :].astype(u.dtype), window_strides=(1,), padding='SAME',
                                 dimension_numbers=('NWC', 'WIO', 'NWC'), feature_group_count=C)
    return y + b


def split_heads(t, n_heads, dh):
    B, S, _ = t.shape
    return t.reshape(B, S, n_heads, dh).transpose(0, 2, 1, 3)


def setup_inputs(seed: int = 0) -> dict:
    key = jax.random.key(seed)
    ks = jax.random.split(key, 24)
    nrm = jax.random.normal
    f32 = jnp.float32
    return {
        'x': nrm(ks[0], (BATCH, SEQ, D_MODEL), f32),
        'c': nrm(ks[1], (BATCH, D_MODEL), f32),
        'w_ada': nrm(ks[2], (DEPTH, D_MODEL, 6 * D_MODEL), f32) * D_MODEL ** -0.5,
        'b_ada': nrm(ks[3], (DEPTH, 6 * D_MODEL), f32) * 0.02,
        'norm1_g': 1.0 + 0.1 * nrm(ks[4], (DEPTH, D_MODEL), f32),
        'w_in': nrm(ks[5], (DEPTH, D_MODEL, PROJ_WIDTH), f32) * D_MODEL ** -0.5,
        'q_norm_a': 1.0 + 0.1 * nrm(ks[6], (DEPTH, HEAD_DIM), f32),
        'k_norm_a': 1.0 + 0.1 * nrm(ks[7], (DEPTH, HEAD_DIM), f32),
        'q_norm_b': 1.0 + 0.1 * nrm(ks[8], (DEPTH, HEAD_DIM), f32),
        'k_norm_b': 1.0 + 0.1 * nrm(ks[9], (DEPTH, HEAD_DIM), f32),
        'rel_bias': nrm(ks[10], (NUM_BUCKETS, N_BIAS_HEADS), f32) * 0.3,
        'lambda_q1': nrm(ks[11], (DEPTH, HEAD_DIM), f32) * 0.1,
        'lambda_k1': nrm(ks[12], (DEPTH, HEAD_DIM), f32) * 0.1,
        'lambda_q2': nrm(ks[13], (DEPTH, HEAD_DIM), f32) * 0.1,
        'lambda_k2': nrm(ks[14], (DEPTH, HEAD_DIM), f32) * 0.1,
        'subln_g': 1.0 + 0.1 * nrm(ks[15], (DEPTH, DIFF_V_DIM), f32),
        'w_out': nrm(ks[16], (DEPTH, MIX_WIDTH, D_MODEL), f32) * MIX_WIDTH ** -0.5,
        'norm2_g': 1.0 + 0.1 * nrm(ks[17], (DEPTH, D_MODEL), f32),
        'w_up': nrm(ks[18], (DEPTH, D_MODEL, 2 * D_FF), f32) * D_MODEL ** -0.5,
        'conv_w': nrm(ks[19], (DEPTH, CONV_WIDTH, 2 * D_FF), f32) * CONV_WIDTH ** -0.5,
        'conv_b': nrm(ks[20], (DEPTH, 2 * D_FF), f32) * 0.02,
        'w_down': nrm(ks[21], (DEPTH, D_FF, D_MODEL), f32) * D_FF ** -0.5,
    }


def reference(x, c, w_ada, b_ada, norm1_g, w_in, q_norm_a, k_norm_a, q_norm_b, k_norm_b, rel_bias,
              lambda_q1, lambda_k1, lambda_q2, lambda_k2, subln_g, w_out, norm2_g, w_up, conv_w, conv_b,
              w_down):
    B, S, D = x.shape
    out_dtype = x.dtype
    h = x.astype(jnp.float32)
    c_act = jax.nn.silu(c.astype(jnp.float32))
    table_a = rel_bias[:, :N_HEADS_A].astype(jnp.float32)
    table_b = rel_bias[:, N_HEADS_A:].astype(jnp.float32)
    qk_b = N_HEADS_B * 2 * HEAD_DIM
    splits = [WIDTH_A, 2 * WIDTH_A, 3 * WIDTH_A, 3 * WIDTH_A + qk_b, 3 * WIDTH_A + 2 * qk_b]

    for layer in range(DEPTH):
        lambda_init = 0.8 - 0.6 * math.exp(-0.3 * layer)
        mod = (c_act @ w_ada[layer] + b_ada[layer])[:, None, :]
        shift1, scale1, gate1, shift2, scale2, gate2 = jnp.split(mod, 6, axis=-1)

        hn = modulate(rms_norm(h, norm1_g[layer]), shift1, scale1)
        proj = hn @ w_in[layer]
        qa, ka, va, qb, kb, vb = jnp.split(proj, splits, axis=-1)

        qa = rms_norm(split_heads(qa, N_HEADS_A, HEAD_DIM), q_norm_a[layer])
        ka = rms_norm(split_heads(ka, N_HEADS_A, HEAD_DIM), k_norm_a[layer])
        va = split_heads(va, N_HEADS_A, HEAD_DIM)
        outs, lses = [], []
        for window, dilation in DILATED_PATTERNS:
            o, l = dilated_window_attention(qa, ka, va, table_a, window, dilation)
            outs.append(o)
            lses.append(l)
        wts = jax.nn.softmax(jnp.stack(lses, axis=0), axis=0)
        o_a = jnp.sum(jnp.stack(outs, axis=0) * wts[..., None], axis=0)
        o_a = o_a.transpose(0, 2, 1, 3).reshape(B, S, WIDTH_A)

        qb = qb.reshape(B, S, N_HEADS_B, 2, HEAD_DIM).transpose(3, 0, 2, 1, 4)
        kb = kb.reshape(B, S, N_HEADS_B, 2, HEAD_DIM).transpose(3, 0, 2, 1, 4)
        qb = rms_norm(qb, q_norm_b[layer])
        kb = rms_norm(kb, k_norm_b[layer])
        vb = split_heads(vb, N_HEADS_B, DIFF_V_DIM)
        lam = (jnp.exp(jnp.sum(lambda_q1[layer] * lambda_k1[layer]).astype(jnp.float32))
               - jnp.exp(jnp.sum(lambda_q2[layer] * lambda_k2[layer]).astype(jnp.float32)) + lambda_init)
        o_b = differential_attention(qb[0], qb[1], kb[0], kb[1], vb, table_b, lam)
        o_b = rms_norm(o_b, subln_g[layer]) * (1.0 - lambda_init)
        o_b = o_b.transpose(0, 2, 1, 3).reshape(B, S, WIDTH_B)

        mixed = jnp.concatenate([o_a, o_b], axis=-1) @ w_out[layer]
        h = h + gate1 * mixed

        hn = modulate(rms_norm(h, norm2_g[layer]), shift2, scale2)
        u = depthwise_conv_centred(hn @ w_up[layer], conv_w[layer], conv_b[layer])
        val, gate = jnp.split(u, 2, axis=-1)
        h = h + gate2 * ((jax.nn.silu(gate) * val) @ w_down[layer])

    return h.astype(out_dtype)
```

```python
import functools
import math

import jax
import jax.numpy as jnp
from jax import lax
from jax.experimental import pallas as pl
from jax.experimental.pallas import tpu as pltpu

F32 = jnp.float32
BF16 = jnp.bfloat16

D_MODEL = 1024
HEAD_DIM = 64
WIDTH_A = 512
WIDTH_B = 512
N_HEADS_A = 8
N_HEADS_B = 4
DILATED_PATTERNS = ((128, 1), (512, 4), (2048, 16))
D_FF = 2816
NUM_BUCKETS = 32
MAX_DISTANCE = 2048
EPS = 1e-6
NEG = -1e30

LANES = 128
BF16_SUBLANES = 16
MXU_DIM = 256
VMEM_LIMIT = 56 * 1024 * 1024

HALF_WIN = 64
DIL_Q = 128
DIL_K = DIL_Q + 2 * HALF_WIN
DIL_TL = 512

DIFF_T = 512
BUCKET_SAT = 1024
DIFF_NB = BUCKET_SAT // DIFF_T
DIFF_NT = 2 * DIFF_NB + 3

PROJ_TM = 512
OUT_TM = 512
FFN_TM = 512
FFN_FC = 256
FFN_HALO = BF16_SUBLANES

_BUCKET_THRESHOLDS = (1, 2, 3, 4, 5, 6, 7, 8, 16, 32, 64, 128, 256, 512, 1024)


def _params(*sem):
    return pltpu.CompilerParams(dimension_semantics=sem, vmem_limit_bytes=VMEM_LIMIT)


def _ada_kernel(c_ref, w_ref, b_ref, o_ref):
    c = c_ref[...]
    ca = c / (1.0 + jnp.exp(-c))
    o_ref[...] = jnp.dot(ca, w_ref[...], preferred_element_type=F32) + b_ref[...]


def _ada(c8, w, b):
    n = w.shape[1]
    tn = 1024
    return pl.pallas_call(
        _ada_kernel,
        grid=(n // tn,),
        in_specs=[pl.BlockSpec((8, D_MODEL), lambda j: (0, 0)),
                  pl.BlockSpec((D_MODEL, tn), lambda j: (0, j)),
                  pl.BlockSpec((1, tn), lambda j: (0, j))],
        out_specs=pl.BlockSpec((8, tn), lambda j: (0, j)),
        out_shape=jax.ShapeDtypeStruct((8, n), F32),
        compiler_params=_params("arbitrary"),
        name="ada",
    )(c8, w, b)


def _head_norm(p, bd, gain):
    ss = jnp.dot((p * p).astype(BF16), bd, preferred_element_type=F32)
    return p * lax.rsqrt(ss * (1.0 / HEAD_DIM) + EPS) * gain


def _in_proj_kernel(x_ref, shift_ref, scale_ref, g_ref, w_ref, bd_ref, gqa_ref, gka_ref, gqb_ref, gkb_ref,
                    qa_ref, ka_ref, va_ref, qb_ref, k1_ref, k2_ref, vb_ref):
    x = x_ref[...]
    ms = jnp.mean(x * x, axis=-1, keepdims=True)
    hn = x * lax.rsqrt(ms + EPS) * g_ref[...]
    hn = (hn * (1.0 + scale_ref[...]) + shift_ref[...]).astype(BF16)
    bd = bd_ref[...]
    lane = lax.broadcasted_iota(jnp.int32, (1, MXU_DIM), 1)
    first = (lane // HEAD_DIM) % 2 == 0

    def proj(g, c):
        lo = g * 512 + c * MXU_DIM
        return jnp.dot(hn, w_ref[:, lo:lo + MXU_DIM], preferred_element_type=F32)

    for c in range(2):
        sl = slice(c * MXU_DIM, (c + 1) * MXU_DIM)
        qa_ref[:, sl] = _head_norm(proj(0, c), bd, gqa_ref[:, sl]).astype(BF16)
        ka_ref[:, sl] = _head_norm(proj(1, c), bd, gka_ref[:, sl]).astype(BF16)
        va_ref[:, sl] = proj(2, c).astype(BF16)
        qb_ref[:, sl] = _head_norm(proj(3, c), bd, gqb_ref[:, sl]).astype(BF16)
        kb = _head_norm(proj(4, c), bd, gkb_ref[:, sl])
        k1_ref[:, sl] = jnp.where(first, kb, 0.0).astype(BF16)
        k2_ref[:, sl] = jnp.where(first, 0.0, kb).astype(BF16)
        vb_ref[:, sl] = proj(5, c).astype(BF16)


def _in_proj(x, shift, scale, g, w_bf, bd, gqa, gka, gqb, gkb):
    B, S, D = x.shape
    tm = PROJ_TM
    row = pl.BlockSpec((None, tm, D), lambda b, i: (b, i, 0))
    mod = pl.BlockSpec((None, 1, D), lambda b, i: (b, 0, 0))
    const = lambda shape: pl.BlockSpec(shape, lambda b, i: (0,) * len(shape))
    out = pl.BlockSpec((None, tm, 512), lambda b, i: (b, i, 0))
    return pl.pallas_call(
        _in_proj_kernel,
        grid=(B, S // tm),
        in_specs=[row, mod, mod, const((1, D)), const(w_bf.shape), const(bd.shape),
                  const((1, 512)), const((1, 512)), const((1, 512)), const((1, 512))],
        out_specs=[out] * 7,
        out_shape=[jax.ShapeDtypeStruct((B, S, 512), BF16)] * 7,
        compiler_params=_params("arbitrary", "arbitrary"),
        name="in_proj",
    )(x, shift, scale, g, w_bf, bd, gqa, gka, gqb, gkb)


def _bias_from_rel(rel, tbl_ref, h):
    n = jnp.abs(rel)
    vneg = jnp.full(rel.shape, tbl_ref[0, h], F32)
    vpos = jnp.full(rel.shape, tbl_ref[NUM_BUCKETS // 2, h], F32)
    for i, th in enumerate(_BUCKET_THRESHOLDS):
        ge = n >= th
        vneg = jnp.where(ge, tbl_ref[i + 1, h], vneg)
        vpos = jnp.where(ge, tbl_ref[NUM_BUCKETS // 2 + i + 1, h], vpos)
    return jnp.where(rel > 0, vpos, vneg)


def _diff_bias_kernel(tbl_ref, o_ref):
    h = pl.program_id(0)
    t = pl.program_id(1)
    shape = (DIFF_T, DIFF_T)
    q = lax.broadcasted_iota(jnp.int32, shape, 0)
    k = lax.broadcasted_iota(jnp.int32, shape, 1)
    rel = k - q + (t - (DIFF_NB + 1)) * DIFF_T
    o_ref[...] = _bias_from_rel(rel, tbl_ref, N_HEADS_A + h)


def _diff_bias(rel_bias):
    return pl.pallas_call(
        _diff_bias_kernel,
        grid=(N_HEADS_B, DIFF_NT),
        in_specs=[pl.BlockSpec(memory_space=pltpu.SMEM)],
        out_specs=pl.BlockSpec((None, None, DIFF_T, DIFF_T), lambda h, t: (h, t, 0, 0)),
        out_shape=jax.ShapeDtypeStruct((N_HEADS_B, DIFF_NT, DIFF_T, DIFF_T), F32),
        compiler_params=_params("arbitrary", "arbitrary"),
        name="diff_bias",
    )(rel_bias)


def _dil_bias_kernel(tbl_ref, o_ref):
    p = pl.program_id(0)
    h = pl.program_id(1)
    shape = (DIL_Q, DIL_K)
    q = lax.broadcasted_iota(jnp.int32, shape, 0)
    k = lax.broadcasted_iota(jnp.int32, shape, 1)
    steps = k - HALF_WIN - q
    dil = lax.shift_left(jnp.int32(1), 2 * p)
    bias = _bias_from_rel(steps * dil, tbl_ref, h)
    o_ref[...] = jnp.where(jnp.abs(steps) <= HALF_WIN, bias, NEG)


def _dil_bias(rel_bias):
    n_pat = len(DILATED_PATTERNS)
    return pl.pallas_call(
        _dil_bias_kernel,
        grid=(n_pat, N_HEADS_A),
        in_specs=[pl.BlockSpec(memory_space=pltpu.SMEM)],
        out_specs=pl.BlockSpec((None, None, DIL_Q, DIL_K), lambda p, h: (p, h, 0, 0)),
        out_shape=jax.ShapeDtypeStruct((n_pat, N_HEADS_A, DIL_Q, DIL_K), F32),
        compiler_params=_params("arbitrary", "arbitrary"),
        name="dil_bias",
    )(rel_bias)


def _dil_kernel(q_ref, kp_ref, kc_ref, kn_ref, vp_ref, vc_ref, vn_ref, bias_ref, o_ref, lse_ref, kbuf, vbuf,
                *, seq_len):
    tl = q_ref.shape[0]
    i = pl.program_id(2)
    kbuf[0:HALF_WIN, :] = kp_ref[...]
    kbuf[HALF_WIN:HALF_WIN + tl, :] = kc_ref[...]
    kbuf[HALF_WIN + tl:, :] = kn_ref[...]
    vbuf[0:HALF_WIN, :] = vp_ref[...]
    vbuf[HALF_WIN:HALF_WIN + tl, :] = vc_ref[...]
    vbuf[HALF_WIN + tl:, :] = vn_ref[...]
    lane = lax.broadcasted_iota(jnp.int32, (1, LANES), 1)
    even = lane < HEAD_DIM
    col = lax.broadcasted_iota(jnp.int32, (1, DIL_K), 1)
    contract_last = (((1,), (1,)), ((), ()))

    def sub_block(sb, carry):
        r0 = pl.multiple_of(sb * DIL_Q, DIL_Q)
        kpos = i * tl + r0 - HALF_WIN + col
        colpen = jnp.where((kpos >= 0) & (kpos < seq_len), 0.0, NEG)
        for hp in range(N_HEADS_A // 2):
            ls = slice(hp * LANES, (hp + 1) * LANES)
            q = q_ref[pl.ds(r0, DIL_Q), ls]
            kw = kbuf[pl.ds(r0, DIL_K), ls]
            vw = vbuf[pl.ds(r0, DIL_K), ls]
            outs, lses = [], []
            for par in range(2):
                keep = even if par == 0 else jnp.logical_not(even)
                kz = jnp.where(keep, kw, jnp.zeros_like(kw))
                s = lax.dot_general(q, kz, contract_last, preferred_element_type=F32)
                s = s + bias_ref[2 * hp + par] + colpen
                m = jnp.max(s, axis=-1, keepdims=True)
                p = jnp.exp(s - m)
                l = jnp.sum(p, axis=-1, keepdims=True)
                o = jnp.dot(p.astype(BF16), vw, preferred_element_type=F32)
                outs.append(o / l)
                lses.append(m + jnp.log(l))
            o_ref[pl.ds(r0, DIL_Q), ls] = jnp.where(even, outs[0], outs[1]).astype(o_ref.dtype)
            lse_ref[pl.ds(r0, DIL_Q), ls] = jnp.where(even, lses[0], lses[1])
        return carry

    lax.fori_loop(0, tl // DIL_Q, sub_block, 0)


def _dilated(q, k, v, bias, dil):
    B, S, W = q.shape
    L = S // dil
    tl = min(DIL_TL, L)
    nhalo = L // HALF_WIN
    view = lambda t: t.reshape(B, L, dil * W)
    cur = pl.BlockSpec((None, tl, W), lambda b, r, i: (b, i, r))
    prev = pl.BlockSpec((None, HALF_WIN, W),
                        lambda b, r, i: (b, jnp.maximum(i * (tl // HALF_WIN) - 1, 0), r))
    nxt = pl.BlockSpec((None, HALF_WIN, W),
                       lambda b, r, i: (b, jnp.minimum((i + 1) * (tl // HALF_WIN), nhalo - 1), r))
    bias_spec = pl.BlockSpec(bias.shape, lambda b, r, i: (0, 0, 0))
    o, lse = pl.pallas_call(
        functools.partial(_dil_kernel, seq_len=L),
        grid=(B, dil, L // tl),
        in_specs=[cur, prev, cur, nxt, prev, cur, nxt, bias_spec],
        out_specs=[cur, cur],
        out_shape=[jax.ShapeDtypeStruct((B, L, dil * W), BF16), jax.ShapeDtypeStruct((B, L, dil * W), F32)],
        scratch_shapes=[pltpu.VMEM((tl + 2 * HALF_WIN, W), BF16), pltpu.VMEM((tl + 2 * HALF_WIN, W), BF16)],
        compiler_params=_params("arbitrary", "arbitrary", "arbitrary"),
        name=f"dilated_{dil}",
    )(view(q), view(k), view(k), view(k), view(v), view(v), view(v), bias)
    return o.reshape(B, S, W), lse.reshape(B, S, W)


def _diff_kernel(q_ref, k1_ref, k2_ref, v_ref, band_ref, lq1_ref, lk1_ref, lq2_ref, lk2_ref, g_ref, o_ref,
                 m1, l1, a1, m2, l2, a2, *, lambda_init):
    i = pl.program_id(2)
    j = pl.program_id(3)

    @pl.when(j == 0)
    def _():
        for m, l, a in ((m1, l1, a1), (m2, l2, a2)):
            m[...] = jnp.full(m.shape, NEG, F32)
            l[...] = jnp.zeros(l.shape, F32)
            a[...] = jnp.zeros(a.shape, F32)

    t = jnp.clip(j - i, -(DIFF_NB + 1), DIFF_NB + 1) + (DIFF_NB + 1)
    bias = band_ref[t]
    q = q_ref[...]
    v = v_ref[...]
    contract_last = (((1,), (1,)), ((), ()))
    for k_ref, m, l, a in ((k1_ref, m1, l1, a1), (k2_ref, m2, l2, a2)):
        s = lax.dot_general(q, k_ref[...], contract_last, preferred_element_type=F32) + bias
        m_prev = m[...]
        m_new = jnp.maximum(m_prev, jnp.max(s, axis=-1, keepdims=True))
        alpha = jnp.exp(m_prev - m_new)
        p = jnp.exp(s - m_new)
        l[...] = alpha * l[...] + jnp.sum(p, axis=-1, keepdims=True)
        a[...] = alpha * a[...] + jnp.dot(p.astype(BF16), v, preferred_element_type=F32)
        m[...] = m_new

    @pl.when(j == pl.num_programs(3) - 1)
    def _():
        lam = (jnp.exp(jnp.sum(lq1_ref[...] * lk1_ref[...], axis=-1, keepdims=True))
               - jnp.exp(jnp.sum(lq2_ref[...] * lk2_ref[...], axis=-1, keepdims=True)) + lambda_init)
        o = a1[...] / l1[...] - lam * (a2[...] / l2[...])
        ms = jnp.mean(o * o, axis=-1, keepdims=True)
        o = o * lax.rsqrt(ms + EPS) * g_ref[...] * (1.0 - lambda_init)
        o_ref[...] = o.astype(o_ref.dtype)


def _diff_attention(qb, k1, k2, vb, band, lq1, lk1, lq2, lk2, g, lambda_init):
    B, S, W = qb.shape
    T = DIFF_T
    qspec = pl.BlockSpec((None, T, LANES), lambda b, h, i, j: (b, i, h))
    kspec = pl.BlockSpec((None, T, LANES), lambda b, h, i, j: (b, j, h))
    band_spec = pl.BlockSpec((None, DIFF_NT, T, T), lambda b, h, i, j: (h, 0, 0, 0))
    vec = lambda n: pl.BlockSpec((1, n), lambda b, h, i, j: (0, 0))
    return pl.pallas_call(
        functools.partial(_diff_kernel, lambda_init=lambda_init),
        grid=(B, N_HEADS_B, S // T, S // T),
        in_specs=[qspec, kspec, kspec, kspec, band_spec, vec(HEAD_DIM), vec(HEAD_DIM), vec(HEAD_DIM),
                  vec(HEAD_DIM), vec(LANES)],
        out_specs=qspec,
        out_shape=jax.ShapeDtypeStruct((B, S, W), BF16),
        scratch_shapes=[pltpu.VMEM((T, 1), F32), pltpu.VMEM((T, 1), F32), pltpu.VMEM((T, LANES), F32),
                        pltpu.VMEM((T, 1), F32), pltpu.VMEM((T, 1), F32), pltpu.VMEM((T, LANES), F32)],
        compiler_params=_params("arbitrary", "arbitrary", "arbitrary", "arbitrary"),
        name="diff_attn",
    )(qb, k1, k2, vb, band, lq1, lk1, lq2, lk2, g)


def _out_proj_kernel(o1_ref, o2_ref, o3_ref, s1_ref, s2_ref, s3_ref, ob_ref, x_ref, gate_ref, shift_ref,
                     scale_ref, g_ref, wa_ref, wb_ref, h_ref, hn_ref):
    s1, s2, s3 = s1_ref[...], s2_ref[...], s3_ref[...]
    mx = jnp.maximum(jnp.maximum(s1, s2), s3)
    e1, e2, e3 = jnp.exp(s1 - mx), jnp.exp(s2 - mx), jnp.exp(s3 - mx)
    oa = (e1 * o1_ref[...].astype(F32) + e2 * o2_ref[...].astype(F32) + e3 * o3_ref[...].astype(F32))
    oa = oa / (e1 + e2 + e3)
    mixed = (jnp.dot(oa.astype(BF16), wa_ref[...], preferred_element_type=F32)
             + jnp.dot(ob_ref[...], wb_ref[...], preferred_element_type=F32))
    h = x_ref[...] + gate_ref[...] * mixed
    h_ref[...] = h
    ms = jnp.mean(h * h, axis=-1, keepdims=True)
    hn = h * lax.rsqrt(ms + EPS) * g_ref[...]
    hn_ref[...] = (hn * (1.0 + scale_ref[...]) + shift_ref[...]).astype(hn_ref.dtype)


def _out_proj(oas, lses, ob, x, gate, shift, scale, g, wa, wb):
    B, S, D = x.shape
    tm = OUT_TM
    half = pl.BlockSpec((None, tm, 512), lambda b, i: (b, i, 0))
    row = pl.BlockSpec((None, tm, D), lambda b, i: (b, i, 0))
    mod = pl.BlockSpec((None, 1, D), lambda b, i: (b, 0, 0))
    const = lambda shape: pl.BlockSpec(shape, lambda b, i: (0,) * len(shape))
    return pl.pallas_call(
        _out_proj_kernel,
        grid=(B, S // tm),
        in_specs=[half] * 7 + [row, mod, mod, mod, const((1, D)), const(wa.shape), const(wb.shape)],
        out_specs=[row, row],
        out_shape=[jax.ShapeDtypeStruct((B, S, D), F32), jax.ShapeDtypeStruct((B, S, D), BF16)],
        compiler_params=_params("arbitrary", "arbitrary"),
        name="out_proj",
    )(*oas, *lses, ob, x, gate, shift, scale, g, wa, wb)


def _ffn_kernel(hp_ref, hc_ref, hn_ref, h_ref, gate_ref, wu_ref, cw_ref, cb_ref, wd_ref, o_ref, lhs):
    i = pl.program_id(1)
    tm = hc_ref.shape[0]
    n = tm + 2 * FFN_HALO
    lhs[0:FFN_HALO, :] = jnp.where(i > 0, hp_ref[...], jnp.zeros_like(hp_ref))
    lhs[FFN_HALO:FFN_HALO + tm, :] = hc_ref[...]
    lhs[FFN_HALO + tm:, :] = jnp.where(i < pl.num_programs(1) - 1, hn_ref[...], jnp.zeros_like(hn_ref))
    x = lhs[...]

    def conv(u, lo):
        cw = cw_ref[:, lo:lo + FFN_FC]
        mid = slice(FFN_HALO, FFN_HALO + tm)
        below = pltpu.roll(u, 1, 0)[mid]
        above = pltpu.roll(u, n - 1, 0)[mid]
        return cw[0:1] * below + cw[1:2] * u[mid] + cw[2:3] * above + cb_ref[:, lo:lo + FFN_FC]

    acc = jnp.zeros((tm, D_MODEL), F32)
    for c in range(D_FF // FFN_FC):
        lo_v = c * FFN_FC
        lo_g = D_FF + c * FFN_FC
        val = conv(jnp.dot(x, wu_ref[:, lo_v:lo_v + FFN_FC], preferred_element_type=F32), lo_v)
        gt = conv(jnp.dot(x, wu_ref[:, lo_g:lo_g + FFN_FC], preferred_element_type=F32), lo_g)
        act = gt / (1.0 + jnp.exp(-gt)) * val
        acc = acc + jnp.dot(act.astype(BF16), wd_ref[lo_v:lo_v + FFN_FC, :], preferred_element_type=F32)
    o_ref[...] = h_ref[...] + gate_ref[...] * acc


def _ffn(hn, h, gate, wu, cw, cb, wd):
    B, S, D = h.shape
    tm = FFN_TM
    per = tm // FFN_HALO
    nh = S // FFN_HALO
    row = lambda dt: pl.BlockSpec((None, tm, D), lambda b, i: (b, i, 0))
    prev = pl.BlockSpec((None, FFN_HALO, D), lambda b, i: (b, jnp.maximum(i * per - 1, 0), 0))
    nxt = pl.BlockSpec((None, FFN_HALO, D), lambda b, i: (b, jnp.minimum((i + 1) * per, nh - 1), 0))
    mod = pl.BlockSpec((None, 1, D), lambda b, i: (b, 0, 0))
    const = lambda shape: pl.BlockSpec(shape, lambda b, i: (0,) * len(shape), pipeline_mode=pl.Buffered(1))
    return pl.pallas_call(
        _ffn_kernel,
        grid=(B, S // tm),
        in_specs=[prev, row(BF16), nxt, row(F32), mod, const(wu.shape), const(cw.shape), const(cb.shape),
                  const(wd.shape)],
        out_specs=row(F32),
        out_shape=jax.ShapeDtypeStruct((B, S, D), F32),
        scratch_shapes=[pltpu.VMEM((tm + 2 * FFN_HALO, D), BF16)],
        compiler_params=_params("arbitrary", "arbitrary"),
        name="conv_ffn",
    )(hn, hn, hn, h, gate, wu, cw, cb, wd)


def _block_diag_ones():
    r = jnp.arange(MXU_DIM) // HEAD_DIM
    return (r[:, None] == r[None, :]).astype(BF16)


def kernel(x, c, w_ada, b_ada, norm1_g, w_in, q_norm_a, k_norm_a, q_norm_b, k_norm_b, rel_bias, lambda_q1,
           lambda_k1, lambda_q2, lambda_k2, subln_g, w_out, norm2_g, w_up, conv_w, conv_b, w_down):
    B, S, D = x.shape
    depth = w_ada.shape[0]
    h = x.astype(F32)
    c8 = jnp.pad(c.astype(F32), ((0, 8 - B), (0, 0)))
    bd = _block_diag_ones()
    rel_bias = rel_bias.astype(F32)
    band_b = _diff_bias(rel_bias)
    band_a = _dil_bias(rel_bias)
    qscale = HEAD_DIM ** -0.5
    tile8 = lambda g: jnp.tile(g.astype(F32), WIDTH_A // HEAD_DIM)[None, :]
    row = lambda v: v.astype(F32)[None, :]

    for layer in range(depth):
        lambda_init = 0.8 - 0.6 * math.exp(-0.3 * layer)
        mod = _ada(c8, w_ada[layer].astype(F32), row(b_ada[layer]))[:B]
        shift1, scale1, gate1, shift2, scale2, gate2 = [m[:, None, :] for m in jnp.split(mod, 6, axis=-1)]

        qa, ka, va, qb, k1, k2, vb = _in_proj(
            h, shift1, scale1, row(norm1_g[layer]), w_in[layer].astype(BF16), bd,
            tile8(q_norm_a[layer]) * qscale, tile8(k_norm_a[layer]),
            tile8(q_norm_b[layer]) * qscale, tile8(k_norm_b[layer]))

        oas, lses = [], []
        for p, (_, dil) in enumerate(DILATED_PATTERNS):
            o, lse = _dilated(qa, ka, va, band_a[p], dil)
            oas.append(o)
            lses.append(lse)

        ob = _diff_attention(qb, k1, k2, vb, band_b, row(lambda_q1[layer]), row(lambda_k1[layer]),
                             row(lambda_q2[layer]), row(lambda_k2[layer]), row(subln_g[layer]), lambda_init)

        w_o = w_out[layer].astype(BF16)
        h, hn = _out_proj(oas, lses, ob, h, gate1, shift2, scale2, row(norm2_g[layer]),
                          w_o[:WIDTH_A], w_o[WIDTH_A:])
        h = _ffn(hn, h, gate2, w_up[layer].astype(BF16), conv_w[layer].astype(F32), row(conv_b[layer]),
                 w_down[layer].astype(BF16))

    return h.astype(x.dtype)
```

```python
import functools
import math

import jax
import jax.numpy as jnp
from jax import lax
from jax.experimental import pallas as pl
from jax.experimental.pallas import tpu as pltpu

F32 = jnp.float32
BF16 = jnp.bfloat16

D_MODEL = 1024
HEAD_DIM = 64
WIDTH_A = 512
WIDTH_B = 512
N_HEADS_A = 8
N_HEADS_B = 4
DILATED_PATTERNS = ((128, 1), (512, 4), (2048, 16))
D_FF = 2816
NUM_BUCKETS = 32
MAX_DISTANCE = 2048
EPS = 1e-6
NEG = -1e30
LOG2E = math.log2(math.e)

LANES = 128
BF16_SUBLANES = 16
MXU_DIM = 256
VMEM_LIMIT = 56 * 1024 * 1024

HALF_WIN = 64
DIL_Q = 128
DIL_K = DIL_Q + 2 * HALF_WIN
DIL_TL = 512

DIFF_T = 512
BUCKET_SAT = 1024
DIFF_NB = BUCKET_SAT // DIFF_T
DIFF_NT = 2 * DIFF_NB + 3

PROJ_TM = 512
OUT_TM = 512
FFN_TM = 512
FFN_FC = 256
FFN_HALO = BF16_SUBLANES

_BUCKET_THRESHOLDS = (1, 2, 3, 4, 5, 6, 7, 8, 16, 32, 64, 128, 256, 512, 1024)


def _params(*sem):
    return pltpu.CompilerParams(dimension_semantics=sem, vmem_limit_bytes=VMEM_LIMIT)


def _ada_kernel(c_ref, w_ref, b_ref, o_ref):
    c = c_ref[...]
    ca = c / (1.0 + jnp.exp(-c))
    o_ref[...] = jnp.dot(ca, w_ref[...], preferred_element_type=F32) + b_ref[...]


def _ada(c8, w, b):
    n = w.shape[1]
    tn = 1024
    return pl.pallas_call(
        _ada_kernel,
        grid=(n // tn,),
        in_specs=[pl.BlockSpec((8, D_MODEL), lambda j: (0, 0)),
                  pl.BlockSpec((D_MODEL, tn), lambda j: (0, j)),
                  pl.BlockSpec((1, tn), lambda j: (0, j))],
        out_specs=pl.BlockSpec((8, tn), lambda j: (0, j)),
        out_shape=jax.ShapeDtypeStruct((8, n), F32),
        compiler_params=_params("arbitrary"),
        name="ada",
    )(c8, w, b)


def _head_norm(p, bd, gain):
    ss = jnp.dot((p * p).astype(BF16), bd, preferred_element_type=F32)
    return p * lax.rsqrt(ss * (1.0 / HEAD_DIM) + EPS) * gain


def _in_proj_kernel(x_ref, shift_ref, scale_ref, g_ref, w_ref, bd_ref, gqa_ref, gka_ref, gqb_ref, gkb_ref,
                    qa_ref, ka_ref, va_ref, qb_ref, k1_ref, k2_ref, vbt_ref):
    x = x_ref[...]
    ms = jnp.mean(x * x, axis=-1, keepdims=True)
    hn = x * lax.rsqrt(ms + EPS) * g_ref[...]
    hn = (hn * (1.0 + scale_ref[...]) + shift_ref[...]).astype(BF16)
    bd = bd_ref[...]
    lane = lax.broadcasted_iota(jnp.int32, (1, MXU_DIM), 1)
    first = (lane // HEAD_DIM) % 2 == 0

    def proj(g, c):
        lo = g * 512 + c * MXU_DIM
        return jnp.dot(hn, w_ref[:, lo:lo + MXU_DIM], preferred_element_type=F32)

    for c in range(2):
        sl = slice(c * MXU_DIM, (c + 1) * MXU_DIM)
        qa_ref[:, sl] = _head_norm(proj(0, c), bd, gqa_ref[:, sl]).astype(BF16)
        ka_ref[:, sl] = _head_norm(proj(1, c), bd, gka_ref[:, sl]).astype(BF16)
        va_ref[:, sl] = proj(2, c).astype(BF16)
        qb_ref[:, sl] = _head_norm(proj(3, c), bd, gqb_ref[:, sl]).astype(BF16)
        kb = _head_norm(proj(4, c), bd, gkb_ref[:, sl])
        k1_ref[:, sl] = jnp.where(first, kb, 0.0).astype(BF16)
        k2_ref[:, sl] = jnp.where(first, 0.0, kb).astype(BF16)
        vbt_ref[sl, :] = proj(5, c).T.astype(BF16)


def _in_proj(x, shift, scale, g, w_bf, bd, gqa, gka, gqb, gkb):
    B, S, D = x.shape
    tm = PROJ_TM
    row = pl.BlockSpec((None, tm, D), lambda b, i: (b, i, 0))
    mod = pl.BlockSpec((None, 1, D), lambda b, i: (b, 0, 0))
    const = lambda shape: pl.BlockSpec(shape, lambda b, i: (0,) * len(shape))
    out = pl.BlockSpec((None, tm, 512), lambda b, i: (b, i, 0))
    out_t = pl.BlockSpec((None, 512, tm), lambda b, i: (b, 0, i))
    return pl.pallas_call(
        _in_proj_kernel,
        grid=(B, S // tm),
        in_specs=[row, mod, mod, const((1, D)), const(w_bf.shape), const(bd.shape),
                  const((1, 512)), const((1, 512)), const((1, 512)), const((1, 512))],
        out_specs=[out] * 6 + [out_t],
        out_shape=[jax.ShapeDtypeStruct((B, S, 512), BF16)] * 6 + [jax.ShapeDtypeStruct((B, 512, S), BF16)],
        compiler_params=_params("arbitrary", "arbitrary"),
        name="in_proj",
    )(x, shift, scale, g, w_bf, bd, gqa, gka, gqb, gkb)


def _bias_from_rel(rel, tbl_ref, h):
    n = jnp.abs(rel)
    vneg = jnp.full(rel.shape, tbl_ref[0, h], F32)
    vpos = jnp.full(rel.shape, tbl_ref[NUM_BUCKETS // 2, h], F32)
    for i, th in enumerate(_BUCKET_THRESHOLDS):
        ge = n >= th
        vneg = jnp.where(ge, tbl_ref[i + 1, h], vneg)
        vpos = jnp.where(ge, tbl_ref[NUM_BUCKETS // 2 + i + 1, h], vpos)
    return jnp.where(rel > 0, vpos, vneg)


def _diff_bias_kernel(tbl_ref, o_ref):
    h = pl.program_id(0)
    t = pl.program_id(1)
    shape = (DIFF_T, DIFF_T)
    k = lax.broadcasted_iota(jnp.int32, shape, 0)
    q = lax.broadcasted_iota(jnp.int32, shape, 1)
    rel = k - q + (t - (DIFF_NB + 1)) * DIFF_T
    o_ref[...] = _bias_from_rel(rel, tbl_ref, N_HEADS_A + h) * LOG2E


def _diff_bias(rel_bias):
    return pl.pallas_call(
        _diff_bias_kernel,
        grid=(N_HEADS_B, DIFF_NT),
        in_specs=[pl.BlockSpec(memory_space=pltpu.SMEM)],
        out_specs=pl.BlockSpec((None, None, DIFF_T, DIFF_T), lambda h, t: (h, t, 0, 0)),
        out_shape=jax.ShapeDtypeStruct((N_HEADS_B, DIFF_NT, DIFF_T, DIFF_T), F32),
        compiler_params=_params("arbitrary", "arbitrary"),
        name="diff_bias",
    )(rel_bias)


def _dil_bias_kernel(tbl_ref, o_ref):
    p = pl.program_id(0)
    h = pl.program_id(1)
    shape = (DIL_Q, DIL_K)
    q = lax.broadcasted_iota(jnp.int32, shape, 0)
    k = lax.broadcasted_iota(jnp.int32, shape, 1)
    steps = k - HALF_WIN - q
    dil = lax.shift_left(jnp.int32(1), 2 * p)
    bias = _bias_from_rel(steps * dil, tbl_ref, h)
    o_ref[...] = jnp.where(jnp.abs(steps) <= HALF_WIN, bias, NEG)


def _dil_bias(rel_bias):
    n_pat = len(DILATED_PATTERNS)
    return pl.pallas_call(
        _dil_bias_kernel,
        grid=(n_pat, N_HEADS_A),
        in_specs=[pl.BlockSpec(memory_space=pltpu.SMEM)],
        out_specs=pl.BlockSpec((None, None, DIL_Q, DIL_K), lambda p, h: (p, h, 0, 0)),
        out_shape=jax.ShapeDtypeStruct((n_pat, N_HEADS_A, DIL_Q, DIL_K), F32),
        compiler_params=_params("arbitrary", "arbitrary"),
        name="dil_bias",
    )(rel_bias)


def _dil_kernel(q_ref, kp_ref, kc_ref, kn_ref, vp_ref, vc_ref, vn_ref, bias_ref, o_ref, lse_ref, kbuf, vbuf,
                *, seq_len):
    tl = q_ref.shape[0]
    i = pl.program_id(2)
    kbuf[0:HALF_WIN, :] = kp_ref[...]
    kbuf[HALF_WIN:HALF_WIN + tl, :] = kc_ref[...]
    kbuf[HALF_WIN + tl:, :] = kn_ref[...]
    vbuf[0:HALF_WIN, :] = vp_ref[...]
    vbuf[HALF_WIN:HALF_WIN + tl, :] = vc_ref[...]
    vbuf[HALF_WIN + tl:, :] = vn_ref[...]
    lane = lax.broadcasted_iota(jnp.int32, (1, LANES), 1)
    even = lane < HEAD_DIM
    col = lax.broadcasted_iota(jnp.int32, (1, DIL_K), 1)
    contract_last = (((1,), (1,)), ((), ()))

    def sub_block(sb, carry):
        r0 = pl.multiple_of(sb * DIL_Q, DIL_Q)
        kpos = i * tl + r0 - HALF_WIN + col
        colpen = jnp.where((kpos >= 0) & (kpos < seq_len), 0.0, NEG)
        for hp in range(N_HEADS_A // 2):
            ls = slice(hp * LANES, (hp + 1) * LANES)
            q = q_ref[pl.ds(r0, DIL_Q), ls]
            kw = kbuf[pl.ds(r0, DIL_K), ls]
            vw = vbuf[pl.ds(r0, DIL_K), ls]
            outs, lses = [], []
            for par in range(2):
                keep = even if par == 0 else jnp.logical_not(even)
                kz = jnp.where(keep, kw, jnp.zeros_like(kw))
                s = lax.dot_general(q, kz, contract_last, preferred_element_type=F32)
                s = s + bias_ref[2 * hp + par] + colpen
                m = jnp.max(s, axis=-1, keepdims=True)
                p = jnp.exp(s - m)
                l = jnp.sum(p, axis=-1, keepdims=True)
                o = jnp.dot(p.astype(BF16), vw, preferred_element_type=F32)
                outs.append(o / l)
                lses.append(m + jnp.log(l))
            o_ref[pl.ds(r0, DIL_Q), ls] = jnp.where(even, outs[0], outs[1]).astype(o_ref.dtype)
            lse_ref[pl.ds(r0, DIL_Q), ls] = jnp.where(even, lses[0], lses[1])
        return carry

    lax.fori_loop(0, tl // DIL_Q, sub_block, 0)


def _dilated(q, k, v, bias, dil):
    B, S, W = q.shape
    L = S // dil
    tl = min(DIL_TL, L)
    nhalo = L // HALF_WIN
    view = lambda t: t.reshape(B, L, dil * W)
    cur = pl.BlockSpec((None, tl, W), lambda b, r, i: (b, i, r))
    prev = pl.BlockSpec((None, HALF_WIN, W),
                        lambda b, r, i: (b, jnp.maximum(i * (tl // HALF_WIN) - 1, 0), r))
    nxt = pl.BlockSpec((None, HALF_WIN, W),
                       lambda b, r, i: (b, jnp.minimum((i + 1) * (tl // HALF_WIN), nhalo - 1), r))
    bias_spec = pl.BlockSpec(bias.shape, lambda b, r, i: (0, 0, 0))
    o, lse = pl.pallas_call(
        functools.partial(_dil_kernel, seq_len=L),
        grid=(B, dil, L // tl),
        in_specs=[cur, prev, cur, nxt, prev, cur, nxt, bias_spec],
        out_specs=[cur, cur],
        out_shape=[jax.ShapeDtypeStruct((B, L, dil * W), BF16), jax.ShapeDtypeStruct((B, L, dil * W), F32)],
        scratch_shapes=[pltpu.VMEM((tl + 2 * HALF_WIN, W), BF16), pltpu.VMEM((tl + 2 * HALF_WIN, W), BF16)],
        compiler_params=_params("arbitrary", "arbitrary", "arbitrary"),
        name=f"dilated_{dil}",
    )(view(q), view(k), view(k), view(k), view(v), view(v), view(v), bias)
    return o.reshape(B, S, W), lse.reshape(B, S, W)


def _diff_kernel(tbl_ref, q_ref, k1_ref, k2_ref, vt_ref, band_ref, lq1_ref, lk1_ref, lq2_ref, lk2_ref, g_ref,
                 o_ref, m1, l1, a1, m2, l2, a2, sa, sb, *, lambda_init):
    h = pl.program_id(1)
    i = pl.program_id(2)
    T = q_ref.shape[0]
    nk = k1_ref.shape[0] // T
    streams = ((k1_ref, m1, l1, a1), (k2_ref, m2, l2, a2))
    for _, m, l, a in streams:
        m[...] = jnp.full(m.shape, NEG, F32)
        l[...] = jnp.zeros(l.shape, F32)
        a[...] = jnp.zeros(a.shape, F32)
    q = q_ref[...]
    contract_last = (((1,), (1,)), ((), ()))

    def scores(j, sbuf):
        r0 = pl.multiple_of(j * T, T)
        for st, (k_ref, _, _, _) in enumerate(streams):
            sbuf[st] = lax.dot_general(k_ref[pl.ds(r0, T), :], q, contract_last, preferred_element_type=F32)

    def softmax_pv(j, sbuf, near, shift):
        r0 = pl.multiple_of(j * T, T)
        vt = vt_ref[:, pl.ds(r0, T)]
        for st, (_, m, l, a) in enumerate(streams):
            s = sbuf[st]
            if near:
                s = s + band_ref[jnp.clip(j - i, -(DIFF_NB + 1), DIFF_NB + 1) + (DIFF_NB + 1)]
            m_prev = m[...]
            m_new = jnp.maximum(m_prev, jnp.max(s, axis=0, keepdims=True) + shift)
            alpha = jnp.exp2(m_prev - m_new)
            p = jnp.exp2(s - (m_new - shift))
            l[...] = alpha * l[...] + jnp.sum(p, axis=0, keepdims=True)
            a[...] = alpha * a[...] + jnp.dot(vt, p.astype(BF16), preferred_element_type=F32)
            m[...] = m_new

    def pair(near, shift):
        def body(pr, carry):
            j0 = 2 * pr
            scores(j0 + 1, sb)
            softmax_pv(j0, sa, near, shift)
            scores(jnp.minimum(j0 + 2, nk - 1), sa)
            softmax_pv(j0 + 1, sb, near, shift)
            return carry
        return body

    scores(0, sa)

    lo = jnp.maximum((i - DIFF_NB) // 2, 0)
    hi = jnp.minimum((i + DIFF_NB) // 2 + 1, nk // 2)
    c_neg = tbl_ref[NUM_BUCKETS // 2 - 1, N_HEADS_A + h] * LOG2E
    c_pos = tbl_ref[NUM_BUCKETS - 1, N_HEADS_A + h] * LOG2E
    lax.fori_loop(0, lo, pair(False, c_neg), 0)
    lax.fori_loop(lo, hi, pair(True, 0.0), 0)
    lax.fori_loop(hi, nk // 2, pair(False, c_pos), 0)

    lam = (jnp.exp(jnp.sum(lq1_ref[...] * lk1_ref[...], axis=-1, keepdims=True))
           - jnp.exp(jnp.sum(lq2_ref[...] * lk2_ref[...], axis=-1, keepdims=True)) + lambda_init)
    o = a1[...] / l1[...] - lam * (a2[...] / l2[...])
    ms = jnp.mean(o * o, axis=0, keepdims=True)
    o = o * lax.rsqrt(ms + EPS) * (g_ref[...] * (1.0 - lambda_init))
    o_ref[...] = o.T.astype(o_ref.dtype)


def _diff_attention(rel_bias, qb, k1, k2, vbt, band, lq1, lk1, lq2, lk2, g_col, lambda_init):
    B, S, W = qb.shape
    T = DIFF_T
    qspec = pl.BlockSpec((None, T, LANES), lambda b, h, i: (b, i, h))
    kspec = pl.BlockSpec((None, S, LANES), lambda b, h, i: (b, 0, h))
    vspec = pl.BlockSpec((None, LANES, S), lambda b, h, i: (b, h, 0))
    band_spec = pl.BlockSpec((None, DIFF_NT, T, T), lambda b, h, i: (h, 0, 0, 0))
    vec = lambda n: pl.BlockSpec((1, n), lambda b, h, i: (0, 0))
    stat = pltpu.VMEM((1, T), F32)
    acc = pltpu.VMEM((LANES, T), F32)
    return pl.pallas_call(
        functools.partial(_diff_kernel, lambda_init=lambda_init),
        grid=(B, N_HEADS_B, S // T),
        in_specs=[pl.BlockSpec(memory_space=pltpu.SMEM), qspec, kspec, kspec, vspec, band_spec,
                  vec(HEAD_DIM), vec(HEAD_DIM), vec(HEAD_DIM), vec(HEAD_DIM),
                  pl.BlockSpec((LANES, 1), lambda b, h, i: (0, 0))],
        out_specs=qspec,
        out_shape=jax.ShapeDtypeStruct((B, S, W), BF16),
        scratch_shapes=[stat, stat, acc, stat, stat, acc, pltpu.VMEM((2, T, T), F32), pltpu.VMEM((2, T, T), F32)],
        compiler_params=_params("arbitrary", "arbitrary", "arbitrary"),
        name="diff_attn",
    )(rel_bias, qb, k1, k2, vbt, band, lq1, lk1, lq2, lk2, g_col)


def _out_proj_kernel(o1_ref, o2_ref, o3_ref, s1_ref, s2_ref, s3_ref, ob_ref, x_ref, gate_ref, shift_ref,
                     scale_ref, g_ref, wa_ref, wb_ref, h_ref, hn_ref):
    s1, s2, s3 = s1_ref[...], s2_ref[...], s3_ref[...]
    mx = jnp.maximum(jnp.maximum(s1, s2), s3)
    e1, e2, e3 = jnp.exp(s1 - mx), jnp.exp(s2 - mx), jnp.exp(s3 - mx)
    oa = (e1 * o1_ref[...].astype(F32) + e2 * o2_ref[...].astype(F32) + e3 * o3_ref[...].astype(F32))
    oa = oa / (e1 + e2 + e3)
    mixed = (jnp.dot(oa.astype(BF16), wa_ref[...], preferred_element_type=F32)
             + jnp.dot(ob_ref[...], wb_ref[...], preferred_element_type=F32))
    h = x_ref[...] + gate_ref[...] * mixed
    h_ref[...] = h
    ms = jnp.mean(h * h, axis=-1, keepdims=True)
    hn = h * lax.rsqrt(ms + EPS) * g_ref[...]
    hn_ref[...] = (hn * (1.0 + scale_ref[...]) + shift_ref[...]).astype(hn_ref.dtype)


def _out_proj(oas, lses, ob, x, gate, shift, scale, g, wa, wb):
    B, S, D = x.shape
    tm = OUT_TM
    half = pl.BlockSpec((None, tm, 512), lambda b, i: (b, i, 0))
    row = pl.BlockSpec((None, tm, D), lambda b, i: (b, i, 0))
    mod = pl.BlockSpec((None, 1, D), lambda b, i: (b, 0, 0))
    const = lambda shape: pl.BlockSpec(shape, lambda b, i: (0,) * len(shape))
    return pl.pallas_call(
        _out_proj_kernel,
        grid=(B, S // tm),
        in_specs=[half] * 7 + [row, mod, mod, mod, const((1, D)), const(wa.shape), const(wb.shape)],
        out_specs=[row, row],
        out_shape=[jax.ShapeDtypeStruct((B, S, D), F32), jax.ShapeDtypeStruct((B, S, D), BF16)],
        compiler_params=_params("arbitrary", "arbitrary"),
        name="out_proj",
    )(*oas, *lses, ob, x, gate, shift, scale, g, wa, wb)


def _ffn_kernel(hp_ref, hc_ref, hn_ref, h_ref, gate_ref, wu_ref, cw_ref, cb_ref, wd_ref, o_ref, lhs):
    i = pl.program_id(1)
    tm = hc_ref.shape[0]
    n = tm + 2 * FFN_HALO
    lhs[0:FFN_HALO, :] = jnp.where(i > 0, hp_ref[...], jnp.zeros_like(hp_ref))
    lhs[FFN_HALO:FFN_HALO + tm, :] = hc_ref[...]
    lhs[FFN_HALO + tm:, :] = jnp.where(i < pl.num_programs(1) - 1, hn_ref[...], jnp.zeros_like(hn_ref))
    x = lhs[...]

    def conv(u, lo):
        cw = cw_ref[:, lo:lo + FFN_FC]
        mid = slice(FFN_HALO, FFN_HALO + tm)
        below = pltpu.roll(u, 1, 0)[mid]
        above = pltpu.roll(u, n - 1, 0)[mid]
        return cw[0:1] * below + cw[1:2] * u[mid] + cw[2:3] * above + cb_ref[:, lo:lo + FFN_FC]

    acc = jnp.zeros((tm, D_MODEL), F32)
    for c in range(D_FF // FFN_FC):
        lo_v = c * FFN_FC
        lo_g = D_FF + c * FFN_FC
        val = conv(jnp.dot(x, wu_ref[:, lo_v:lo_v + FFN_FC], preferred_element_type=F32), lo_v)
        gt = conv(jnp.dot(x, wu_ref[:, lo_g:lo_g + FFN_FC], preferred_element_type=F32), lo_g)
        act = gt / (1.0 + jnp.exp(-gt)) * val
        acc = acc + jnp.dot(act.astype(BF16), wd_ref[lo_v:lo_v + FFN_FC, :], preferred_element_type=F32)
    o_ref[...] = h_ref[...] + gate_ref[...] * acc


def _ffn(hn, h, gate, wu, cw, cb, wd):
    B, S, D = h.shape
    tm = FFN_TM
    per = tm // FFN_HALO
    nh = S // FFN_HALO
    row = lambda dt: pl.BlockSpec((None, tm, D), lambda b, i: (b, i, 0))
    prev = pl.BlockSpec((None, FFN_HALO, D), lambda b, i: (b, jnp.maximum(i * per - 1, 0), 0))
    nxt = pl.BlockSpec((None, FFN_HALO, D), lambda b, i: (b, jnp.minimum((i + 1) * per, nh - 1), 0))
    mod = pl.BlockSpec((None, 1, D), lambda b, i: (b, 0, 0))
    const = lambda shape: pl.BlockSpec(shape, lambda b, i: (0,) * len(shape), pipeline_mode=pl.Buffered(1))
    return pl.pallas_call(
        _ffn_kernel,
        grid=(B, S // tm),
        in_specs=[prev, row(BF16), nxt, row(F32), mod, const(wu.shape), const(cw.shape), const(cb.shape),
                  const(wd.shape)],
        out_specs=row(F32),
        out_shape=jax.ShapeDtypeStruct((B, S, D), F32),
        scratch_shapes=[pltpu.VMEM((tm + 2 * FFN_HALO, D), BF16)],
        compiler_params=_params("arbitrary", "arbitrary"),
        name="conv_ffn",
    )(hn, hn, hn, h, gate, wu, cw, cb, wd)


def _block_diag_ones():
    r = jnp.arange(MXU_DIM) // HEAD_DIM
    return (r[:, None] == r[None, :]).astype(BF16)


def kernel(x, c, w_ada, b_ada, norm1_g, w_in, q_norm_a, k_norm_a, q_norm_b, k_norm_b, rel_bias, lambda_q1,
           lambda_k1, lambda_q2, lambda_k2, subln_g, w_out, norm2_g, w_up, conv_w, conv_b, w_down):
    B, S, D = x.shape
    depth = w_ada.shape[0]
    h = x.astype(F32)
    c8 = jnp.pad(c.astype(F32), ((0, 8 - B), (0, 0)))
    bd = _block_diag_ones()
    rel_bias = rel_bias.astype(F32)
    band_b = _diff_bias(rel_bias)
    band_a = _dil_bias(rel_bias)
    qscale = HEAD_DIM ** -0.5
    tile8 = lambda g: jnp.tile(g.astype(F32), WIDTH_A // HEAD_DIM)[None, :]
    row = lambda v: v.astype(F32)[None, :]

    for layer in range(depth):
        lambda_init = 0.8 - 0.6 * math.exp(-0.3 * layer)
        mod = _ada(c8, w_ada[layer].astype(F32), row(b_ada[layer]))[:B]
        shift1, scale1, gate1, shift2, scale2, gate2 = [m[:, None, :] for m in jnp.split(mod, 6, axis=-1)]

        qa, ka, va, qb, k1, k2, vbt = _in_proj(
            h, shift1, scale1, row(norm1_g[layer]), w_in[layer].astype(BF16), bd,
            tile8(q_norm_a[layer]) * qscale, tile8(k_norm_a[layer]),
            tile8(q_norm_b[layer]) * (qscale * LOG2E), tile8(k_norm_b[layer]))

        oas, lses = [], []
        for p, (_, dil) in enumerate(DILATED_PATTERNS):
            o, lse = _dilated(qa, ka, va, band_a[p], dil)
            oas.append(o)
            lses.append(lse)

        ob = _diff_attention(rel_bias, qb, k1, k2, vbt, band_b, row(lambda_q1[layer]), row(lambda_k1[layer]),
                             row(lambda_q2[layer]), row(lambda_k2[layer]),
                             subln_g[layer].astype(F32)[:, None], lambda_init)

        w_o = w_out[layer].astype(BF16)
        h, hn = _out_proj(oas, lses, ob, h, gate1, shift2, scale2, row(norm2_g[layer]),
                          w_o[:WIDTH_A], w_o[WIDTH_A:])
        h = _ffn(hn, h, gate2, w_up[layer].astype(BF16), conv_w[layer].astype(F32), row(conv_b[layer]),
                 w_down[layer].astype(BF16))

    return h.astype(x.dtype)
```

```python
import functools
import math

import jax
import jax.numpy as jnp
from jax import lax
from jax.experimental import pallas as pl
from jax.experimental.pallas import tpu as pltpu

F32 = jnp.float32
BF16 = jnp.bfloat16

D_MODEL = 1024
HEAD_DIM = 64
WIDTH_A = 512
WIDTH_B = 512
N_HEADS_A = 8
N_HEADS_B = 4
DILATED_PATTERNS = ((128, 1), (512, 4), (2048, 16))
D_FF = 2816
NUM_BUCKETS = 32
MAX_DISTANCE = 2048
EPS = 1e-6
NEG = -1e30
LOG2E = math.log2(math.e)

LANES = 128
BF16_SUBLANES = 16
MXU_DIM = 256
VMEM_LIMIT = 56 * 1024 * 1024

HALF_WIN = 64
DIL_Q = 128
DIL_K = DIL_Q + 2 * HALF_WIN
DIL_TL = 512

DIFF_T = 512
BUCKET_SAT = 1024
DIFF_NB = BUCKET_SAT // DIFF_T
DIFF_NT = 2 * DIFF_NB + 3

PROJ_TM = 512
OUT_TM = 512
FFN_TM = 512
FFN_FC = 256
FFN_HALO = BF16_SUBLANES

_BUCKET_THRESHOLDS = (1, 2, 3, 4, 5, 6, 7, 8, 16, 32, 64, 128, 256, 512, 1024)


def _params(*sem):
    return pltpu.CompilerParams(dimension_semantics=sem, vmem_limit_bytes=VMEM_LIMIT)


def _ada_kernel(c_ref, w_ref, b_ref, o_ref):
    c = c_ref[...]
    ca = c / (1.0 + jnp.exp(-c))
    o_ref[...] = jnp.dot(ca, w_ref[...], preferred_element_type=F32) + b_ref[...]


def _ada(c8, w, b):
    n = w.shape[1]
    tn = 1024
    return pl.pallas_call(
        _ada_kernel,
        grid=(n // tn,),
        in_specs=[pl.BlockSpec((8, D_MODEL), lambda j: (0, 0)),
                  pl.BlockSpec((D_MODEL, tn), lambda j: (0, j)),
                  pl.BlockSpec((1, tn), lambda j: (0, j))],
        out_specs=pl.BlockSpec((8, tn), lambda j: (0, j)),
        out_shape=jax.ShapeDtypeStruct((8, n), F32),
        compiler_params=_params("arbitrary"),
        name="ada",
    )(c8, w, b)


def _head_norm(p, bd, gain):
    ss = jnp.dot((p * p).astype(BF16), bd, preferred_element_type=F32)
    return p * lax.rsqrt(ss * (1.0 / HEAD_DIM) + EPS) * gain


def _in_proj_kernel(x_ref, shift_ref, scale_ref, g_ref, w_ref, bd_ref, gqa_ref, gka_ref, gqb_ref, gkb_ref,
                    qa_ref, ka_ref, va_ref, qa4_ref, ka4_ref, va4_ref, qa16_ref, ka16_ref, va16_ref,
                    qb_ref, k1_ref, k2_ref, vbt_ref, pbuf, stage):
    tm = x_ref.shape[0]
    x = x_ref[...]
    ms = jnp.mean(x * x, axis=-1, keepdims=True)
    hn = x * lax.rsqrt(ms + EPS) * g_ref[...]
    hn = (hn * (1.0 + scale_ref[...]) + shift_ref[...]).astype(BF16)
    pbuf[...] = jnp.dot(hn, w_ref[...], preferred_element_type=F32)
    bd = bd_ref[...]
    lane = lax.broadcasted_iota(jnp.int32, (1, MXU_DIM), 1)
    first = (lane // HEAD_DIM) % 2 == 0

    def proj(g, c):
        lo = g * 512 + c * MXU_DIM
        return pbuf[:, lo:lo + MXU_DIM]

    for c in range(2):
        sl = slice(c * MXU_DIM, (c + 1) * MXU_DIM)
        group_a = (_head_norm(proj(0, c), bd, gqa_ref[:, sl]), _head_norm(proj(1, c), bd, gka_ref[:, sl]),
                   proj(2, c))
        for a, (val, ref) in enumerate(zip(group_a, (qa_ref, ka_ref, va_ref))):
            ref[:, sl] = val.astype(BF16)
            stage[a, 2 * c] = val[:, :LANES]
            stage[a, 2 * c + 1] = val[:, LANES:]
        qb_ref[:, sl] = _head_norm(proj(3, c), bd, gqb_ref[:, sl]).astype(BF16)
        kb = _head_norm(proj(4, c), bd, gkb_ref[:, sl])
        k1_ref[:, sl] = jnp.where(first, kb, 0.0).astype(BF16)
        k2_ref[:, sl] = jnp.where(first, 0.0, kb).astype(BF16)
        vbt_ref[sl, :] = proj(5, c).T.astype(BF16)

    by_residue = ((4, (qa4_ref, ka4_ref, va4_ref)), (16, (qa16_ref, ka16_ref, va16_ref)))
    for dil, refs in by_residue:
        rows = tm // dil
        for a, ref in enumerate(refs):
            for r in range(dil):
                for s in range(WIDTH_A // LANES):
                    lo = r * WIDTH_A + s * LANES
                    ref[:, lo:lo + LANES] = stage[a, s, pl.ds(r, rows, stride=dil), :].astype(BF16)


def _in_proj(x, shift, scale, g, w_bf, bd, gqa, gka, gqb, gkb):
    B, S, D = x.shape
    tm = PROJ_TM
    row = pl.BlockSpec((None, tm, D), lambda b, i: (b, i, 0))
    mod = pl.BlockSpec((None, 1, D), lambda b, i: (b, 0, 0))
    const = lambda shape: pl.BlockSpec(shape, lambda b, i: (0,) * len(shape))
    out = pl.BlockSpec((None, tm, 512), lambda b, i: (b, i, 0))
    out_t = pl.BlockSpec((None, 512, tm), lambda b, i: (b, 0, i))
    res = lambda dil: pl.BlockSpec((None, tm // dil, dil * WIDTH_A), lambda b, i: (b, i, 0))
    res_shape = lambda dil: jax.ShapeDtypeStruct((B, S // dil, dil * WIDTH_A), BF16)
    nat_shape = jax.ShapeDtypeStruct((B, S, 512), BF16)
    return pl.pallas_call(
        _in_proj_kernel,
        grid=(B, S // tm),
        in_specs=[row, mod, mod, const((1, D)), const(w_bf.shape), const(bd.shape),
                  const((1, 512)), const((1, 512)), const((1, 512)), const((1, 512))],
        out_specs=[out] * 3 + [res(4)] * 3 + [res(16)] * 3 + [out] * 3 + [out_t],
        out_shape=[nat_shape] * 3 + [res_shape(4)] * 3 + [res_shape(16)] * 3 + [nat_shape] * 3
                  + [jax.ShapeDtypeStruct((B, 512, S), BF16)],
        scratch_shapes=[pltpu.VMEM((tm, w_bf.shape[1]), F32),
                        pltpu.VMEM((3, WIDTH_A // LANES, tm, LANES), F32)],
        compiler_params=_params("arbitrary", "arbitrary"),
        name="in_proj",
    )(x, shift, scale, g, w_bf, bd, gqa, gka, gqb, gkb)


def _bias_from_rel(rel, tbl_ref, h):
    n = jnp.abs(rel)
    vneg = jnp.full(rel.shape, tbl_ref[0, h], F32)
    vpos = jnp.full(rel.shape, tbl_ref[NUM_BUCKETS // 2, h], F32)
    for i, th in enumerate(_BUCKET_THRESHOLDS):
        ge = n >= th
        vneg = jnp.where(ge, tbl_ref[i + 1, h], vneg)
        vpos = jnp.where(ge, tbl_ref[NUM_BUCKETS // 2 + i + 1, h], vpos)
    return jnp.where(rel > 0, vpos, vneg)


def _diff_bias_kernel(tbl_ref, o_ref):
    h = pl.program_id(0)
    t = pl.program_id(1)
    shape = (DIFF_T, DIFF_T)
    k = lax.broadcasted_iota(jnp.int32, shape, 0)
    q = lax.broadcasted_iota(jnp.int32, shape, 1)
    rel = k - q + (t - (DIFF_NB + 1)) * DIFF_T
    o_ref[...] = _bias_from_rel(rel, tbl_ref, N_HEADS_A + h) * LOG2E


def _diff_bias(rel_bias):
    return pl.pallas_call(
        _diff_bias_kernel,
        grid=(N_HEADS_B, DIFF_NT),
        in_specs=[pl.BlockSpec(memory_space=pltpu.SMEM)],
        out_specs=pl.BlockSpec((None, None, DIFF_T, DIFF_T), lambda h, t: (h, t, 0, 0)),
        out_shape=jax.ShapeDtypeStruct((N_HEADS_B, DIFF_NT, DIFF_T, DIFF_T), F32),
        compiler_params=_params("arbitrary", "arbitrary"),
        name="diff_bias",
    )(rel_bias)


def _dil_bias_kernel(tbl_ref, o_ref):
    p = pl.program_id(0)
    h = pl.program_id(1)
    shape = (DIL_Q, DIL_K)
    q = lax.broadcasted_iota(jnp.int32, shape, 0)
    k = lax.broadcasted_iota(jnp.int32, shape, 1)
    steps = k - HALF_WIN - q
    dil = lax.shift_left(jnp.int32(1), 2 * p)
    bias = _bias_from_rel(steps * dil, tbl_ref, h)
    o_ref[...] = jnp.where(jnp.abs(steps) <= HALF_WIN, bias, NEG)


def _dil_bias(rel_bias):
    n_pat = len(DILATED_PATTERNS)
    return pl.pallas_call(
        _dil_bias_kernel,
        grid=(n_pat, N_HEADS_A),
        in_specs=[pl.BlockSpec(memory_space=pltpu.SMEM)],
        out_specs=pl.BlockSpec((None, None, DIL_Q, DIL_K), lambda p, h: (p, h, 0, 0)),
        out_shape=jax.ShapeDtypeStruct((n_pat, N_HEADS_A, DIL_Q, DIL_K), F32),
        compiler_params=_params("arbitrary", "arbitrary"),
        name="dil_bias",
    )(rel_bias)


def _dil_kernel(q_ref, kp_ref, kc_ref, kn_ref, vp_ref, vc_ref, vn_ref, bias_ref, o_ref, lse_ref, kbuf, vbuf,
                *, seq_len):
    tl = q_ref.shape[0]
    i = pl.program_id(2)
    kbuf[0:HALF_WIN, :] = kp_ref[...]
    kbuf[HALF_WIN:HALF_WIN + tl, :] = kc_ref[...]
    kbuf[HALF_WIN + tl:, :] = kn_ref[...]
    vbuf[0:HALF_WIN, :] = vp_ref[...]
    vbuf[HALF_WIN:HALF_WIN + tl, :] = vc_ref[...]
    vbuf[HALF_WIN + tl:, :] = vn_ref[...]
    lane = lax.broadcasted_iota(jnp.int32, (1, LANES), 1)
    even = lane < HEAD_DIM
    col = lax.broadcasted_iota(jnp.int32, (1, DIL_K), 1)
    contract_last = (((1,), (1,)), ((), ()))

    def sub_block(sb, carry):
        r0 = pl.multiple_of(sb * DIL_Q, DIL_Q)
        kpos = i * tl + r0 - HALF_WIN + col
        colpen = jnp.where((kpos >= 0) & (kpos < seq_len), 0.0, NEG)
        for hp in range(N_HEADS_A // 2):
            ls = slice(hp * LANES, (hp + 1) * LANES)
            q = q_ref[pl.ds(r0, DIL_Q), ls]
            kw = kbuf[pl.ds(r0, DIL_K), ls]
            vw = vbuf[pl.ds(r0, DIL_K), ls]
            outs, lses = [], []
            for par in range(2):
                keep = even if par == 0 else jnp.logical_not(even)
                kz = jnp.where(keep, kw, jnp.zeros_like(kw))
                s = lax.dot_general(q, kz, contract_last, preferred_element_type=F32)
                s = s + bias_ref[2 * hp + par] + colpen
                m = jnp.max(s, axis=-1, keepdims=True)
                p = jnp.exp(s - m)
                l = jnp.sum(p, axis=-1, keepdims=True)
                o = jnp.dot(p.astype(BF16), vw, preferred_element_type=F32)
                outs.append(o / l)
                lses.append(m + jnp.log(l))
            o_ref[pl.ds(r0, DIL_Q), ls] = jnp.where(even, outs[0], outs[1]).astype(o_ref.dtype)
            lse_ref[pl.ds(r0, DIL_Q), ls] = jnp.where(even, lses[0], lses[1])
        return carry

    lax.fori_loop(0, tl // DIL_Q, sub_block, 0)


def _dilated(q, k, v, bias, dil):
    B, L, _ = q.shape
    W = WIDTH_A
    tl = min(DIL_TL, L)
    nhalo = L // HALF_WIN
    cur = pl.BlockSpec((None, tl, W), lambda b, r, i: (b, i, r))
    prev = pl.BlockSpec((None, HALF_WIN, W),
                        lambda b, r, i: (b, jnp.maximum(i * (tl // HALF_WIN) - 1, 0), r))
    nxt = pl.BlockSpec((None, HALF_WIN, W),
                       lambda b, r, i: (b, jnp.minimum((i + 1) * (tl // HALF_WIN), nhalo - 1), r))
    bias_spec = pl.BlockSpec(bias.shape, lambda b, r, i: (0, 0, 0))
    return pl.pallas_call(
        functools.partial(_dil_kernel, seq_len=L),
        grid=(B, dil, L // tl),
        in_specs=[cur, prev, cur, nxt, prev, cur, nxt, bias_spec],
        out_specs=[cur, cur],
        out_shape=[jax.ShapeDtypeStruct((B, L, dil * W), BF16), jax.ShapeDtypeStruct((B, L, dil * W), F32)],
        scratch_shapes=[pltpu.VMEM((tl + 2 * HALF_WIN, W), BF16), pltpu.VMEM((tl + 2 * HALF_WIN, W), BF16)],
        compiler_params=_params("arbitrary", "arbitrary", "arbitrary"),
        name=f"dilated_{dil}",
    )(q, k, k, k, v, v, v, bias)


def _diff_kernel(tbl_ref, q_ref, k1_ref, k2_ref, vt_ref, band_ref, lq1_ref, lk1_ref, lq2_ref, lk2_ref, g_ref,
                 o_ref, m1, l1, a1, m2, l2, a2, sa, sb, *, lambda_init):
    h = pl.program_id(1)
    i = pl.program_id(2)
    T = q_ref.shape[0]
    nk = k1_ref.shape[0] // T
    streams = ((k1_ref, m1, l1, a1), (k2_ref, m2, l2, a2))
    for _, m, l, a in streams:
        m[...] = jnp.full(m.shape, NEG, F32)
        l[...] = jnp.zeros(l.shape, F32)
        a[...] = jnp.zeros(a.shape, F32)
    q = q_ref[...]
    contract_last = (((1,), (1,)), ((), ()))

    def scores(j, sbuf, st):
        r0 = pl.multiple_of(j * T, T)
        sbuf[st] = lax.dot_general(streams[st][0][pl.ds(r0, T), :], q, contract_last,
                                   preferred_element_type=F32)

    def softmax_pv(j, sbuf, st, near, shift):
        r0 = pl.multiple_of(j * T, T)
        _, m, l, a = streams[st]
        s = sbuf[st]
        if near:
            s = s + band_ref[jnp.clip(j - i, -(DIFF_NB + 1), DIFF_NB + 1) + (DIFF_NB + 1)]
        m_prev = m[...]
        m_new = jnp.maximum(m_prev, jnp.max(s, axis=0, keepdims=True) + shift)
        alpha = jnp.exp2(m_prev - m_new)
        p = jnp.exp2(s - (m_new - shift))
        l[...] = alpha * l[...] + jnp.sum(p, axis=0, keepdims=True)
        a[...] = alpha * a[...] + jnp.dot(vt_ref[:, pl.ds(r0, T)], p.astype(BF16), preferred_element_type=F32)
        m[...] = m_new

    def pair(near, shift):
        def body(pr, carry):
            j0 = 2 * pr
            j2 = jnp.minimum(j0 + 2, nk - 1)
            for st in range(2):
                scores(j0 + 1, sb, st)
                softmax_pv(j0, sa, st, near, shift)
            for st in range(2):
                scores(j2, sa, st)
                softmax_pv(j0 + 1, sb, st, near, shift)
            return carry
        return body

    scores(0, sa, 0)
    scores(0, sa, 1)

    lo = jnp.maximum((i - DIFF_NB) // 2, 0)
    hi = jnp.minimum((i + DIFF_NB) // 2 + 1, nk // 2)
    c_neg = tbl_ref[NUM_BUCKETS // 2 - 1, N_HEADS_A + h] * LOG2E
    c_pos = tbl_ref[NUM_BUCKETS - 1, N_HEADS_A + h] * LOG2E
    lax.fori_loop(0, lo, pair(False, c_neg), 0)
    lax.fori_loop(lo, hi, pair(True, 0.0), 0)
    lax.fori_loop(hi, nk // 2, pair(False, c_pos), 0)

    lam = (jnp.exp(jnp.sum(lq1_ref[...] * lk1_ref[...], axis=-1, keepdims=True))
           - jnp.exp(jnp.sum(lq2_ref[...] * lk2_ref[...], axis=-1, keepdims=True)) + lambda_init)
    o = a1[...] / l1[...] - lam * (a2[...] / l2[...])
    ms = jnp.mean(o * o, axis=0, keepdims=True)
    o = o * lax.rsqrt(ms + EPS) * (g_ref[...] * (1.0 - lambda_init))
    o_ref[...] = o.T.astype(o_ref.dtype)


def _diff_attention(rel_bias, qb, k1, k2, vbt, band, lq1, lk1, lq2, lk2, g_col, lambda_init):
    B, S, W = qb.shape
    T = DIFF_T
    qspec = pl.BlockSpec((None, T, LANES), lambda b, h, i: (b, i, h))
    kspec = pl.BlockSpec((None, S, LANES), lambda b, h, i: (b, 0, h))
    vspec = pl.BlockSpec((None, LANES, S), lambda b, h, i: (b, h, 0))
    band_spec = pl.BlockSpec((None, DIFF_NT, T, T), lambda b, h, i: (h, 0, 0, 0))
    vec = lambda n: pl.BlockSpec((1, n), lambda b, h, i: (0, 0))
    stat = pltpu.VMEM((1, T), F32)
    acc = pltpu.VMEM((LANES, T), F32)
    return pl.pallas_call(
        functools.partial(_diff_kernel, lambda_init=lambda_init),
        grid=(B, N_HEADS_B, S // T),
        in_specs=[pl.BlockSpec(memory_space=pltpu.SMEM), qspec, kspec, kspec, vspec, band_spec,
                  vec(HEAD_DIM), vec(HEAD_DIM), vec(HEAD_DIM), vec(HEAD_DIM),
                  pl.BlockSpec((LANES, 1), lambda b, h, i: (0, 0))],
        out_specs=qspec,
        out_shape=jax.ShapeDtypeStruct((B, S, W), BF16),
        scratch_shapes=[stat, stat, acc, stat, stat, acc, pltpu.VMEM((2, T, T), F32), pltpu.VMEM((2, T, T), F32)],
        compiler_params=_params("arbitrary", "arbitrary", "arbitrary"),
        name="diff_attn",
    )(rel_bias, qb, k1, k2, vbt, band, lq1, lk1, lq2, lk2, g_col)


def _out_proj_kernel(o1_ref, o4_ref, o16_ref, s1_ref, s4_ref, s16_ref, ob_ref, x_ref, gate_ref, shift_ref,
                     scale_ref, g_ref, wa_ref, wb_ref, h_ref, hn_ref, nat, oa_buf):
    tm = x_ref.shape[0]
    for n, (dil, ref) in enumerate(((4, o4_ref), (4, s4_ref), (16, o16_ref), (16, s16_ref))):
        rows = tm // dil
        for r in range(dil):
            for s in range(WIDTH_A // LANES):
                lo = r * WIDTH_A + s * LANES
                nat[n, s, pl.ds(r, rows, stride=dil), :] = ref[:, lo:lo + LANES].astype(F32)
    for s in range(WIDTH_A // LANES):
        sl = slice(s * LANES, (s + 1) * LANES)
        s1, s2, s3 = s1_ref[:, sl], nat[1, s], nat[3, s]
        mx = jnp.maximum(jnp.maximum(s1, s2), s3)
        e1, e2, e3 = jnp.exp(s1 - mx), jnp.exp(s2 - mx), jnp.exp(s3 - mx)
        oa = (e1 * o1_ref[:, sl].astype(F32) + e2 * nat[0, s] + e3 * nat[2, s]) / (e1 + e2 + e3)
        oa_buf[:, sl] = oa.astype(BF16)
    mixed = (jnp.dot(oa_buf[...], wa_ref[...], preferred_element_type=F32)
             + jnp.dot(ob_ref[...], wb_ref[...], preferred_element_type=F32))
    h = x_ref[...] + gate_ref[...] * mixed
    h_ref[...] = h
    ms = jnp.mean(h * h, axis=-1, keepdims=True)
    hn = h * lax.rsqrt(ms + EPS) * g_ref[...]
    hn_ref[...] = (hn * (1.0 + scale_ref[...]) + shift_ref[...]).astype(hn_ref.dtype)


def _out_proj(oas, lses, ob, x, gate, shift, scale, g, wa, wb):
    B, S, D = x.shape
    tm = OUT_TM
    half = pl.BlockSpec((None, tm, 512), lambda b, i: (b, i, 0))
    row = pl.BlockSpec((None, tm, D), lambda b, i: (b, i, 0))
    mod = pl.BlockSpec((None, 1, D), lambda b, i: (b, 0, 0))
    const = lambda shape: pl.BlockSpec(shape, lambda b, i: (0,) * len(shape))
    res = lambda dil: pl.BlockSpec((None, tm // dil, dil * WIDTH_A), lambda b, i: (b, i, 0))
    pats = [half, res(4), res(16)]
    return pl.pallas_call(
        _out_proj_kernel,
        grid=(B, S // tm),
        in_specs=pats + pats + [half, row, mod, mod, mod, const((1, D)), const(wa.shape), const(wb.shape)],
        out_specs=[row, row],
        out_shape=[jax.ShapeDtypeStruct((B, S, D), F32), jax.ShapeDtypeStruct((B, S, D), BF16)],
        scratch_shapes=[pltpu.VMEM((4, WIDTH_A // LANES, tm, LANES), F32), pltpu.VMEM((tm, WIDTH_A), BF16)],
        compiler_params=_params("arbitrary", "arbitrary"),
        name="out_proj",
    )(*oas, *lses, ob, x, gate, shift, scale, g, wa, wb)


def _ffn_kernel(hp_ref, hc_ref, hn_ref, h_ref, gate_ref, wu_ref, cw_ref, cb_ref, wd_ref, o_ref, lhs, act):
    i = pl.program_id(1)
    tm = hc_ref.shape[0]
    n = tm + 2 * FFN_HALO
    lhs[0:FFN_HALO, :] = jnp.where(i > 0, hp_ref[...], jnp.zeros_like(hp_ref))
    lhs[FFN_HALO:FFN_HALO + tm, :] = hc_ref[...]
    lhs[FFN_HALO + tm:, :] = jnp.where(i < pl.num_programs(1) - 1, hn_ref[...], jnp.zeros_like(hn_ref))
    x = lhs[...]

    def conv(u, lo):
        cw = cw_ref[:, lo:lo + FFN_FC]
        mid = slice(FFN_HALO, FFN_HALO + tm)
        below = pltpu.roll(u, 1, 0)[mid]
        above = pltpu.roll(u, n - 1, 0)[mid]
        return cw[0:1] * below + cw[1:2] * u[mid] + cw[2:3] * above + cb_ref[:, lo:lo + FFN_FC]

    for c in range(D_FF // FFN_FC):
        lo_v = c * FFN_FC
        lo_g = D_FF + c * FFN_FC
        val = conv(jnp.dot(x, wu_ref[:, lo_v:lo_v + FFN_FC], preferred_element_type=F32), lo_v)
        gt = conv(jnp.dot(x, wu_ref[:, lo_g:lo_g + FFN_FC], preferred_element_type=F32), lo_g)
        act[:, lo_v:lo_v + FFN_FC] = (gt / (1.0 + jnp.exp(-gt)) * val).astype(BF16)
    down = jnp.dot(act[...], wd_ref[...], preferred_element_type=F32)
    o_ref[...] = h_ref[...] + gate_ref[...] * down


def _ffn(hn, h, gate, wu, cw, cb, wd):
    B, S, D = h.shape
    tm = FFN_TM
    per = tm // FFN_HALO
    nh = S // FFN_HALO
    row = lambda dt: pl.BlockSpec((None, tm, D), lambda b, i: (b, i, 0))
    prev = pl.BlockSpec((None, FFN_HALO, D), lambda b, i: (b, jnp.maximum(i * per - 1, 0), 0))
    nxt = pl.BlockSpec((None, FFN_HALO, D), lambda b, i: (b, jnp.minimum((i + 1) * per, nh - 1), 0))
    mod = pl.BlockSpec((None, 1, D), lambda b, i: (b, 0, 0))
    const = lambda shape: pl.BlockSpec(shape, lambda b, i: (0,) * len(shape), pipeline_mode=pl.Buffered(1))
    return pl.pallas_call(
        _ffn_kernel,
        grid=(B, S // tm),
        in_specs=[prev, row(BF16), nxt, row(F32), mod, const(wu.shape), const(cw.shape), const(cb.shape),
                  const(wd.shape)],
        out_specs=row(F32),
        out_shape=jax.ShapeDtypeStruct((B, S, D), F32),
        scratch_shapes=[pltpu.VMEM((tm + 2 * FFN_HALO, D), BF16), pltpu.VMEM((tm, D_FF), BF16)],
        compiler_params=_params("arbitrary", "arbitrary"),
        name="conv_ffn",
    )(hn, hn, hn, h, gate, wu, cw, cb, wd)


def _block_diag_ones():
    r = jnp.arange(MXU_DIM) // HEAD_DIM
    return (r[:, None] == r[None, :]).astype(BF16)


def kernel(x, c, w_ada, b_ada, norm1_g, w_in, q_norm_a, k_norm_a, q_norm_b, k_norm_b, rel_bias, lambda_q1,
           lambda_k1, lambda_q2, lambda_k2, subln_g, w_out, norm2_g, w_up, conv_w, conv_b, w_down):
    B, S, D = x.shape
    depth = w_ada.shape[0]
    h = x.astype(F32)
    c8 = jnp.pad(c.astype(F32), ((0, 8 - B), (0, 0)))
    bd = _block_diag_ones()
    rel_bias = rel_bias.astype(F32)
    band_b = _diff_bias(rel_bias)
    band_a = _dil_bias(rel_bias)
    qscale = HEAD_DIM ** -0.5
    tile8 = lambda g: jnp.tile(g.astype(F32), WIDTH_A // HEAD_DIM)[None, :]
    row = lambda v: v.astype(F32)[None, :]

    for layer in range(depth):
        lambda_init = 0.8 - 0.6 * math.exp(-0.3 * layer)
        mod = _ada(c8, w_ada[layer].astype(F32), row(b_ada[layer]))[:B]
        shift1, scale1, gate1, shift2, scale2, gate2 = [m[:, None, :] for m in jnp.split(mod, 6, axis=-1)]

        qa, ka, va, qa4, ka4, va4, qa16, ka16, va16, qb, k1, k2, vbt = _in_proj(
            h, shift1, scale1, row(norm1_g[layer]), w_in[layer].astype(BF16), bd,
            tile8(q_norm_a[layer]) * qscale, tile8(k_norm_a[layer]),
            tile8(q_norm_b[layer]) * (qscale * LOG2E), tile8(k_norm_b[layer]))

        oas, lses = [], []
        qkv = {1: (qa, ka, va), 4: (qa4, ka4, va4), 16: (qa16, ka16, va16)}
        for p, (_, dil) in enumerate(DILATED_PATTERNS):
            o, lse = _dilated(*qkv[dil], band_a[p], dil)
            oas.append(o)
            lses.append(lse)

        ob = _diff_attention(rel_bias, qb, k1, k2, vbt, band_b, row(lambda_q1[layer]), row(lambda_k1[layer]),
                             row(lambda_q2[layer]), row(lambda_k2[layer]),
                             subln_g[layer].astype(F32)[:, None], lambda_init)

        w_o = w_out[layer].astype(BF16)
        h, hn = _out_proj(oas, lses, ob, h, gate1, shift2, scale2, row(norm2_g[layer]),
                          w_o[:WIDTH_A], w_o[WIDTH_A:])
        h = _ffn(hn, h, gate2, w_up[layer].astype(BF16), conv_w[layer].astype(F32), row(conv_b[layer]),
                 w_down[layer].astype(BF16))

    return h.astype(x.dtype)
```

```python
import functools
import math

import jax
import jax.numpy as jnp
from jax import lax
from jax.experimental import pallas as pl
from jax.experimental.pallas import tpu as pltpu

F32 = jnp.float32
BF16 = jnp.bfloat16

D_MODEL = 1024
HEAD_DIM = 64
WIDTH_A = 512
WIDTH_B = 512
N_HEADS_A = 8
N_HEADS_B = 4
DILATED_PATTERNS = ((128, 1), (512, 4), (2048, 16))
D_FF = 2816
NUM_BUCKETS = 32
MAX_DISTANCE = 2048
EPS = 1e-6
NEG = -1e30
LOG2E = math.log2(math.e)

LANES = 128
BF16_SUBLANES = 16
MXU_DIM = 256
VMEM_LIMIT = 56 * 1024 * 1024

HALF_WIN = 64
DIL_Q = 128
DIL_K = DIL_Q + 2 * HALF_WIN
DIL_TL = 512

DIFF_T = 512
BUCKET_SAT = 1024
DIFF_NB = BUCKET_SAT // DIFF_T
DIFF_NT = 2 * DIFF_NB + 3
DIFF_GROUP = 4

PROJ_TM = 512
OUT_TM = 512
FFN_TM = 512
FFN_FC = 256
FFN_HALO = BF16_SUBLANES

_BUCKET_THRESHOLDS = (1, 2, 3, 4, 5, 6, 7, 8, 16, 32, 64, 128, 256, 512, 1024)


def _params(*sem):
    return pltpu.CompilerParams(dimension_semantics=sem, vmem_limit_bytes=VMEM_LIMIT)


def _ada_kernel(c_ref, w_ref, b_ref, o_ref):
    c = c_ref[...]
    ca = c / (1.0 + jnp.exp(-c))
    o_ref[...] = jnp.dot(ca, w_ref[...], preferred_element_type=F32) + b_ref[...]


def _ada(c8, w, b):
    n = w.shape[1]
    tn = 1024
    return pl.pallas_call(
        _ada_kernel,
        grid=(n // tn,),
        in_specs=[pl.BlockSpec((8, D_MODEL), lambda j: (0, 0)),
                  pl.BlockSpec((D_MODEL, tn), lambda j: (0, j)),
                  pl.BlockSpec((1, tn), lambda j: (0, j))],
        out_specs=pl.BlockSpec((8, tn), lambda j: (0, j)),
        out_shape=jax.ShapeDtypeStruct((8, n), F32),
        compiler_params=_params("arbitrary"),
        name="ada",
    )(c8, w, b)


def _head_norm(p, bd, gain):
    ss = jnp.dot((p * p).astype(BF16), bd, preferred_element_type=F32)
    return p * lax.rsqrt(ss * (1.0 / HEAD_DIM) + EPS) * gain


def _in_proj_kernel(x_ref, shift_ref, scale_ref, g_ref, w_ref, bd_ref, gqa_ref, gka_ref, gqb_ref, gkb_ref,
                    qa_ref, ka_ref, va_ref, qa4_ref, ka4_ref, va4_ref, qa16_ref, ka16_ref, va16_ref,
                    qb_ref, k1_ref, k2_ref, vbt_ref, pbuf, stage):
    tm = x_ref.shape[0]
    x = x_ref[...]
    ms = jnp.mean(x * x, axis=-1, keepdims=True)
    hn = x * lax.rsqrt(ms + EPS) * g_ref[...]
    hn = (hn * (1.0 + scale_ref[...]) + shift_ref[...]).astype(BF16)
    pbuf[...] = jnp.dot(hn, w_ref[...], preferred_element_type=F32)
    bd = bd_ref[...]
    lane = lax.broadcasted_iota(jnp.int32, (1, MXU_DIM), 1)
    first = (lane // HEAD_DIM) % 2 == 0

    def proj(g, c):
        lo = g * 512 + c * MXU_DIM
        return pbuf[:, lo:lo + MXU_DIM]

    for c in range(2):
        sl = slice(c * MXU_DIM, (c + 1) * MXU_DIM)
        group_a = (_head_norm(proj(0, c), bd, gqa_ref[:, sl]), _head_norm(proj(1, c), bd, gka_ref[:, sl]),
                   proj(2, c))
        for a, (val, ref) in enumerate(zip(group_a, (qa_ref, ka_ref, va_ref))):
            ref[:, sl] = val.astype(BF16)
            stage[a, 2 * c] = val[:, :LANES]
            stage[a, 2 * c + 1] = val[:, LANES:]
        qb_ref[:, sl] = _head_norm(proj(3, c), bd, gqb_ref[:, sl]).astype(BF16)
        kb = _head_norm(proj(4, c), bd, gkb_ref[:, sl])
        k1_ref[:, sl] = jnp.where(first, kb, 0.0).astype(BF16)
        k2_ref[:, sl] = jnp.where(first, 0.0, kb).astype(BF16)
        vbt_ref[sl, :] = proj(5, c).T.astype(BF16)

    by_residue = ((4, (qa4_ref, ka4_ref, va4_ref)), (16, (qa16_ref, ka16_ref, va16_ref)))
    for dil, refs in by_residue:
        rows = tm // dil
        for a, ref in enumerate(refs):
            for r in range(dil):
                for s in range(WIDTH_A // LANES):
                    lo = r * WIDTH_A + s * LANES
                    ref[:, lo:lo + LANES] = stage[a, s, pl.ds(r, rows, stride=dil), :].astype(BF16)


def _in_proj(x, shift, scale, g, w_bf, bd, gqa, gka, gqb, gkb):
    B, S, D = x.shape
    tm = PROJ_TM
    row = pl.BlockSpec((None, tm, D), lambda b, i: (b, i, 0))
    mod = pl.BlockSpec((None, 1, D), lambda b, i: (b, 0, 0))
    const = lambda shape: pl.BlockSpec(shape, lambda b, i: (0,) * len(shape))
    out = pl.BlockSpec((None, tm, 512), lambda b, i: (b, i, 0))
    out_t = pl.BlockSpec((None, 512, tm), lambda b, i: (b, 0, i))
    res = lambda dil: pl.BlockSpec((None, tm // dil, dil * WIDTH_A), lambda b, i: (b, i, 0))
    res_shape = lambda dil: jax.ShapeDtypeStruct((B, S // dil, dil * WIDTH_A), BF16)
    nat_shape = jax.ShapeDtypeStruct((B, S, 512), BF16)
    return pl.pallas_call(
        _in_proj_kernel,
        grid=(B, S // tm),
        in_specs=[row, mod, mod, const((1, D)), const(w_bf.shape), const(bd.shape),
                  const((1, 512)), const((1, 512)), const((1, 512)), const((1, 512))],
        out_specs=[out] * 3 + [res(4)] * 3 + [res(16)] * 3 + [out] * 3 + [out_t],
        out_shape=[nat_shape] * 3 + [res_shape(4)] * 3 + [res_shape(16)] * 3 + [nat_shape] * 3
                  + [jax.ShapeDtypeStruct((B, 512, S), BF16)],
        scratch_shapes=[pltpu.VMEM((tm, w_bf.shape[1]), F32),
                        pltpu.VMEM((3, WIDTH_A // LANES, tm, LANES), F32)],
        compiler_params=_params("arbitrary", "arbitrary"),
        name="in_proj",
    )(x, shift, scale, g, w_bf, bd, gqa, gka, gqb, gkb)


def _bias_from_rel(rel, tbl_ref, h):
    n = jnp.abs(rel)
    vneg = jnp.full(rel.shape, tbl_ref[0, h], F32)
    vpos = jnp.full(rel.shape, tbl_ref[NUM_BUCKETS // 2, h], F32)
    for i, th in enumerate(_BUCKET_THRESHOLDS):
        ge = n >= th
        vneg = jnp.where(ge, tbl_ref[i + 1, h], vneg)
        vpos = jnp.where(ge, tbl_ref[NUM_BUCKETS // 2 + i + 1, h], vpos)
    return jnp.where(rel > 0, vpos, vneg)


def _diff_bias_kernel(tbl_ref, o_ref, omax_ref):
    h = pl.program_id(0)
    t = pl.program_id(1)
    shape = (DIFF_T, DIFF_T)
    k = lax.broadcasted_iota(jnp.int32, shape, 0)
    q = lax.broadcasted_iota(jnp.int32, shape, 1)
    rel = k - q + (t - (DIFF_NB + 1)) * DIFF_T
    bias = _bias_from_rel(rel, tbl_ref, N_HEADS_A + h) * LOG2E
    o_ref[...] = bias
    omax_ref[...] = jnp.max(bias, axis=0, keepdims=True)


def _diff_bias(rel_bias):
    return pl.pallas_call(
        _diff_bias_kernel,
        grid=(N_HEADS_B, DIFF_NT),
        in_specs=[pl.BlockSpec(memory_space=pltpu.SMEM)],
        out_specs=[pl.BlockSpec((None, None, DIFF_T, DIFF_T), lambda h, t: (h, t, 0, 0)),
                   pl.BlockSpec((None, None, 1, DIFF_T), lambda h, t: (h, t, 0, 0))],
        out_shape=[jax.ShapeDtypeStruct((N_HEADS_B, DIFF_NT, DIFF_T, DIFF_T), F32),
                   jax.ShapeDtypeStruct((N_HEADS_B, DIFF_NT, 1, DIFF_T), F32)],
        compiler_params=_params("arbitrary", "arbitrary"),
        name="diff_bias",
    )(rel_bias)


def _dil_bias_kernel(tbl_ref, o_ref):
    p = pl.program_id(0)
    h = pl.program_id(1)
    shape = (DIL_Q, DIL_K)
    q = lax.broadcasted_iota(jnp.int32, shape, 0)
    k = lax.broadcasted_iota(jnp.int32, shape, 1)
    steps = k - HALF_WIN - q
    dil = lax.shift_left(jnp.int32(1), 2 * p)
    bias = _bias_from_rel(steps * dil, tbl_ref, h)
    o_ref[...] = jnp.where(jnp.abs(steps) <= HALF_WIN, bias, NEG)


def _dil_bias(rel_bias):
    n_pat = len(DILATED_PATTERNS)
    return pl.pallas_call(
        _dil_bias_kernel,
        grid=(n_pat, N_HEADS_A),
        in_specs=[pl.BlockSpec(memory_space=pltpu.SMEM)],
        out_specs=pl.BlockSpec((None, None, DIL_Q, DIL_K), lambda p, h: (p, h, 0, 0)),
        out_shape=jax.ShapeDtypeStruct((n_pat, N_HEADS_A, DIL_Q, DIL_K), F32),
        compiler_params=_params("arbitrary", "arbitrary"),
        name="dil_bias",
    )(rel_bias)


def _dil_kernel(q_ref, kp_ref, kc_ref, kn_ref, vp_ref, vc_ref, vn_ref, bias_ref, o_ref, lse_ref, kbuf, vbuf,
                *, seq_len):
    tl = q_ref.shape[0]
    i = pl.program_id(2)
    kbuf[0:HALF_WIN, :] = kp_ref[...]
    kbuf[HALF_WIN:HALF_WIN + tl, :] = kc_ref[...]
    kbuf[HALF_WIN + tl:, :] = kn_ref[...]
    vbuf[0:HALF_WIN, :] = vp_ref[...]
    vbuf[HALF_WIN:HALF_WIN + tl, :] = vc_ref[...]
    vbuf[HALF_WIN + tl:, :] = vn_ref[...]
    lane = lax.broadcasted_iota(jnp.int32, (1, LANES), 1)
    even = lane < HEAD_DIM
    col = lax.broadcasted_iota(jnp.int32, (1, DIL_K), 1)
    contract_last = (((1,), (1,)), ((), ()))

    def sub_block(sb, carry):
        r0 = pl.multiple_of(sb * DIL_Q, DIL_Q)
        kpos = i * tl + r0 - HALF_WIN + col
        colpen = jnp.where((kpos >= 0) & (kpos < seq_len), 0.0, NEG)
        for hp in range(N_HEADS_A // 2):
            ls = slice(hp * LANES, (hp + 1) * LANES)
            q = q_ref[pl.ds(r0, DIL_Q), ls]
            kw = kbuf[pl.ds(r0, DIL_K), ls]
            vw = vbuf[pl.ds(r0, DIL_K), ls]
            outs, lses = [], []
            for par in range(2):
                keep = even if par == 0 else jnp.logical_not(even)
                kz = jnp.where(keep, kw, jnp.zeros_like(kw))
                s = lax.dot_general(q, kz, contract_last, preferred_element_type=F32)
                s = s + bias_ref[2 * hp + par] + colpen
                m = jnp.max(s, axis=-1, keepdims=True)
                p = jnp.exp(s - m)
                l = jnp.sum(p, axis=-1, keepdims=True)
                o = jnp.dot(p.astype(BF16), vw, preferred_element_type=F32)
                outs.append(o / l)
                lses.append(m + jnp.log(l))
            o_ref[pl.ds(r0, DIL_Q), ls] = jnp.where(even, outs[0], outs[1]).astype(o_ref.dtype)
            lse_ref[pl.ds(r0, DIL_Q), ls] = jnp.where(even, lses[0], lses[1])
        return carry

    lax.fori_loop(0, tl // DIL_Q, sub_block, 0)


def _dilated(q, k, v, bias, dil):
    B, L, _ = q.shape
    W = WIDTH_A
    tl = min(DIL_TL, L)
    nhalo = L // HALF_WIN
    cur = pl.BlockSpec((None, tl, W), lambda b, r, i: (b, i, r))
    prev = pl.BlockSpec((None, HALF_WIN, W),
                        lambda b, r, i: (b, jnp.maximum(i * (tl // HALF_WIN) - 1, 0), r))
    nxt = pl.BlockSpec((None, HALF_WIN, W),
                       lambda b, r, i: (b, jnp.minimum((i + 1) * (tl // HALF_WIN), nhalo - 1), r))
    bias_spec = pl.BlockSpec(bias.shape, lambda b, r, i: (0, 0, 0))
    return pl.pallas_call(
        functools.partial(_dil_kernel, seq_len=L),
        grid=(B, dil, L // tl),
        in_specs=[cur, prev, cur, nxt, prev, cur, nxt, bias_spec],
        out_specs=[cur, cur],
        out_shape=[jax.ShapeDtypeStruct((B, L, dil * W), BF16), jax.ShapeDtypeStruct((B, L, dil * W), F32)],
        scratch_shapes=[pltpu.VMEM((tl + 2 * HALF_WIN, W), BF16), pltpu.VMEM((tl + 2 * HALF_WIN, W), BF16)],
        compiler_params=_params("arbitrary", "arbitrary", "arbitrary"),
        name=f"dilated_{dil}",
    )(q, k, k, k, v, v, v, bias)


def _diff_kernel(tbl_ref, q_ref, k1_ref, k2_ref, vt_ref, band_ref, bmax_ref, lq1_ref, lk1_ref, lq2_ref, lk2_ref,
                 g_ref, o_ref, m1, l1, a1, m2, l2, a2, sa, sb, mxa, mxb, *, lambda_init):
    h = pl.program_id(1)
    i = pl.program_id(2)
    T = q_ref.shape[0]
    nk = k1_ref.shape[0] // T
    streams = ((k1_ref, m1, l1, a1), (k2_ref, m2, l2, a2))
    for _, m, l, a in streams:
        m[...] = jnp.full(m.shape, NEG, F32)
        l[...] = jnp.zeros(l.shape, F32)
        a[...] = jnp.zeros(a.shape, F32)
    q = q_ref[...]
    contract_last = (((1,), (1,)), ((), ()))

    def scores(j, sbuf, mxbuf, st):
        r0 = pl.multiple_of(j * T, T)
        s = lax.dot_general(streams[st][0][pl.ds(r0, T), :], q, contract_last, preferred_element_type=F32)
        sbuf[st] = s
        mxbuf[st] = jnp.max(s, axis=0, keepdims=True)

    def softmax_pv(j, sbuf, mxbuf, st, near, shift):
        r0 = pl.multiple_of(j * T, T)
        _, m, l, a = streams[st]
        s = sbuf[st]
        if near:
            t = jnp.clip(j - i, -(DIFF_NB + 1), DIFF_NB + 1) + (DIFF_NB + 1)
            s = s + band_ref[t]
            bound = mxbuf[st] + bmax_ref[t]
        else:
            bound = mxbuf[st] + shift
        m_prev = m[...]
        m_new = jnp.maximum(m_prev, bound)
        alpha = jnp.exp2(m_prev - m_new)
        p = jnp.exp2(s - (m_new - shift))
        l[...] = alpha * l[...] + jnp.sum(p, axis=0, keepdims=True)
        a[...] = alpha * a[...] + jnp.dot(vt_ref[:, pl.ds(r0, T)], p.astype(BF16), preferred_element_type=F32)
        m[...] = m_new

    def group(near, shift):
        def body(g, carry):
            j0 = DIFF_GROUP * g
            bufs = ((sa, mxa), (sb, mxb))
            for u in range(DIFF_GROUP):
                cur, nxt = bufs[u % 2], bufs[(u + 1) % 2]
                for st in range(2):
                    scores(jnp.minimum(j0 + u + 1, nk - 1), *nxt, st)
                    softmax_pv(j0 + u, *cur, st, near, shift)
            return carry
        return body

    scores(0, sa, mxa, 0)
    scores(0, sa, mxa, 1)

    lo = jnp.maximum((i - DIFF_NB) // DIFF_GROUP, 0)
    hi = jnp.minimum((i + DIFF_NB) // DIFF_GROUP + 1, nk // DIFF_GROUP)
    c_neg = tbl_ref[NUM_BUCKETS // 2 - 1, N_HEADS_A + h] * LOG2E
    c_pos = tbl_ref[NUM_BUCKETS - 1, N_HEADS_A + h] * LOG2E
    lax.fori_loop(0, lo, group(False, c_neg), 0)
    lax.fori_loop(lo, hi, group(True, 0.0), 0)
    lax.fori_loop(hi, nk // DIFF_GROUP, group(False, c_pos), 0)

    lam =(jnp.exp(jnp.sum(lq1_ref[...] * lk1_ref[...], axis=-1, keepdims=True))
           - jnp.exp(jnp.sum(lq2_ref[...] * lk2_ref[...], axis=-1, keepdims=True)) + lambda_init)
    o = a1[...] / l1[...] - lam * (a2[...] / l2[...])
    ms = jnp.mean(o * o, axis=0, keepdims=True)
    o = o * lax.rsqrt(ms + EPS) * (g_ref[...] * (1.0 - lambda_init))
    o_ref[...] = o.T.astype(o_ref.dtype)


def _diff_attention(rel_bias, qb, k1, k2, vbt, band, bmax, lq1, lk1, lq2, lk2, g_col, lambda_init):
    B, S, W = qb.shape
    T = DIFF_T
    qspec = pl.BlockSpec((None, T, LANES), lambda b, h, i: (b, i, h))
    kspec = pl.BlockSpec((None, S, LANES), lambda b, h, i: (b, 0, h))
    vspec = pl.BlockSpec((None, LANES, S), lambda b, h, i: (b, h, 0))
    band_spec = pl.BlockSpec((None, DIFF_NT, T, T), lambda b, h, i: (h, 0, 0, 0))
    bmax_spec = pl.BlockSpec((None, DIFF_NT, 1, T), lambda b, h, i: (h, 0, 0, 0))
    vec = lambda n: pl.BlockSpec((1, n), lambda b, h, i: (0, 0))
    stat = pltpu.VMEM((1, T), F32)
    acc = pltpu.VMEM((LANES, T), F32)
    return pl.pallas_call(
        functools.partial(_diff_kernel, lambda_init=lambda_init),
        grid=(B, N_HEADS_B, S // T),
        in_specs=[pl.BlockSpec(memory_space=pltpu.SMEM), qspec, kspec, kspec, vspec, band_spec, bmax_spec,
                  vec(HEAD_DIM), vec(HEAD_DIM), vec(HEAD_DIM), vec(HEAD_DIM),
                  pl.BlockSpec((LANES, 1), lambda b, h, i: (0, 0))],
        out_specs=qspec,
        out_shape=jax.ShapeDtypeStruct((B, S, W), BF16),
        scratch_shapes=[stat, stat, acc, stat, stat, acc, pltpu.VMEM((2, T, T), F32), pltpu.VMEM((2, T, T), F32),
                        pltpu.VMEM((2, 1, T), F32), pltpu.VMEM((2, 1, T), F32)],
        compiler_params=_params("arbitrary", "arbitrary", "arbitrary"),
        name="diff_attn",
    )(rel_bias, qb, k1, k2, vbt, band, bmax, lq1, lk1, lq2, lk2, g_col)


def _out_proj_kernel(o1_ref, o4_ref, o16_ref, s1_ref, s4_ref, s16_ref, ob_ref, x_ref, gate_ref, shift_ref,
                     scale_ref, g_ref, wa_ref, wb_ref, h_ref, hn_ref, nat, oa_buf):
    tm = x_ref.shape[0]
    for n, (dil, ref) in enumerate(((4, o4_ref), (4, s4_ref), (16, o16_ref), (16, s16_ref))):
        rows = tm // dil
        for r in range(dil):
            for s in range(WIDTH_A // LANES):
                lo = r * WIDTH_A + s * LANES
                nat[n, s, pl.ds(r, rows, stride=dil), :] = ref[:, lo:lo + LANES].astype(F32)
    for s in range(WIDTH_A // LANES):
        sl = slice(s * LANES, (s + 1) * LANES)
        s1, s2, s3 = s1_ref[:, sl], nat[1, s], nat[3, s]
        mx = jnp.maximum(jnp.maximum(s1, s2), s3)
        e1, e2, e3 = jnp.exp(s1 - mx), jnp.exp(s2 - mx), jnp.exp(s3 - mx)
        oa = (e1 * o1_ref[:, sl].astype(F32) + e2 * nat[0, s] + e3 * nat[2, s]) / (e1 + e2 + e3)
        oa_buf[:, sl] = oa.astype(BF16)
    mixed = (jnp.dot(oa_buf[...], wa_ref[...], preferred_element_type=F32)
             + jnp.dot(ob_ref[...], wb_ref[...], preferred_element_type=F32))
    h = x_ref[...] + gate_ref[...] * mixed
    h_ref[...] = h
    ms = jnp.mean(h * h, axis=-1, keepdims=True)
    hn = h * lax.rsqrt(ms + EPS) * g_ref[...]
    hn_ref[...] = (hn * (1.0 + scale_ref[...]) + shift_ref[...]).astype(hn_ref.dtype)


def _out_proj(oas, lses, ob, x, gate, shift, scale, g, wa, wb):
    B, S, D = x.shape
    tm = OUT_TM
    half = pl.BlockSpec((None, tm, 512), lambda b, i: (b, i, 0))
    row = pl.BlockSpec((None, tm, D), lambda b, i: (b, i, 0))
    mod = pl.BlockSpec((None, 1, D), lambda b, i: (b, 0, 0))
    const = lambda shape: pl.BlockSpec(shape, lambda b, i: (0,) * len(shape))
    res = lambda dil: pl.BlockSpec((None, tm // dil, dil * WIDTH_A), lambda b, i: (b, i, 0))
    pats = [half, res(4), res(16)]
    return pl.pallas_call(
        _out_proj_kernel,
        grid=(B, S // tm),
        in_specs=pats + pats + [half, row, mod, mod, mod, const((1, D)), const(wa.shape), const(wb.shape)],
        out_specs=[row, row],
        out_shape=[jax.ShapeDtypeStruct((B, S, D), F32), jax.ShapeDtypeStruct((B, S, D), BF16)],
        scratch_shapes=[pltpu.VMEM((4, WIDTH_A // LANES, tm, LANES), F32), pltpu.VMEM((tm, WIDTH_A), BF16)],
        compiler_params=_params("arbitrary", "arbitrary"),
        name="out_proj",
    )(*oas, *lses, ob, x, gate, shift, scale, g, wa, wb)


def _ffn_kernel(hp_ref, hc_ref, hn_ref, h_ref, gate_ref, wu_ref, cw_ref, cb_ref, wd_ref, o_ref, lhs, act):
    i = pl.program_id(1)
    tm = hc_ref.shape[0]
    n = tm + 2 * FFN_HALO
    lhs[0:FFN_HALO, :] = jnp.where(i > 0, hp_ref[...], jnp.zeros_like(hp_ref))
    lhs[FFN_HALO:FFN_HALO + tm, :] = hc_ref[...]
    lhs[FFN_HALO + tm:, :] = jnp.where(i < pl.num_programs(1) - 1, hn_ref[...], jnp.zeros_like(hn_ref))
    x = lhs[...]

    def conv(u, lo):
        cw = cw_ref[:, lo:lo + FFN_FC]
        mid = slice(FFN_HALO, FFN_HALO + tm)
        below = pltpu.roll(u, 1, 0)[mid]
        above = pltpu.roll(u, n - 1, 0)[mid]
        return cw[0:1] * below + cw[1:2] * u[mid] + cw[2:3] * above + cb_ref[:, lo:lo + FFN_FC]

    for c in range(D_FF // FFN_FC):
        lo_v = c * FFN_FC
        lo_g = D_FF + c * FFN_FC
        val = conv(jnp.dot(x, wu_ref[:, lo_v:lo_v + FFN_FC], preferred_element_type=F32), lo_v)
        gt = conv(jnp.dot(x, wu_ref[:, lo_g:lo_g + FFN_FC], preferred_element_type=F32), lo_g)
        act[:, lo_v:lo_v + FFN_FC] = (gt / (1.0 + jnp.exp(-gt)) * val).astype(BF16)
    down = jnp.dot(act[...], wd_ref[...], preferred_element_type=F32)
    o_ref[...] = h_ref[...] + gate_ref[...] * down


def _ffn(hn, h, gate, wu, cw, cb, wd):
    B, S, D = h.shape
    tm = FFN_TM
    per = tm // FFN_HALO
    nh = S // FFN_HALO
    row = lambda dt: pl.BlockSpec((None, tm, D), lambda b, i: (b, i, 0))
    prev = pl.BlockSpec((None, FFN_HALO, D), lambda b, i: (b, jnp.maximum(i * per - 1, 0), 0))
    nxt = pl.BlockSpec((None, FFN_HALO, D), lambda b, i: (b, jnp.minimum((i + 1) * per, nh - 1), 0))
    mod = pl.BlockSpec((None, 1, D), lambda b, i: (b, 0, 0))
    const = lambda shape: pl.BlockSpec(shape, lambda b, i: (0,) * len(shape), pipeline_mode=pl.Buffered(1))
    return pl.pallas_call(
        _ffn_kernel,
        grid=(B, S // tm),
        in_specs=[prev, row(BF16), nxt, row(F32), mod, const(wu.shape), const(cw.shape), const(cb.shape),
                  const(wd.shape)],
        out_specs=row(F32),
        out_shape=jax.ShapeDtypeStruct((B, S, D), F32),
        scratch_shapes=[pltpu.VMEM((tm + 2 * FFN_HALO, D), BF16), pltpu.VMEM((tm, D_FF), BF16)],
        compiler_params=_params("arbitrary", "arbitrary"),
        name="conv_ffn",
    )(hn, hn, hn, h, gate, wu, cw, cb, wd)


def _block_diag_ones():
    r = jnp.arange(MXU_DIM) // HEAD_DIM
    return (r[:, None] == r[None, :]).astype(BF16)


def kernel(x, c, w_ada, b_ada, norm1_g, w_in, q_norm_a, k_norm_a, q_norm_b, k_norm_b, rel_bias, lambda_q1,
           lambda_k1, lambda_q2, lambda_k2, subln_g, w_out, norm2_g, w_up, conv_w, conv_b, w_down):
    B, S, D = x.shape
    depth = w_ada.shape[0]
    h = x.astype(F32)
    c8 = jnp.pad(c.astype(F32), ((0, 8 - B), (0, 0)))
    bd = _block_diag_ones()
    rel_bias = rel_bias.astype(F32)
    band_b, bmax_b = _diff_bias(rel_bias)
    band_a = _dil_bias(rel_bias)
    qscale = HEAD_DIM ** -0.5
    tile8 = lambda g: jnp.tile(g.astype(F32), WIDTH_A // HEAD_DIM)[None, :]
    row = lambda v: v.astype(F32)[None, :]

    for layer in range(depth):
        lambda_init = 0.8 - 0.6 * math.exp(-0.3 * layer)
        mod = _ada(c8, w_ada[layer].astype(F32), row(b_ada[layer]))[:B]
        shift1, scale1, gate1, shift2, scale2, gate2 = [m[:, None, :] for m in jnp.split(mod, 6, axis=-1)]

        qa, ka, va, qa4, ka4, va4, qa16, ka16, va16, qb, k1, k2, vbt = _in_proj(
            h, shift1, scale1, row(norm1_g[layer]), w_in[layer].astype(BF16), bd,
            tile8(q_norm_a[layer]) * qscale, tile8(k_norm_a[layer]),
            tile8(q_norm_b[layer]) * (qscale * LOG2E), tile8(k_norm_b[layer]))

        oas, lses = [], []
        qkv = {1: (qa, ka, va), 4: (qa4, ka4, va4), 16: (qa16, ka16, va16)}
        for p, (_, dil) in enumerate(DILATED_PATTERNS):
            o, lse = _dilated(*qkv[dil], band_a[p], dil)
            oas.append(o)
            lses.append(lse)

        ob = _diff_attention(rel_bias, qb, k1, k2, vbt, band_b, bmax_b, row(lambda_q1[layer]), row(lambda_k1[layer]),
                             row(lambda_q2[layer]), row(lambda_k2[layer]),
                             subln_g[layer].astype(F32)[:, None], lambda_init)

        w_o = w_out[layer].astype(BF16)
        h, hn = _out_proj(oas, lses, ob, h, gate1, shift2, scale2, row(norm2_g[layer]),
                          w_o[:WIDTH_A], w_o[WIDTH_A:])
        h = _ffn(hn, h, gate2, w_up[layer].astype(BF16), conv_w[layer].astype(F32), row(conv_b[layer]),
                 w_down[layer].astype(BF16))

    return h.astype(x.dtype)
```

```python
import functools
import math

import jax
import jax.numpy as jnp
from jax import lax
from jax.experimental import pallas as pl
from jax.experimental.pallas import tpu as pltpu

F32 = jnp.float32
BF16 = jnp.bfloat16

D_MODEL = 1024
HEAD_DIM = 64
WIDTH_A = 512
WIDTH_B = 512
N_HEADS_A = 8
N_HEADS_B = 4
DILATED_PATTERNS = ((128, 1), (512, 4), (2048, 16))
D_FF = 2816
NUM_BUCKETS = 32
MAX_DISTANCE = 2048
EPS = 1e-6
NEG = -1e30
LOG2E = math.log2(math.e)

LANES = 128
BF16_SUBLANES = 16
MXU_DIM = 256
VMEM_LIMIT = 56 * 1024 * 1024

HALF_WIN = 64
DIL_Q = 128
DIL_K = DIL_Q + 2 * HALF_WIN
DIL_TL = 512
DIL_AHEAD = 2

DIFF_T = 512
BUCKET_SAT = 1024
DIFF_NB = BUCKET_SAT // DIFF_T
DIFF_NT = 2 * DIFF_NB + 3
DIFF_GROUP = 4

PROJ_TM = 512
OUT_TM = 512
FFN_TM = 512
FFN_FC = 256
FFN_HALO = BF16_SUBLANES

_BUCKET_THRESHOLDS = (1, 2, 3, 4, 5, 6, 7, 8, 16, 32, 64, 128, 256, 512, 1024)


def _params(*sem):
    return pltpu.CompilerParams(dimension_semantics=sem, vmem_limit_bytes=VMEM_LIMIT)


def _ada_kernel(c_ref, w_ref, b_ref, o_ref):
    c = c_ref[...]
    ca = c / (1.0 + jnp.exp(-c))
    o_ref[...] = jnp.dot(ca, w_ref[...], preferred_element_type=F32) + b_ref[...]


def _ada(c8, w, b):
    n = w.shape[1]
    tn = 1024
    return pl.pallas_call(
        _ada_kernel,
        grid=(n // tn,),
        in_specs=[pl.BlockSpec((8, D_MODEL), lambda j: (0, 0)),
                  pl.BlockSpec((D_MODEL, tn), lambda j: (0, j)),
                  pl.BlockSpec((1, tn), lambda j: (0, j))],
        out_specs=pl.BlockSpec((8, tn), lambda j: (0, j)),
        out_shape=jax.ShapeDtypeStruct((8, n), F32),
        compiler_params=_params("arbitrary"),
        name="ada",
    )(c8, w, b)


def _head_norm(p, bd, gain):
    ss = jnp.dot((p * p).astype(BF16), bd, preferred_element_type=F32)
    return p * lax.rsqrt(ss * (1.0 / HEAD_DIM) + EPS) * gain


def _in_proj_kernel(x_ref, shift_ref, scale_ref, g_ref, w_ref, bd_ref, gqa_ref, gka_ref, gqb_ref, gkb_ref,
                    qa_ref, ka_ref, va_ref, qa4_ref, ka4_ref, va4_ref, qa16_ref, ka16_ref, va16_ref,
                    qb_ref, k1_ref, k2_ref, vbt_ref, pbuf, stage):
    tm = x_ref.shape[0]
    x = x_ref[...]
    ms = jnp.mean(x * x, axis=-1, keepdims=True)
    hn = x * lax.rsqrt(ms + EPS) * g_ref[...]
    hn = (hn * (1.0 + scale_ref[...]) + shift_ref[...]).astype(BF16)
    pbuf[...] = jnp.dot(hn, w_ref[...], preferred_element_type=F32)
    bd = bd_ref[...]
    lane = lax.broadcasted_iota(jnp.int32, (1, MXU_DIM), 1)
    first = (lane // HEAD_DIM) % 2 == 0

    def proj(g, c):
        lo = g * 512 + c * MXU_DIM
        return pbuf[:, lo:lo + MXU_DIM]

    for c in range(2):
        sl = slice(c * MXU_DIM, (c + 1) * MXU_DIM)
        group_a = (_head_norm(proj(0, c), bd, gqa_ref[:, sl]), _head_norm(proj(1, c), bd, gka_ref[:, sl]),
                   proj(2, c))
        for a, (val, ref) in enumerate(zip(group_a, (qa_ref, ka_ref, va_ref))):
            ref[:, sl] = val.astype(BF16)
            stage[a, 2 * c] = val[:, :LANES]
            stage[a, 2 * c + 1] = val[:, LANES:]
        qb_ref[:, sl] = _head_norm(proj(3, c), bd, gqb_ref[:, sl]).astype(BF16)
        kb = _head_norm(proj(4, c), bd, gkb_ref[:, sl])
        k1_ref[:, sl] = jnp.where(first, kb, 0.0).astype(BF16)
        k2_ref[:, sl] = jnp.where(first, 0.0, kb).astype(BF16)
        vbt_ref[sl, :] = proj(5, c).T.astype(BF16)

    by_residue = ((4, (qa4_ref, ka4_ref, va4_ref)), (16, (qa16_ref, ka16_ref, va16_ref)))
    for dil, refs in by_residue:
        rows = tm // dil
        for a, ref in enumerate(refs):
            for r in range(dil):
                for s in range(WIDTH_A // LANES):
                    lo = r * WIDTH_A + s * LANES
                    ref[:, lo:lo + LANES] = stage[a, s, pl.ds(r, rows, stride=dil), :].astype(BF16)


def _in_proj(x, shift, scale, g, w_bf, bd, gqa, gka, gqb, gkb):
    B, S, D = x.shape
    tm = PROJ_TM
    row = pl.BlockSpec((None, tm, D), lambda b, i: (b, i, 0))
    mod = pl.BlockSpec((None, 1, D), lambda b, i: (b, 0, 0))
    const = lambda shape: pl.BlockSpec(shape, lambda b, i: (0,) * len(shape))
    out = pl.BlockSpec((None, tm, 512), lambda b, i: (b, i, 0))
    out_t = pl.BlockSpec((None, 512, tm), lambda b, i: (b, 0, i))
    res = lambda dil: pl.BlockSpec((None, tm // dil, dil * WIDTH_A), lambda b, i: (b, i, 0))
    res_shape = lambda dil: jax.ShapeDtypeStruct((B, S // dil, dil * WIDTH_A), BF16)
    nat_shape = jax.ShapeDtypeStruct((B, S, 512), BF16)
    return pl.pallas_call(
        _in_proj_kernel,
        grid=(B, S // tm),
        in_specs=[row, mod, mod, const((1, D)), const(w_bf.shape), const(bd.shape),
                  const((1, 512)), const((1, 512)), const((1, 512)), const((1, 512))],
        out_specs=[out] * 3 + [res(4)] * 3 + [res(16)] * 3 + [out] * 3 + [out_t],
        out_shape=[nat_shape] * 3 + [res_shape(4)] * 3 + [res_shape(16)] * 3 + [nat_shape] * 3
                  + [jax.ShapeDtypeStruct((B, 512, S), BF16)],
        scratch_shapes=[pltpu.VMEM((tm, w_bf.shape[1]), F32),
                        pltpu.VMEM((3, WIDTH_A // LANES, tm, LANES), F32)],
        compiler_params=_params("arbitrary", "arbitrary"),
        name="in_proj",
    )(x, shift, scale, g, w_bf, bd, gqa, gka, gqb, gkb)


def _bias_from_rel(rel, tbl_ref, h):
    n = jnp.abs(rel)
    vneg = jnp.full(rel.shape, tbl_ref[0, h], F32)
    vpos = jnp.full(rel.shape, tbl_ref[NUM_BUCKETS // 2, h], F32)
    for i, th in enumerate(_BUCKET_THRESHOLDS):
        ge = n >= th
        vneg = jnp.where(ge, tbl_ref[i + 1, h], vneg)
        vpos = jnp.where(ge, tbl_ref[NUM_BUCKETS // 2 + i + 1, h], vpos)
    return jnp.where(rel > 0, vpos, vneg)


def _diff_bias_kernel(tbl_ref, o_ref, omax_ref):
    h = pl.program_id(0)
    t = pl.program_id(1)
    shape = (DIFF_T, DIFF_T)
    k = lax.broadcasted_iota(jnp.int32, shape, 0)
    q = lax.broadcasted_iota(jnp.int32, shape, 1)
    rel = k - q + (t - (DIFF_NB + 1)) * DIFF_T
    bias = _bias_from_rel(rel, tbl_ref, N_HEADS_A + h) * LOG2E
    o_ref[...] = bias
    omax_ref[...] = jnp.max(bias, axis=0, keepdims=True)


def _diff_bias(rel_bias):
    return pl.pallas_call(
        _diff_bias_kernel,
        grid=(N_HEADS_B, DIFF_NT),
        in_specs=[pl.BlockSpec(memory_space=pltpu.SMEM)],
        out_specs=[pl.BlockSpec((None, None, DIFF_T, DIFF_T), lambda h, t: (h, t, 0, 0)),
                   pl.BlockSpec((None, None, 1, DIFF_T), lambda h, t: (h, t, 0, 0))],
        out_shape=[jax.ShapeDtypeStruct((N_HEADS_B, DIFF_NT, DIFF_T, DIFF_T), F32),
                   jax.ShapeDtypeStruct((N_HEADS_B, DIFF_NT, 1, DIFF_T), F32)],
        compiler_params=_params("arbitrary", "arbitrary"),
        name="diff_bias",
    )(rel_bias)


def _dil_bias_kernel(tbl_ref, o_ref):
    p = pl.program_id(0)
    hp = pl.program_id(1)
    shape = (DIL_K, DIL_Q)
    k = lax.broadcasted_iota(jnp.int32, shape, 0)
    q = lax.broadcasted_iota(jnp.int32, shape, 1)
    steps = k - HALF_WIN - q
    dil = lax.shift_left(jnp.int32(1), 2 * p)
    inside = jnp.abs(steps) <= HALF_WIN
    for par in range(2):
        bias = jnp.where(inside, _bias_from_rel(steps * dil, tbl_ref, 2 * hp + par) * LOG2E, NEG)
        ls = slice(par * DIL_Q, (par + 1) * DIL_Q)
        o_ref[0, :, ls] = bias
        o_ref[1, :, ls] = jnp.where(k >= HALF_WIN, bias, NEG)
        o_ref[2, :, ls] = jnp.where(k < DIL_K - HALF_WIN, bias, NEG)


def _dil_bias(rel_bias):
    n_pat = len(DILATED_PATTERNS)
    n_pair = N_HEADS_A // 2
    return pl.pallas_call(
        _dil_bias_kernel,
        grid=(n_pat, n_pair),
        in_specs=[pl.BlockSpec(memory_space=pltpu.SMEM)],
        out_specs=pl.BlockSpec((None, 3, None, DIL_K, 2 * DIL_Q), lambda p, hp: (p, 0, hp, 0, 0)),
        out_shape=jax.ShapeDtypeStruct((n_pat, 3, n_pair, DIL_K, 2 * DIL_Q), F32),
        compiler_params=_params("arbitrary", "arbitrary"),
        name="dil_bias",
    )(rel_bias)


def _dil_kernel(q_ref, kp_ref, kc_ref, kn_ref, vp_ref, vc_ref, vn_ref, bias_ref, o_ref, lse_ref, kbuf, vtbuf):
    tl = q_ref.shape[0]
    i = pl.program_id(2)
    n_sub = tl // DIL_Q
    n_pair = N_HEADS_A // 2
    kbuf[0:HALF_WIN, :] = kp_ref[...]
    kbuf[HALF_WIN:HALF_WIN + tl, :] = kc_ref[...]
    kbuf[HALF_WIN + tl:, :] = kn_ref[...]
    for hp in range(n_pair):
        ls = slice(hp * LANES, (hp + 1) * LANES)
        v = jnp.concatenate([vp_ref[:, ls], vc_ref[:, ls], vn_ref[:, ls]], axis=0)
        vtbuf[hp] = v.astype(F32).T.astype(BF16)
    lane = lax.broadcasted_iota(jnp.int32, (1, LANES), 1)
    even = lane < HEAD_DIM
    contract_last = (((1,), (1,)), ((), ()))
    first_step = i == 0
    last_step = i == pl.num_programs(2) - 1

    def scores(sb, hp):
        r0 = sb * DIL_Q
        ls = slice(hp * LANES, (hp + 1) * LANES)
        if sb == 0:
            variant = jnp.where(first_step, 1, 0)
        elif sb == n_sub - 1:
            variant = jnp.where(last_step, 2, 0)
        else:
            variant = 0
        q = q_ref[r0:r0 + DIL_Q, ls]
        zero = jnp.zeros_like(q)
        qcat = jnp.concatenate([jnp.where(even, q, zero), jnp.where(even, zero, q)], axis=0)
        s = lax.dot_general(kbuf[r0:r0 + DIL_K, ls], qcat, contract_last, preferred_element_type=F32)
        return s + bias_ref[variant, hp]

    def finish(sb, hp, s):
        r0 = sb * DIL_Q
        ls = slice(hp * LANES, (hp + 1) * LANES)
        m = jnp.max(s, axis=0, keepdims=True)
        p = jnp.exp2(s - m)
        l = jnp.sum(p, axis=0, keepdims=True)
        ot = jnp.dot(vtbuf[hp, :, r0:r0 + DIL_K], p.astype(BF16), preferred_element_type=F32) / l
        lse = jnp.broadcast_to((m + jnp.log2(l)) * (1.0 / LOG2E), (HEAD_DIM, 2 * DIL_Q))
        ot = jnp.concatenate([ot[:HEAD_DIM, :DIL_Q], ot[HEAD_DIM:, DIL_Q:]], axis=0)
        lse = jnp.concatenate([lse[:, :DIL_Q], lse[:, DIL_Q:]], axis=0)
        o_ref[r0:r0 + DIL_Q, ls] = ot.T.astype(o_ref.dtype)
        lse_ref[r0:r0 + DIL_Q, ls] = lse.T

    bodies = [(sb, hp) for sb in range(n_sub) for hp in range(n_pair)]
    staged = [scores(*b) for b in bodies[:DIL_AHEAD]]
    for n, b in enumerate(bodies):
        if n + DIL_AHEAD < len(bodies):
            staged.append(scores(*bodies[n + DIL_AHEAD]))
        finish(*b, staged[n])


def _dilated(q, k, v, bias, dil):
    B, L, _ = q.shape
    W = WIDTH_A
    tl = min(DIL_TL, L)
    nhalo = L // HALF_WIN
    cur = pl.BlockSpec((None, tl, W), lambda b, r, i: (b, i, r))
    prev = pl.BlockSpec((None, HALF_WIN, W),
                        lambda b, r, i: (b, jnp.maximum(i * (tl // HALF_WIN) - 1, 0), r))
    nxt = pl.BlockSpec((None, HALF_WIN, W),
                       lambda b, r, i: (b, jnp.minimum((i + 1) * (tl // HALF_WIN), nhalo - 1), r))
    bias_spec = pl.BlockSpec(bias.shape, lambda b, r, i: (0,) * bias.ndim)
    return pl.pallas_call(
        _dil_kernel,
        grid=(B, dil, L // tl),
        in_specs=[cur, prev, cur, nxt, prev, cur, nxt, bias_spec],
        out_specs=[cur, cur],
        out_shape=[jax.ShapeDtypeStruct((B, L, dil * W), BF16), jax.ShapeDtypeStruct((B, L, dil * W), F32)],
        scratch_shapes=[pltpu.VMEM((tl + 2 * HALF_WIN, W), BF16),
                        pltpu.VMEM((N_HEADS_A // 2, LANES, tl + 2 * HALF_WIN), BF16)],
        compiler_params=_params("arbitrary", "arbitrary", "arbitrary"),
        name=f"dilated_{dil}",
    )(q, k, k, k, v, v, v, bias)


def _diff_kernel(tbl_ref, q_ref, k1_ref, k2_ref, vt_ref, band_ref, bmax_ref, lq1_ref, lk1_ref, lq2_ref, lk2_ref,
                 g_ref, o_ref, m1, l1, a1, m2, l2, a2, sa, sb, mxa, mxb, *, lambda_init):
    h = pl.program_id(1)
    i = pl.program_id(2)
    T = q_ref.shape[0]
    nk = k1_ref.shape[0] // T
    streams = ((k1_ref, m1, l1, a1), (k2_ref, m2, l2, a2))
    for _, m, l, a in streams:
        m[...] = jnp.full(m.shape, NEG, F32)
        l[...] = jnp.zeros(l.shape, F32)
        a[...] = jnp.zeros(a.shape, F32)
    q = q_ref[...]
    contract_last = (((1,), (1,)), ((), ()))

    def scores(j, sbuf, mxbuf, st):
        r0 = pl.multiple_of(j * T, T)
        s = lax.dot_general(streams[st][0][pl.ds(r0, T), :], q, contract_last, preferred_element_type=F32)
        sbuf[st] = s
        mxbuf[st] = jnp.max(s, axis=0, keepdims=True)

    def softmax_pv(j, sbuf, mxbuf, st, near, shift):
        r0 = pl.multiple_of(j * T, T)
        _, m, l, a = streams[st]
        s = sbuf[st]
        if near:
            t = jnp.clip(j - i, -(DIFF_NB + 1), DIFF_NB + 1) + (DIFF_NB + 1)
            s = s + band_ref[t]
            bound = mxbuf[st] + bmax_ref[t]
        else:
            bound = mxbuf[st] + shift
        m_prev = m[...]
        m_new = jnp.maximum(m_prev, bound)
        alpha = jnp.exp2(m_prev - m_new)
        p = jnp.exp2(s - (m_new - shift))
        l[...] = alpha * l[...] + jnp.sum(p, axis=0, keepdims=True)
        a[...] = alpha * a[...] + jnp.dot(vt_ref[:, pl.ds(r0, T)], p.astype(BF16), preferred_element_type=F32)
        m[...] = m_new

    def group(near, shift):
        def body(g, carry):
            j0 = DIFF_GROUP * g
            bufs = ((sa, mxa), (sb, mxb))
            for u in range(DIFF_GROUP):
                cur, nxt = bufs[u % 2], bufs[(u + 1) % 2]
                for st in range(2):
                    scores(jnp.minimum(j0 + u + 1, nk - 1), *nxt, st)
                    softmax_pv(j0 + u, *cur, st, near, shift)
            return carry
        return body

    scores(0, sa, mxa, 0)
    scores(0, sa, mxa, 1)

    lo = jnp.maximum((i - DIFF_NB) // DIFF_GROUP, 0)
    hi = jnp.minimum((i + DIFF_NB) // DIFF_GROUP + 1, nk // DIFF_GROUP)
    c_neg = tbl_ref[NUM_BUCKETS // 2 - 1, N_HEADS_A + h] * LOG2E
    c_pos = tbl_ref[NUM_BUCKETS - 1, N_HEADS_A + h] * LOG2E
    lax.fori_loop(0, lo, group(False, c_neg), 0)
    lax.fori_loop(lo, hi, group(True, 0.0), 0)
    lax.fori_loop(hi, nk // DIFF_GROUP, group(False, c_pos), 0)

    lam =(jnp.exp(jnp.sum(lq1_ref[...] * lk1_ref[...], axis=-1, keepdims=True))
           - jnp.exp(jnp.sum(lq2_ref[...] * lk2_ref[...], axis=-1, keepdims=True)) + lambda_init)
    o = a1[...] / l1[...] - lam * (a2[...] / l2[...])
    ms = jnp.mean(o * o, axis=0, keepdims=True)
    o = o * lax.rsqrt(ms + EPS) * (g_ref[...] * (1.0 - lambda_init))
    o_ref[...] = o.T.astype(o_ref.dtype)


def _diff_attention(rel_bias, qb, k1, k2, vbt, band, bmax, lq1, lk1, lq2, lk2, g_col, lambda_init):
    B, S, W = qb.shape
    T = DIFF_T
    qspec = pl.BlockSpec((None, T, LANES), lambda b, h, i: (b, i, h))
    kspec = pl.BlockSpec((None, S, LANES), lambda b, h, i: (b, 0, h))
    vspec = pl.BlockSpec((None, LANES, S), lambda b, h, i: (b, h, 0))
    band_spec = pl.BlockSpec((None, DIFF_NT, T, T), lambda b, h, i: (h, 0, 0, 0))
    bmax_spec = pl.BlockSpec((None, DIFF_NT, 1, T), lambda b, h, i: (h, 0, 0, 0))
    vec = lambda n: pl.BlockSpec((1, n), lambda b, h, i: (0, 0))
    stat = pltpu.VMEM((1, T), F32)
    acc = pltpu.VMEM((LANES, T), F32)
    return pl.pallas_call(
        functools.partial(_diff_kernel, lambda_init=lambda_init),
        grid=(B, N_HEADS_B, S // T),
        in_specs=[pl.BlockSpec(memory_space=pltpu.SMEM), qspec, kspec, kspec, vspec, band_spec, bmax_spec,
                  vec(HEAD_DIM), vec(HEAD_DIM), vec(HEAD_DIM), vec(HEAD_DIM),
                  pl.BlockSpec((LANES, 1), lambda b, h, i: (0, 0))],
        out_specs=qspec,
        out_shape=jax.ShapeDtypeStruct((B, S, W), BF16),
        scratch_shapes=[stat, stat, acc, stat, stat, acc, pltpu.VMEM((2, T, T), F32), pltpu.VMEM((2, T, T), F32),
                        pltpu.VMEM((2, 1, T), F32), pltpu.VMEM((2, 1, T), F32)],
        compiler_params=_params("arbitrary", "arbitrary", "arbitrary"),
        name="diff_attn",
    )(rel_bias, qb, k1, k2, vbt, band, bmax, lq1, lk1, lq2, lk2, g_col)


def _out_proj_kernel(o1_ref, o4_ref, o16_ref, s1_ref, s4_ref, s16_ref, ob_ref, x_ref, gate_ref, shift_ref,
                     scale_ref, g_ref, wa_ref, wb_ref, h_ref, hn_ref, nat, oa_buf):
    tm = x_ref.shape[0]
    for n, (dil, ref) in enumerate(((4, o4_ref), (4, s4_ref), (16, o16_ref), (16, s16_ref))):
        rows = tm // dil
        for r in range(dil):
            for s in range(WIDTH_A // LANES):
                lo = r * WIDTH_A + s * LANES
                nat[n, s, pl.ds(r, rows, stride=dil), :] = ref[:, lo:lo + LANES].astype(F32)
    for s in range(WIDTH_A // LANES):
        sl = slice(s * LANES, (s + 1) * LANES)
        s1, s2, s3 = s1_ref[:, sl], nat[1, s], nat[3, s]
        mx = jnp.maximum(jnp.maximum(s1, s2), s3)
        e1, e2, e3 = jnp.exp(s1 - mx), jnp.exp(s2 - mx), jnp.exp(s3 - mx)
        oa = (e1 * o1_ref[:, sl].astype(F32) + e2 * nat[0, s] + e3 * nat[2, s]) / (e1 + e2 + e3)
        oa_buf[:, sl] = oa.astype(BF16)
    mixed = (jnp.dot(oa_buf[...], wa_ref[...], preferred_element_type=F32)
             + jnp.dot(ob_ref[...], wb_ref[...], preferred_element_type=F32))
    h = x_ref[...] + gate_ref[...] * mixed
    h_ref[...] = h
    ms = jnp.mean(h * h, axis=-1, keepdims=True)
    hn = h * lax.rsqrt(ms + EPS) * g_ref[...]
    hn_ref[...] = (hn * (1.0 + scale_ref[...]) + shift_ref[...]).astype(hn_ref.dtype)


def _out_proj(oas, lses, ob, x, gate, shift, scale, g, wa, wb):
    B, S, D = x.shape
    tm = OUT_TM
    half = pl.BlockSpec((None, tm, 512), lambda b, i: (b, i, 0))
    row = pl.BlockSpec((None, tm, D), lambda b, i: (b, i, 0))
    mod = pl.BlockSpec((None, 1, D), lambda b, i: (b, 0, 0))
    const = lambda shape: pl.BlockSpec(shape, lambda b, i: (0,) * len(shape))
    res = lambda dil: pl.BlockSpec((None, tm // dil, dil * WIDTH_A), lambda b, i: (b, i, 0))
    pats = [half, res(4), res(16)]
    return pl.pallas_call(
        _out_proj_kernel,
        grid=(B, S // tm),
        in_specs=pats + pats + [half, row, mod, mod, mod, const((1, D)), const(wa.shape), const(wb.shape)],
        out_specs=[row, row],
        out_shape=[jax.ShapeDtypeStruct((B, S, D), F32), jax.ShapeDtypeStruct((B, S, D), BF16)],
        scratch_shapes=[pltpu.VMEM((4, WIDTH_A // LANES, tm, LANES), F32), pltpu.VMEM((tm, WIDTH_A), BF16)],
        compiler_params=_params("arbitrary", "arbitrary"),
        name="out_proj",
    )(*oas, *lses, ob, x, gate, shift, scale, g, wa, wb)


def _ffn_kernel(hp_ref, hc_ref, hn_ref, h_ref, gate_ref, wu_ref, cw_ref, cb_ref, wd_ref, o_ref, lhs, act):
    i = pl.program_id(1)
    tm = hc_ref.shape[0]
    n = tm + 2 * FFN_HALO
    lhs[0:FFN_HALO, :] = jnp.where(i > 0, hp_ref[...], jnp.zeros_like(hp_ref))
    lhs[FFN_HALO:FFN_HALO + tm, :] = hc_ref[...]
    lhs[FFN_HALO + tm:, :] = jnp.where(i < pl.num_programs(1) - 1, hn_ref[...], jnp.zeros_like(hn_ref))
    x = lhs[...]

    def conv(u, lo):
        cw = cw_ref[:, lo:lo + FFN_FC]
        mid = slice(FFN_HALO, FFN_HALO + tm)
        below = pltpu.roll(u, 1, 0)[mid]
        above = pltpu.roll(u, n - 1, 0)[mid]
        return cw[0:1] * below + cw[1:2] * u[mid] + cw[2:3] * above + cb_ref[:, lo:lo + FFN_FC]

    for c in range(D_FF // FFN_FC):
        lo_v = c * FFN_FC
        lo_g = D_FF + c * FFN_FC
        val = conv(jnp.dot(x, wu_ref[:, lo_v:lo_v + FFN_FC], preferred_element_type=F32), lo_v)
        gt = conv(jnp.dot(x, wu_ref[:, lo_g:lo_g + FFN_FC], preferred_element_type=F32), lo_g)
        act[:, lo_v:lo_v + FFN_FC] = (gt / (1.0 + jnp.exp(-gt)) * val).astype(BF16)
    down = jnp.dot(act[...], wd_ref[...], preferred_element_type=F32)
    o_ref[...] = h_ref[...] + gate_ref[...] * down


def _ffn(hn, h, gate, wu, cw, cb, wd):
    B, S, D = h.shape
    tm = FFN_TM
    per = tm // FFN_HALO
    nh = S // FFN_HALO
    row = lambda dt: pl.BlockSpec((None, tm, D), lambda b, i: (b, i, 0))
    prev = pl.BlockSpec((None, FFN_HALO, D), lambda b, i: (b, jnp.maximum(i * per - 1, 0), 0))
    nxt = pl.BlockSpec((None, FFN_HALO, D), lambda b, i: (b, jnp.minimum((i + 1) * per, nh - 1), 0))
    mod = pl.BlockSpec((None, 1, D), lambda b, i: (b, 0, 0))
    const = lambda shape: pl.BlockSpec(shape, lambda b, i: (0,) * len(shape), pipeline_mode=pl.Buffered(1))
    return pl.pallas_call(
        _ffn_kernel,
        grid=(B, S // tm),
        in_specs=[prev, row(BF16), nxt, row(F32), mod, const(wu.shape), const(cw.shape), const(cb.shape),
                  const(wd.shape)],
        out_specs=row(F32),
        out_shape=jax.ShapeDtypeStruct((B, S, D), F32),
        scratch_shapes=[pltpu.VMEM((tm + 2 * FFN_HALO, D), BF16), pltpu.VMEM((tm, D_FF), BF16)],
        compiler_params=_params("arbitrary", "arbitrary"),
        name="conv_ffn",
    )(hn, hn, hn, h, gate, wu, cw, cb, wd)


def _block_diag_ones():
    r = jnp.arange(MXU_DIM) // HEAD_DIM
    return (r[:, None] == r[None, :]).astype(BF16)


def kernel(x, c, w_ada, b_ada, norm1_g, w_in, q_norm_a, k_norm_a, q_norm_b, k_norm_b, rel_bias, lambda_q1,
           lambda_k1, lambda_q2, lambda_k2, subln_g, w_out, norm2_g, w_up, conv_w, conv_b, w_down):
    B, S, D = x.shape
    depth = w_ada.shape[0]
    h = x.astype(F32)
    c8 = jnp.pad(c.astype(F32), ((0, 8 - B), (0, 0)))
    bd = _block_diag_ones()
    rel_bias = rel_bias.astype(F32)
    band_b, bmax_b = _diff_bias(rel_bias)
    band_a = _dil_bias(rel_bias)
    qscale = HEAD_DIM ** -0.5
    tile8 = lambda g: jnp.tile(g.astype(F32), WIDTH_A // HEAD_DIM)[None, :]
    row = lambda v: v.astype(F32)[None, :]

    for layer in range(depth):
        lambda_init = 0.8 - 0.6 * math.exp(-0.3 * layer)
        mod = _ada(c8, w_ada[layer].astype(F32), row(b_ada[layer]))[:B]
        shift1, scale1, gate1, shift2, scale2, gate2 = [m[:, None, :] for m in jnp.split(mod, 6, axis=-1)]

        qa, ka, va, qa4, ka4, va4, qa16, ka16, va16, qb, k1, k2, vbt = _in_proj(
            h, shift1, scale1, row(norm1_g[layer]), w_in[layer].astype(BF16), bd,
            tile8(q_norm_a[layer]) * (qscale * LOG2E), tile8(k_norm_a[layer]),
            tile8(q_norm_b[layer]) * (qscale * LOG2E), tile8(k_norm_b[layer]))

        oas, lses = [], []
        qkv = {1: (qa, ka, va), 4: (qa4, ka4, va4), 16: (qa16, ka16, va16)}
        for p, (_, dil) in enumerate(DILATED_PATTERNS):
            o, lse = _dilated(*qkv[dil], band_a[p], dil)
            oas.append(o)
            lses.append(lse)

        ob = _diff_attention(rel_bias, qb, k1, k2, vbt, band_b, bmax_b, row(lambda_q1[layer]), row(lambda_k1[layer]),
                             row(lambda_q2[layer]), row(lambda_k2[layer]),
                             subln_g[layer].astype(F32)[:, None], lambda_init)

        w_o = w_out[layer].astype(BF16)
        h, hn = _out_proj(oas, lses, ob, h, gate1, shift2, scale2, row(norm2_g[layer]),
                          w_o[:WIDTH_A], w_o[WIDTH_A:])
        h = _ffn(hn, h, gate2, w_up[layer].astype(BF16), conv_w[layer].astype(F32), row(conv_b[layer]),
                 w_down[layer].astype(BF16))

    return h.astype(x.dtype)
```

```python
import functools
import math

import jax
import jax.numpy as jnp
from jax import lax
from jax.experimental import pallas as pl
from jax.experimental.pallas import tpu as pltpu

F32 = jnp.float32
BF16 = jnp.bfloat16

D_MODEL = 1024
HEAD_DIM = 64
WIDTH_A = 512
WIDTH_B = 512
N_HEADS_A = 8
N_HEADS_B = 4
DILATED_PATTERNS = ((128, 1), (512, 4), (2048, 16))
D_FF = 2816
NUM_BUCKETS = 32
MAX_DISTANCE = 2048
EPS = 1e-6
NEG = -1e30
LOG2E = math.log2(math.e)

LANES = 128
BF16_SUBLANES = 16
MXU_DIM = 256
VMEM_LIMIT = 56 * 1024 * 1024

HALF_WIN = 64
DIL_Q = 128
DIL_K = DIL_Q + 2 * HALF_WIN
DIL_TL = 512
DIL_AHEAD = 4

DIFF_T = 512
BUCKET_SAT = 1024
DIFF_NB = BUCKET_SAT // DIFF_T
DIFF_NT = 2 * DIFF_NB + 3
DIFF_GROUP = 4

PROJ_TM = 512
OUT_TM = 512
FFN_TM = 512
FFN_FC = 256
FFN_HALO = BF16_SUBLANES

_BUCKET_THRESHOLDS = (1, 2, 3, 4, 5, 6, 7, 8, 16, 32, 64, 128, 256, 512, 1024)


def _params(*sem):
    return pltpu.CompilerParams(dimension_semantics=sem, vmem_limit_bytes=VMEM_LIMIT)


def _ada_kernel(c_ref, w_ref, b_ref, o_ref):
    c = c_ref[...]
    ca = c / (1.0 + jnp.exp(-c))
    o_ref[...] = jnp.dot(ca, w_ref[...], preferred_element_type=F32) + b_ref[...]


def _ada(c8, w, b):
    n = w.shape[1]
    tn = 1024
    return pl.pallas_call(
        _ada_kernel,
        grid=(n // tn,),
        in_specs=[pl.BlockSpec((8, D_MODEL), lambda j: (0, 0)),
                  pl.BlockSpec((D_MODEL, tn), lambda j: (0, j)),
                  pl.BlockSpec((1, tn), lambda j: (0, j))],
        out_specs=pl.BlockSpec((8, tn), lambda j: (0, j)),
        out_shape=jax.ShapeDtypeStruct((8, n), F32),
        compiler_params=_params("arbitrary"),
        name="ada",
    )(c8, w, b)


def _head_norm(p, bd, gain):
    ss = jnp.dot((p * p).astype(BF16), bd, preferred_element_type=F32)
    return p * lax.rsqrt(ss * (1.0 / HEAD_DIM) + EPS) * gain


def _in_proj_kernel(x_ref, shift_ref, scale_ref, g_ref, w_ref, bd_ref, gqa_ref, gka_ref, gqb_ref, gkb_ref,
                    qa_ref, ka_ref, va_ref, qa4_ref, ka4_ref, va4_ref, qa16_ref, ka16_ref, va16_ref,
                    qb_ref, k1_ref, k2_ref, vbt_ref, pbuf, stage, stage4):
    tm = x_ref.shape[0]
    x = x_ref[...]
    ms = jnp.mean(x * x, axis=-1, keepdims=True)
    hn = x * lax.rsqrt(ms + EPS) * g_ref[...]
    hn = (hn * (1.0 + scale_ref[...]) + shift_ref[...]).astype(BF16)
    pbuf[...] = jnp.dot(hn, w_ref[...], preferred_element_type=F32)
    bd = bd_ref[...]
    lane = lax.broadcasted_iota(jnp.int32, (1, MXU_DIM), 1)
    first = (lane // HEAD_DIM) % 2 == 0

    def proj(g, c):
        lo = g * 512 + c * MXU_DIM
        return pbuf[:, lo:lo + MXU_DIM]

    for c in range(2):
        sl = slice(c * MXU_DIM, (c + 1) * MXU_DIM)
        group_a = (_head_norm(proj(0, c), bd, gqa_ref[:, sl]), _head_norm(proj(1, c), bd, gka_ref[:, sl]),
                   proj(2, c))
        for a, (val, ref) in enumerate(zip(group_a, (qa_ref, ka_ref, va_ref))):
            ref[:, sl] = val.astype(BF16)
            stage[a, 2 * c] = val[:, :LANES]
            stage[a, 2 * c + 1] = val[:, LANES:]
        qb_ref[:, sl] = _head_norm(proj(3, c), bd, gqb_ref[:, sl]).astype(BF16)
        kb = _head_norm(proj(4, c), bd, gkb_ref[:, sl])
        k1_ref[:, sl] = jnp.where(first, kb, 0.0).astype(BF16)
        k2_ref[:, sl] = jnp.where(first, 0.0, kb).astype(BF16)
        vbt_ref[sl, :] = proj(5, c).T.astype(BF16)

    for a, (ref4, ref16) in enumerate(((qa4_ref, qa16_ref), (ka4_ref, ka16_ref), (va4_ref, va16_ref))):
        for lo_r in range(4):
            for s in range(WIDTH_A // LANES):
                part = stage[a, s, pl.ds(lo_r, tm // 4, stride=4), :]
                stage4[a, lo_r, s] = part
                ref4[:, lo_r * WIDTH_A + s * LANES:lo_r * WIDTH_A + (s + 1) * LANES] = part.astype(BF16)
        for hi_r in range(4):
            for lo_r in range(4):
                for s in range(WIDTH_A // LANES):
                    lo = (4 * hi_r + lo_r) * WIDTH_A + s * LANES
                    part = stage4[a, lo_r, s, pl.ds(hi_r, tm // 16, stride=4), :]
                    ref16[:, lo:lo + LANES] = part.astype(BF16)


def _in_proj(x, shift, scale, g, w_bf, bd, gqa, gka, gqb, gkb):
    B, S, D = x.shape
    tm = PROJ_TM
    row = pl.BlockSpec((None, tm, D), lambda b, i: (b, i, 0))
    mod = pl.BlockSpec((None, 1, D), lambda b, i: (b, 0, 0))
    const = lambda shape: pl.BlockSpec(shape, lambda b, i: (0,) * len(shape))
    out = pl.BlockSpec((None, tm, 512), lambda b, i: (b, i, 0))
    out_t = pl.BlockSpec((None, 512, tm), lambda b, i: (b, 0, i))
    res = lambda dil: pl.BlockSpec((None, tm // dil, dil * WIDTH_A), lambda b, i: (b, i, 0))
    res_shape = lambda dil: jax.ShapeDtypeStruct((B, S // dil, dil * WIDTH_A), BF16)
    nat_shape = jax.ShapeDtypeStruct((B, S, 512), BF16)
    return pl.pallas_call(
        _in_proj_kernel,
        grid=(B, S // tm),
        in_specs=[row, mod, mod, const((1, D)), const(w_bf.shape), const(bd.shape),
                  const((1, 512)), const((1, 512)), const((1, 512)), const((1, 512))],
        out_specs=[out] * 3 + [res(4)] * 3 + [res(16)] * 3 + [out] * 3 + [out_t],
        out_shape=[nat_shape] * 3 + [res_shape(4)] * 3 + [res_shape(16)] * 3 + [nat_shape] * 3
                  + [jax.ShapeDtypeStruct((B, 512, S), BF16)],
        scratch_shapes=[pltpu.VMEM((tm, w_bf.shape[1]), F32),
                        pltpu.VMEM((3, WIDTH_A // LANES, tm, LANES), F32),
                        pltpu.VMEM((3, 4, WIDTH_A // LANES, tm // 4, LANES), F32)],
        compiler_params=_params("arbitrary", "arbitrary"),
        name="in_proj",
    )(x, shift, scale, g, w_bf, bd, gqa, gka, gqb, gkb)


def _bias_from_rel(rel, tbl_ref, h):
    n = jnp.abs(rel)
    vneg = jnp.full(rel.shape, tbl_ref[0, h], F32)
    vpos = jnp.full(rel.shape, tbl_ref[NUM_BUCKETS // 2, h], F32)
    for i, th in enumerate(_BUCKET_THRESHOLDS):
        ge = n >= th
        vneg = jnp.where(ge, tbl_ref[i + 1, h], vneg)
        vpos = jnp.where(ge, tbl_ref[NUM_BUCKETS // 2 + i + 1, h], vpos)
    return jnp.where(rel > 0, vpos, vneg)


def _diff_bias_kernel(tbl_ref, o_ref, omax_ref):
    h = pl.program_id(0)
    t = pl.program_id(1)
    shape = (DIFF_T, DIFF_T)
    k = lax.broadcasted_iota(jnp.int32, shape, 0)
    q = lax.broadcasted_iota(jnp.int32, shape, 1)
    rel = k - q + (t - (DIFF_NB + 1)) * DIFF_T
    bias = _bias_from_rel(rel, tbl_ref, N_HEADS_A + h) * LOG2E
    o_ref[...] = bias
    omax_ref[...] = jnp.max(bias, axis=0, keepdims=True)


def _diff_bias(rel_bias):
    return pl.pallas_call(
        _diff_bias_kernel,
        grid=(N_HEADS_B, DIFF_NT),
        in_specs=[pl.BlockSpec(memory_space=pltpu.SMEM)],
        out_specs=[pl.BlockSpec((None, None, DIFF_T, DIFF_T), lambda h, t: (h, t, 0, 0)),
                   pl.BlockSpec((None, None, 1, DIFF_T), lambda h, t: (h, t, 0, 0))],
        out_shape=[jax.ShapeDtypeStruct((N_HEADS_B, DIFF_NT, DIFF_T, DIFF_T), F32),
                   jax.ShapeDtypeStruct((N_HEADS_B, DIFF_NT, 1, DIFF_T), F32)],
        compiler_params=_params("arbitrary", "arbitrary"),
        name="diff_bias",
    )(rel_bias)


def _dil_bias_kernel(tbl_ref, o_ref):
    p = pl.program_id(0)
    hp = pl.program_id(1)
    shape = (DIL_K, DIL_Q)
    k = lax.broadcasted_iota(jnp.int32, shape, 0)
    q = lax.broadcasted_iota(jnp.int32, shape, 1)
    steps = k - HALF_WIN - q
    dil = lax.shift_left(jnp.int32(1), 2 * p)
    inside = jnp.abs(steps) <= HALF_WIN
    for par in range(2):
        bias = jnp.where(inside, _bias_from_rel(steps * dil, tbl_ref, 2 * hp + par) * LOG2E, NEG)
        ls = slice(par * DIL_Q, (par + 1) * DIL_Q)
        o_ref[0, :, ls] = bias
        o_ref[1, :, ls] = jnp.where(k >= HALF_WIN, bias, NEG)
        o_ref[2, :, ls] = jnp.where(k < DIL_K - HALF_WIN, bias, NEG)


def _dil_bias(rel_bias):
    n_pat = len(DILATED_PATTERNS)
    n_pair = N_HEADS_A // 2
    return pl.pallas_call(
        _dil_bias_kernel,
        grid=(n_pat, n_pair),
        in_specs=[pl.BlockSpec(memory_space=pltpu.SMEM)],
        out_specs=pl.BlockSpec((None, 3, None, DIL_K, 2 * DIL_Q), lambda p, hp: (p, 0, hp, 0, 0)),
        out_shape=jax.ShapeDtypeStruct((n_pat, 3, n_pair, DIL_K, 2 * DIL_Q), F32),
        compiler_params=_params("arbitrary", "arbitrary"),
        name="dil_bias",
    )(rel_bias)


def _dil_kernel(q_ref, kp_ref, kc_ref, kn_ref, vp_ref, vc_ref, vn_ref, bias_ref, o_ref, lse_ref, kbuf, vtbuf):
    tl = q_ref.shape[0]
    i = pl.program_id(2)
    n_sub = tl // DIL_Q
    n_pair = N_HEADS_A // 2
    kbuf[0:HALF_WIN, :] = kp_ref[...]
    kbuf[HALF_WIN:HALF_WIN + tl, :] = kc_ref[...]
    kbuf[HALF_WIN + tl:, :] = kn_ref[...]
    for hp in range(n_pair):
        ls = slice(hp * LANES, (hp + 1) * LANES)
        v = jnp.concatenate([vp_ref[:, ls], vc_ref[:, ls], vn_ref[:, ls]], axis=0)
        vtbuf[hp] = v.astype(F32).T.astype(BF16)
    lane = lax.broadcasted_iota(jnp.int32, (1, LANES), 1)
    even = lane < HEAD_DIM
    contract_last = (((1,), (1,)), ((), ()))
    first_step = i == 0
    last_step = i == pl.num_programs(2) - 1

    def scores(sb, hp):
        r0 = sb * DIL_Q
        ls = slice(hp * LANES, (hp + 1) * LANES)
        if sb == 0:
            variant = jnp.where(first_step, 1, 0)
        elif sb == n_sub - 1:
            variant = jnp.where(last_step, 2, 0)
        else:
            variant = 0
        q = q_ref[r0:r0 + DIL_Q, ls]
        zero = jnp.zeros_like(q)
        qcat = jnp.concatenate([jnp.where(even, q, zero), jnp.where(even, zero, q)], axis=0)
        s = lax.dot_general(kbuf[r0:r0 + DIL_K, ls], qcat, contract_last, preferred_element_type=F32)
        return s + bias_ref[variant, hp]

    def finish(sb, hp, s):
        r0 = sb * DIL_Q
        ls = slice(hp * LANES, (hp + 1) * LANES)
        m = jnp.max(s, axis=0, keepdims=True)
        p = jnp.exp2(s - m)
        l = jnp.sum(p, axis=0, keepdims=True)
        ot = jnp.dot(vtbuf[hp, :, r0:r0 + DIL_K], p.astype(BF16), preferred_element_type=F32) / l
        lse = jnp.broadcast_to((m + jnp.log2(l)) * (1.0 / LOG2E), (HEAD_DIM, 2 * DIL_Q))
        ot = jnp.concatenate([ot[:HEAD_DIM, :DIL_Q], ot[HEAD_DIM:, DIL_Q:]], axis=0)
        lse = jnp.concatenate([lse[:, :DIL_Q], lse[:, DIL_Q:]], axis=0)
        o_ref[r0:r0 + DIL_Q, ls] = ot.T.astype(o_ref.dtype)
        lse_ref[r0:r0 + DIL_Q, ls] = lse.T

    bodies = [(sb, hp) for sb in range(n_sub) for hp in range(n_pair)]
    staged = [scores(*b) for b in bodies[:DIL_AHEAD]]
    for n, b in enumerate(bodies):
        if n + DIL_AHEAD < len(bodies):
            staged.append(scores(*bodies[n + DIL_AHEAD]))
        finish(*b, staged[n])


def _dilated(q, k, v, bias, dil):
    B, L, _ = q.shape
    W = WIDTH_A
    tl = min(DIL_TL, L)
    nhalo = L // HALF_WIN
    cur = pl.BlockSpec((None, tl, W), lambda b, r, i: (b, i, r))
    prev = pl.BlockSpec((None, HALF_WIN, W),
                        lambda b, r, i: (b, jnp.maximum(i * (tl // HALF_WIN) - 1, 0), r))
    nxt = pl.BlockSpec((None, HALF_WIN, W),
                       lambda b, r, i: (b, jnp.minimum((i + 1) * (tl // HALF_WIN), nhalo - 1), r))
    bias_spec = pl.BlockSpec(bias.shape, lambda b, r, i: (0,) * bias.ndim)
    return pl.pallas_call(
        _dil_kernel,
        grid=(B, dil, L // tl),
        in_specs=[cur, prev, cur, nxt, prev, cur, nxt, bias_spec],
        out_specs=[cur, cur],
        out_shape=[jax.ShapeDtypeStruct((B, L, dil * W), BF16), jax.ShapeDtypeStruct((B, L, dil * W), F32)],
        scratch_shapes=[pltpu.VMEM((tl + 2 * HALF_WIN, W), BF16),
                        pltpu.VMEM((N_HEADS_A // 2, LANES, tl + 2 * HALF_WIN), BF16)],
        compiler_params=_params("arbitrary", "arbitrary", "arbitrary"),
        name=f"dilated_{dil}",
    )(q, k, k, k, v, v, v, bias)


def _diff_kernel(tbl_ref, q_ref, qn_ref, k1_ref, k2_ref, vt_ref, band_ref, bmax_ref, lq1_ref, lk1_ref, lq2_ref,
                 lk2_ref, g_ref, o_ref, m1, l1, a1, m2, l2, a2, sa, sb, mxa, mxb, *, lambda_init):
    h = pl.program_id(1)
    i = pl.program_id(2)
    T = q_ref.shape[0]
    nk = k1_ref.shape[0] // T
    streams = ((k1_ref, m1, l1, a1), (k2_ref, m2, l2, a2))
    for _, m, l, a in streams:
        m[...] = jnp.full(m.shape, NEG, F32)
        l[...] = jnp.zeros(l.shape, F32)
        a[...] = jnp.zeros(a.shape, F32)
    q = q_ref[...]
    contract_last = (((1,), (1,)), ((), ()))

    def scores(j, sbuf, mxbuf, st, wraps=False):
        if wraps:
            wrap = j == nk
            j = jnp.where(wrap, 0, j)
            qq = jnp.where(wrap, qn_ref[...], q)
        else:
            qq = q
        r0 = pl.multiple_of(j * T, T)
        s = lax.dot_general(streams[st][0][pl.ds(r0, T), :], qq, contract_last, preferred_element_type=F32)
        sbuf[st] = s
        mxbuf[st] = jnp.max(s, axis=0, keepdims=True)

    def softmax_pv(j, sbuf, mxbuf, st, near, shift):
        r0 = pl.multiple_of(j * T, T)
        _, m, l, a = streams[st]
        s = sbuf[st]
        if near:
            t = jnp.clip(j - i, -(DIFF_NB + 1), DIFF_NB + 1) + (DIFF_NB + 1)
            s = s + band_ref[t]
            bound = mxbuf[st] + bmax_ref[t]
        else:
            bound = mxbuf[st] + shift
        m_prev = m[...]
        m_new = jnp.maximum(m_prev, bound)
        alpha = jnp.exp2(m_prev - m_new)
        p = jnp.exp2(s - (m_new - shift))
        l[...] = alpha * l[...] + jnp.sum(p, axis=0, keepdims=True)
        a[...] = alpha * a[...] + jnp.dot(vt_ref[:, pl.ds(r0, T)], p.astype(BF16), preferred_element_type=F32)
        m[...] = m_new

    def group(near, shift):
        def body(g, carry):
            j0 = DIFF_GROUP * g
            bufs = ((sa, mxa), (sb, mxb))
            for u in range(DIFF_GROUP):
                cur, nxt = bufs[u % 2], bufs[(u + 1) % 2]
                for st in range(2):
                    scores(j0 + u + 1, *nxt, st, wraps=u == DIFF_GROUP - 1)
                    softmax_pv(j0 + u, *cur, st, near, shift)
            return carry
        return body

    @pl.when(i == 0)
    def _():
        scores(0, sa, mxa, 0)
        scores(0, sa, mxa, 1)

    lo = jnp.maximum((i - DIFF_NB) // DIFF_GROUP, 0)
    hi = jnp.minimum((i + DIFF_NB) // DIFF_GROUP + 1, nk // DIFF_GROUP)
    c_neg = tbl_ref[NUM_BUCKETS // 2 - 1, N_HEADS_A + h] * LOG2E
    c_pos = tbl_ref[NUM_BUCKETS - 1, N_HEADS_A + h] * LOG2E
    lax.fori_loop(0, lo, group(False, c_neg), 0)
    lax.fori_loop(lo, hi, group(True, 0.0), 0)
    lax.fori_loop(hi, nk // DIFF_GROUP, group(False, c_pos), 0)

    lam =(jnp.exp(jnp.sum(lq1_ref[...] * lk1_ref[...], axis=-1, keepdims=True))
           - jnp.exp(jnp.sum(lq2_ref[...] * lk2_ref[...], axis=-1, keepdims=True)) + lambda_init)
    o = a1[...] / l1[...] - lam * (a2[...] / l2[...])
    ms = jnp.mean(o * o, axis=0, keepdims=True)
    o = o * lax.rsqrt(ms + EPS) * (g_ref[...] * (1.0 - lambda_init))
    o_ref[...] = o.T.astype(o_ref.dtype)


def _diff_attention(rel_bias, qb, k1, k2, vbt, band, bmax, lq1, lk1, lq2, lk2, g_col, lambda_init):
    B, S, W = qb.shape
    T = DIFF_T
    qspec = pl.BlockSpec((None, T, LANES), lambda b, h, i: (b, i, h))
    qnext = pl.BlockSpec((None, T, LANES), lambda b, h, i: (b, jnp.minimum(i + 1, S // T - 1), h))
    kspec = pl.BlockSpec((None, S, LANES), lambda b, h, i: (b, 0, h))
    vspec = pl.BlockSpec((None, LANES, S), lambda b, h, i: (b, h, 0))
    band_spec = pl.BlockSpec((None, DIFF_NT, T, T), lambda b, h, i: (h, 0, 0, 0))
    bmax_spec = pl.BlockSpec((None, DIFF_NT, 1, T), lambda b, h, i: (h, 0, 0, 0))
    vec = lambda n: pl.BlockSpec((1, n), lambda b, h, i: (0, 0))
    stat = pltpu.VMEM((1, T), F32)
    acc = pltpu.VMEM((LANES, T), F32)
    return pl.pallas_call(
        functools.partial(_diff_kernel, lambda_init=lambda_init),
        grid=(B, N_HEADS_B, S // T),
        in_specs=[pl.BlockSpec(memory_space=pltpu.SMEM), qspec, qnext, kspec, kspec, vspec, band_spec, bmax_spec,
                  vec(HEAD_DIM), vec(HEAD_DIM), vec(HEAD_DIM), vec(HEAD_DIM),
                  pl.BlockSpec((LANES, 1), lambda b, h, i: (0, 0))],
        out_specs=qspec,
        out_shape=jax.ShapeDtypeStruct((B, S, W), BF16),
        scratch_shapes=[stat, stat, acc, stat, stat, acc, pltpu.VMEM((2, T, T), F32), pltpu.VMEM((2, T, T), F32),
                        pltpu.VMEM((2, 1, T), F32), pltpu.VMEM((2, 1, T), F32)],
        compiler_params=_params("arbitrary", "arbitrary", "arbitrary"),
        name="diff_attn",
    )(rel_bias, qb, qb, k1, k2, vbt, band, bmax, lq1, lk1, lq2, lk2, g_col)


def _out_proj_kernel(o1_ref, o4_ref, o16_ref, s1_ref, s4_ref, s16_ref, ob_ref, x_ref, gate_ref, shift_ref,
                     scale_ref, g_ref, wa_ref, wb_ref, h_ref, hn_ref, nat, oa_buf):
    tm = x_ref.shape[0]
    for n, (dil, ref) in enumerate(((4, o4_ref), (4, s4_ref), (16, o16_ref), (16, s16_ref))):
        rows = tm // dil
        for r in range(dil):
            for s in range(WIDTH_A // LANES):
                lo = r * WIDTH_A + s * LANES
                nat[n, s, pl.ds(r, rows, stride=dil), :] = ref[:, lo:lo + LANES].astype(F32)
    for s in range(WIDTH_A // LANES):
        sl = slice(s * LANES, (s + 1) * LANES)
        s1, s2, s3 = s1_ref[:, sl], nat[1, s], nat[3, s]
        mx = jnp.maximum(jnp.maximum(s1, s2), s3)
        e1, e2, e3 = jnp.exp(s1 - mx), jnp.exp(s2 - mx), jnp.exp(s3 - mx)
        oa = (e1 * o1_ref[:, sl].astype(F32) + e2 * nat[0, s] + e3 * nat[2, s]) / (e1 + e2 + e3)
        oa_buf[:, sl] = oa.astype(BF16)
    mixed = (jnp.dot(oa_buf[...], wa_ref[...], preferred_element_type=F32)
             + jnp.dot(ob_ref[...], wb_ref[...], preferred_element_type=F32))
    h = x_ref[...] + gate_ref[...] * mixed
    h_ref[...] = h
    ms = jnp.mean(h * h, axis=-1, keepdims=True)
    hn = h * lax.rsqrt(ms + EPS) * g_ref[...]
    hn_ref[...] = (hn * (1.0 + scale_ref[...]) + shift_ref[...]).astype(hn_ref.dtype)


def _out_proj(oas, lses, ob, x, gate, shift, scale, g, wa, wb):
    B, S, D = x.shape
    tm = OUT_TM
    half = pl.BlockSpec((None, tm, 512), lambda b, i: (b, i, 0))
    row = pl.BlockSpec((None, tm, D), lambda b, i: (b, i, 0))
    mod = pl.BlockSpec((None, 1, D), lambda b, i: (b, 0, 0))
    const = lambda shape: pl.BlockSpec(shape, lambda b, i: (0,) * len(shape))
    res = lambda dil: pl.BlockSpec((None, tm // dil, dil * WIDTH_A), lambda b, i: (b, i, 0))
    pats = [half, res(4), res(16)]
    return pl.pallas_call(
        _out_proj_kernel,
        grid=(B, S // tm),
        in_specs=pats + pats + [half, row, mod, mod, mod, const((1, D)), const(wa.shape), const(wb.shape)],
        out_specs=[row, row],
        out_shape=[jax.ShapeDtypeStruct((B, S, D), F32), jax.ShapeDtypeStruct((B, S, D), BF16)],
        scratch_shapes=[pltpu.VMEM((4, WIDTH_A // LANES, tm, LANES), F32), pltpu.VMEM((tm, WIDTH_A), BF16)],
        compiler_params=_params("arbitrary", "arbitrary"),
        name="out_proj",
    )(*oas, *lses, ob, x, gate, shift, scale, g, wa, wb)


def _ffn_kernel(hp_ref, hc_ref, hn_ref, h_ref, gate_ref, wu_ref, cw_ref, cb_ref, wd_ref, o_ref, lhs, act):
    i = pl.program_id(1)
    tm = hc_ref.shape[0]
    n = tm + 2 * FFN_HALO
    lhs[0:FFN_HALO, :] = jnp.where(i > 0, hp_ref[...], jnp.zeros_like(hp_ref))
    lhs[FFN_HALO:FFN_HALO + tm, :] = hc_ref[...]
    lhs[FFN_HALO + tm:, :] = jnp.where(i < pl.num_programs(1) - 1, hn_ref[...], jnp.zeros_like(hn_ref))
    x = lhs[...]

    def conv(u, lo):
        cw = cw_ref[:, lo:lo + FFN_FC]
        mid = slice(FFN_HALO, FFN_HALO + tm)
        below = pltpu.roll(u, 1, 0)[mid]
        above = pltpu.roll(u, n - 1, 0)[mid]
        return cw[0:1] * below + cw[1:2] * u[mid] + cw[2:3] * above + cb_ref[:, lo:lo + FFN_FC]

    for c in range(D_FF // FFN_FC):
        lo_v = c * FFN_FC
        lo_g = D_FF + c * FFN_FC
        val = conv(jnp.dot(x, wu_ref[:, lo_v:lo_v + FFN_FC], preferred_element_type=F32), lo_v)
        gt = conv(jnp.dot(x, wu_ref[:, lo_g:lo_g + FFN_FC], preferred_element_type=F32), lo_g)
        act[:, lo_v:lo_v + FFN_FC] = (gt / (1.0 + jnp.exp(-gt)) * val).astype(BF16)
    down = jnp.dot(act[...], wd_ref[...], preferred_element_type=F32)
    o_ref[...] = h_ref[...] + gate_ref[...] * down


def _ffn(hn, h, gate, wu, cw, cb, wd):
    B, S, D = h.shape
    tm = FFN_TM
    per = tm // FFN_HALO
    nh = S // FFN_HALO
    row = lambda dt: pl.BlockSpec((None, tm, D), lambda b, i: (b, i, 0))
    prev = pl.BlockSpec((None, FFN_HALO, D), lambda b, i: (b, jnp.maximum(i * per - 1, 0), 0))
    nxt = pl.BlockSpec((None, FFN_HALO, D), lambda b, i: (b, jnp.minimum((i + 1) * per, nh - 1), 0))
    mod = pl.BlockSpec((None, 1, D), lambda b, i: (b, 0, 0))
    const = lambda shape: pl.BlockSpec(shape, lambda b, i: (0,) * len(shape), pipeline_mode=pl.Buffered(1))
    return pl.pallas_call(
        _ffn_kernel,
        grid=(B, S // tm),
        in_specs=[prev, row(BF16), nxt, row(F32), mod, const(wu.shape), const(cw.shape), const(cb.shape),
                  const(wd.shape)],
        out_specs=row(F32),
        out_shape=jax.ShapeDtypeStruct((B, S, D), F32),
        scratch_shapes=[pltpu.VMEM((tm + 2 * FFN_HALO, D), BF16), pltpu.VMEM((tm, D_FF), BF16)],
        compiler_params=_params("arbitrary", "arbitrary"),
        name="conv_ffn",
    )(hn, hn, hn, h, gate, wu, cw, cb, wd)


def _block_diag_ones():
    r = jnp.arange(MXU_DIM) // HEAD_DIM
    return (r[:, None] == r[None, :]).astype(BF16)


def kernel(x, c, w_ada, b_ada, norm1_g, w_in, q_norm_a, k_norm_a, q_norm_b, k_norm_b, rel_bias, lambda_q1,
           lambda_k1, lambda_q2, lambda_k2, subln_g, w_out, norm2_g, w_up, conv_w, conv_b, w_down):
    B, S, D = x.shape
    depth = w_ada.shape[0]
    h = x.astype(F32)
    c8 = jnp.pad(c.astype(F32), ((0, 8 - B), (0, 0)))
    bd = _block_diag_ones()
    rel_bias = rel_bias.astype(F32)
    band_b, bmax_b = _diff_bias(rel_bias)
    band_a = _dil_bias(rel_bias)
    qscale = HEAD_DIM ** -0.5
    tile8 = lambda g: jnp.tile(g.astype(F32), WIDTH_A // HEAD_DIM)[None, :]
    row = lambda v: v.astype(F32)[None, :]

    for layer in range(depth):
        lambda_init = 0.8 - 0.6 * math.exp(-0.3 * layer)
        mod = _ada(c8, w_ada[layer].astype(F32), row(b_ada[layer]))[:B]
        shift1, scale1, gate1, shift2, scale2, gate2 = [m[:, None, :] for m in jnp.split(mod, 6, axis=-1)]

        qa, ka, va, qa4, ka4, va4, qa16, ka16, va16, qb, k1, k2, vbt = _in_proj(
            h, shift1, scale1, row(norm1_g[layer]), w_in[layer].astype(BF16), bd,
            tile8(q_norm_a[layer]) * (qscale * LOG2E), tile8(k_norm_a[layer]),
            tile8(q_norm_b[layer]) * (qscale * LOG2E), tile8(k_norm_b[layer]))

        oas, lses = [], []
        qkv = {1: (qa, ka, va), 4: (qa4, ka4, va4), 16: (qa16, ka16, va16)}
        for p, (_, dil) in enumerate(DILATED_PATTERNS):
            o, lse = _dilated(*qkv[dil], band_a[p], dil)
            oas.append(o)
            lses.append(lse)

        ob = _diff_attention(rel_bias, qb, k1, k2, vbt, band_b, bmax_b, row(lambda_q1[layer]), row(lambda_k1[layer]),
                             row(lambda_q2[layer]), row(lambda_k2[layer]),
                             subln_g[layer].astype(F32)[:, None], lambda_init)

        w_o = w_out[layer].astype(BF16)
        h, hn = _out_proj(oas, lses, ob, h, gate1, shift2, scale2, row(norm2_g[layer]),
                          w_o[:WIDTH_A], w_o[WIDTH_A:])
        h = _ffn(hn, h, gate2, w_up[layer].astype(BF16), conv_w[layer].astype(F32), row(conv_b[layer]),
                 w_down[layer].astype(BF16))

    return h.astype(x.dtype)
```

```python
import functools
import math

import jax
import jax.numpy as jnp
from jax import lax
from jax.experimental import pallas as pl
from jax.experimental.pallas import tpu as pltpu

F32 = jnp.float32
BF16 = jnp.bfloat16

D_MODEL = 1024
HEAD_DIM = 64
WIDTH_A = 512
WIDTH_B = 512
N_HEADS_A = 8
N_HEADS_B = 4
DILATED_PATTERNS = ((128, 1), (512, 4), (2048, 16))
D_FF = 2816
NUM_BUCKETS = 32
MAX_DISTANCE = 2048
EPS = 1e-6
NEG = -1e30
LOG2E = math.log2(math.e)

LANES = 128
BF16_SUBLANES = 16
MXU_DIM = 256
VMEM_LIMIT = 56 * 1024 * 1024

HALF_WIN = 64
DIL_Q = 128
DIL_K = DIL_Q + 2 * HALF_WIN
DIL_TL = 512
LSE_REP = LANES // N_HEADS_A
DIL_AHEAD = 4

DIFF_T = 512
BUCKET_SAT = 1024
DIFF_NB = BUCKET_SAT // DIFF_T
DIFF_NT = 2 * DIFF_NB + 3
DIFF_GROUP = 8

PROJ_TM = 512
OUT_TM = 512
FFN_TM = 512
FFN_FC = 256
FFN_HALO = BF16_SUBLANES

_BUCKET_THRESHOLDS = (1, 2, 3, 4, 5, 6, 7, 8, 16, 32, 64, 128, 256, 512, 1024)


def _params(*sem):
    return pltpu.CompilerParams(dimension_semantics=sem, vmem_limit_bytes=VMEM_LIMIT)


def _ada_kernel(c_ref, w_ref, b_ref, o_ref):
    c = c_ref[...]
    ca = c / (1.0 + jnp.exp(-c))
    o_ref[...] = jnp.dot(ca, w_ref[...], preferred_element_type=F32) + b_ref[...]


def _ada(c8, w, b):
    n = w.shape[1]
    tn = 1024
    return pl.pallas_call(
        _ada_kernel,
        grid=(n // tn,),
        in_specs=[pl.BlockSpec((8, D_MODEL), lambda j: (0, 0)),
                  pl.BlockSpec((D_MODEL, tn), lambda j: (0, j)),
                  pl.BlockSpec((1, tn), lambda j: (0, j))],
        out_specs=pl.BlockSpec((8, tn), lambda j: (0, j)),
        out_shape=jax.ShapeDtypeStruct((8, n), F32),
        compiler_params=_params("arbitrary"),
        name="ada",
    )(c8, w, b)


def _head_norm(p, bd, gain):
    ss = jnp.dot((p * p).astype(BF16), bd, preferred_element_type=F32)
    return p * lax.rsqrt(ss * (1.0 / HEAD_DIM) + EPS) * gain


def _in_proj_kernel(x_ref, shift_ref, scale_ref, g_ref, w_ref, bd_ref, gqa_ref, gka_ref, gqb_ref, gkb_ref,
                    qa_ref, ka_ref, va_ref, qa4_ref, ka4_ref, va4_ref, qa16_ref, ka16_ref, va16_ref,
                    qb_ref, k1_ref, k2_ref, vbt_ref, pbuf, stage, stage4):
    tm = x_ref.shape[0]
    x = x_ref[...]
    ms = jnp.mean(x * x, axis=-1, keepdims=True)
    hn = x * lax.rsqrt(ms + EPS) * g_ref[...]
    hn = (hn * (1.0 + scale_ref[...]) + shift_ref[...]).astype(BF16)
    pbuf[...] = jnp.dot(hn, w_ref[...], preferred_element_type=F32)
    bd = bd_ref[...]
    lane = lax.broadcasted_iota(jnp.int32, (1, MXU_DIM), 1)
    first = (lane // HEAD_DIM) % 2 == 0

    def proj(g, c):
        lo = g * 512 + c * MXU_DIM
        return pbuf[:, lo:lo + MXU_DIM]

    for c in range(2):
        sl = slice(c * MXU_DIM, (c + 1) * MXU_DIM)
        group_a = (_head_norm(proj(0, c), bd, gqa_ref[:, sl]), _head_norm(proj(1, c), bd, gka_ref[:, sl]),
                   proj(2, c))
        for a, (val, ref) in enumerate(zip(group_a, (qa_ref, ka_ref, va_ref))):
            ref[:, sl] = val.astype(BF16)
            stage[a, 2 * c] = val[:, :LANES]
            stage[a, 2 * c + 1] = val[:, LANES:]
        qb_ref[:, sl] = _head_norm(proj(3, c), bd, gqb_ref[:, sl]).astype(BF16)
        kb = _head_norm(proj(4, c), bd, gkb_ref[:, sl])
        k1_ref[:, sl] = jnp.where(first, kb, 0.0).astype(BF16)
        k2_ref[:, sl] = jnp.where(first, 0.0, kb).astype(BF16)
        vbt_ref[sl, :] = proj(5, c).T.astype(BF16)

    for a, (ref4, ref16) in enumerate(((qa4_ref, qa16_ref), (ka4_ref, ka16_ref), (va4_ref, va16_ref))):
        for lo_r in range(4):
            for s in range(WIDTH_A // LANES):
                part = stage[a, s, pl.ds(lo_r, tm // 4, stride=4), :]
                stage4[a, lo_r, s] = part
                ref4[:, lo_r * WIDTH_A + s * LANES:lo_r * WIDTH_A + (s + 1) * LANES] = part.astype(BF16)
        for hi_r in range(4):
            for lo_r in range(4):
                for s in range(WIDTH_A // LANES):
                    lo = (4 * hi_r + lo_r) * WIDTH_A + s * LANES
                    part = stage4[a, lo_r, s, pl.ds(hi_r, tm // 16, stride=4), :]
                    ref16[:, lo:lo + LANES] = part.astype(BF16)


def _in_proj(x, shift, scale, g, w_bf, bd, gqa, gka, gqb, gkb):
    B, S, D = x.shape
    tm = PROJ_TM
    row = pl.BlockSpec((None, tm, D), lambda b, i: (b, i, 0))
    mod = pl.BlockSpec((None, 1, D), lambda b, i: (b, 0, 0))
    const = lambda shape: pl.BlockSpec(shape, lambda b, i: (0,) * len(shape))
    out = pl.BlockSpec((None, tm, 512), lambda b, i: (b, i, 0))
    out_t = pl.BlockSpec((None, 512, tm), lambda b, i: (b, 0, i))
    res = lambda dil: pl.BlockSpec((None, tm // dil, dil * WIDTH_A), lambda b, i: (b, i, 0))
    res_shape = lambda dil: jax.ShapeDtypeStruct((B, S // dil, dil * WIDTH_A), BF16)
    nat_shape = jax.ShapeDtypeStruct((B, S, 512), BF16)
    return pl.pallas_call(
        _in_proj_kernel,
        grid=(B, S // tm),
        in_specs=[row, mod, mod, const((1, D)), const(w_bf.shape), const(bd.shape),
                  const((1, 512)), const((1, 512)), const((1, 512)), const((1, 512))],
        out_specs=[out] * 3 + [res(4)] * 3 + [res(16)] * 3 + [out] * 3 + [out_t],
        out_shape=[nat_shape] * 3 + [res_shape(4)] * 3 + [res_shape(16)] * 3 + [nat_shape] * 3
                  + [jax.ShapeDtypeStruct((B, 512, S), BF16)],
        scratch_shapes=[pltpu.VMEM((tm, w_bf.shape[1]), F32),
                        pltpu.VMEM((3, WIDTH_A // LANES, tm, LANES), F32),
                        pltpu.VMEM((3, 4, WIDTH_A // LANES, tm // 4, LANES), F32)],
        compiler_params=_params("arbitrary", "arbitrary"),
        name="in_proj",
    )(x, shift, scale, g, w_bf, bd, gqa, gka, gqb, gkb)


def _bias_from_rel(rel, tbl_ref, h):
    n = jnp.abs(rel)
    vneg = jnp.full(rel.shape, tbl_ref[0, h], F32)
    vpos = jnp.full(rel.shape, tbl_ref[NUM_BUCKETS // 2, h], F32)
    for i, th in enumerate(_BUCKET_THRESHOLDS):
        ge = n >= th
        vneg = jnp.where(ge, tbl_ref[i + 1, h], vneg)
        vpos = jnp.where(ge, tbl_ref[NUM_BUCKETS // 2 + i + 1, h], vpos)
    return jnp.where(rel > 0, vpos, vneg)


def _diff_bias_kernel(tbl_ref, o_ref, omax_ref):
    h = pl.program_id(0)
    t = pl.program_id(1)
    shape = (DIFF_T, DIFF_T)
    k = lax.broadcasted_iota(jnp.int32, shape, 0)
    q = lax.broadcasted_iota(jnp.int32, shape, 1)
    rel = k - q + (t - (DIFF_NB + 1)) * DIFF_T
    bias = _bias_from_rel(rel, tbl_ref, N_HEADS_A + h) * LOG2E
    o_ref[...] = bias
    omax_ref[...] = jnp.max(bias, axis=0, keepdims=True)


def _diff_bias(rel_bias):
    return pl.pallas_call(
        _diff_bias_kernel,
        grid=(N_HEADS_B, DIFF_NT),
        in_specs=[pl.BlockSpec(memory_space=pltpu.SMEM)],
        out_specs=[pl.BlockSpec((None, None, DIFF_T, DIFF_T), lambda h, t: (h, t, 0, 0)),
                   pl.BlockSpec((None, None, 1, DIFF_T), lambda h, t: (h, t, 0, 0))],
        out_shape=[jax.ShapeDtypeStruct((N_HEADS_B, DIFF_NT, DIFF_T, DIFF_T), F32),
                   jax.ShapeDtypeStruct((N_HEADS_B, DIFF_NT, 1, DIFF_T), F32)],
        compiler_params=_params("arbitrary", "arbitrary"),
        name="diff_bias",
    )(rel_bias)


def _dil_bias_kernel(tbl_ref, o_ref):
    p = pl.program_id(0)
    hp = pl.program_id(1)
    shape = (DIL_K, DIL_Q)
    k = lax.broadcasted_iota(jnp.int32, shape, 0)
    q = lax.broadcasted_iota(jnp.int32, shape, 1)
    steps = k - HALF_WIN - q
    dil = lax.shift_left(jnp.int32(1), 2 * p)
    inside = jnp.abs(steps) <= HALF_WIN
    for par in range(2):
        bias = jnp.where(inside, _bias_from_rel(steps * dil, tbl_ref, 2 * hp + par) * LOG2E, NEG)
        ls = slice(par * DIL_Q, (par + 1) * DIL_Q)
        o_ref[0, :, ls] = bias
        o_ref[1, :, ls] = jnp.where(k >= HALF_WIN, bias, NEG)
        o_ref[2, :, ls] = jnp.where(k < DIL_K - HALF_WIN, bias, NEG)


def _dil_bias(rel_bias):
    n_pat = len(DILATED_PATTERNS)
    n_pair = N_HEADS_A // 2
    return pl.pallas_call(
        _dil_bias_kernel,
        grid=(n_pat, n_pair),
        in_specs=[pl.BlockSpec(memory_space=pltpu.SMEM)],
        out_specs=pl.BlockSpec((None, 3, None, DIL_K, 2 * DIL_Q), lambda p, hp: (p, 0, hp, 0, 0)),
        out_shape=jax.ShapeDtypeStruct((n_pat, 3, n_pair, DIL_K, 2 * DIL_Q), F32),
        compiler_params=_params("arbitrary", "arbitrary"),
        name="dil_bias",
    )(rel_bias)


def _dil_kernel(q_ref, kp_ref, kc_ref, kn_ref, vp_ref, vc_ref, vn_ref, bias_ref, o_ref, lse_ref, kbuf, vtbuf):
    tl = q_ref.shape[0]
    i = pl.program_id(2)
    n_sub = tl // DIL_Q
    n_pair = N_HEADS_A // 2
    kbuf[0:HALF_WIN, :] = kp_ref[...]
    kbuf[HALF_WIN:HALF_WIN + tl, :] = kc_ref[...]
    kbuf[HALF_WIN + tl:, :] = kn_ref[...]
    for hp in range(n_pair):
        ls = slice(hp * LANES, (hp + 1) * LANES)
        v = jnp.concatenate([vp_ref[:, ls], vc_ref[:, ls], vn_ref[:, ls]], axis=0)
        vtbuf[hp] = v.astype(F32).T.astype(BF16)
    lane = lax.broadcasted_iota(jnp.int32, (1, LANES), 1)
    even = lane < HEAD_DIM
    contract_last = (((1,), (1,)), ((), ()))
    first_step = i == 0
    last_step = i == pl.num_programs(2) - 1

    def scores(sb, hp):
        r0 = sb * DIL_Q
        ls = slice(hp * LANES, (hp + 1) * LANES)
        if sb == 0:
            variant = jnp.where(first_step, 1, 0)
        elif sb == n_sub - 1:
            variant = jnp.where(last_step, 2, 0)
        else:
            variant = 0
        q = q_ref[r0:r0 + DIL_Q, ls]
        zero = jnp.zeros_like(q)
        qcat = jnp.concatenate([jnp.where(even, q, zero), jnp.where(even, zero, q)], axis=0)
        s = lax.dot_general(kbuf[r0:r0 + DIL_K, ls], qcat, contract_last, preferred_element_type=F32)
        return s + bias_ref[variant, hp]

    def finish(sb, hp, s):
        r0 = sb * DIL_Q
        ls = slice(hp * LANES, (hp + 1) * LANES)
        m = jnp.max(s, axis=0, keepdims=True)
        p = jnp.exp2(s - m)
        l = jnp.sum(p, axis=0, keepdims=True)
        ot = jnp.dot(vtbuf[hp, :, r0:r0 + DIL_K], p.astype(BF16), preferred_element_type=F32) / l
        ot = jnp.concatenate([ot[:HEAD_DIM, :DIL_Q], ot[HEAD_DIM:, DIL_Q:]], axis=0)
        o_ref[r0:r0 + DIL_Q, ls] = ot.T.astype(o_ref.dtype)
        lse = (m + jnp.log2(l)) * (1.0 / LOG2E)
        lse_rows.extend([lse[:, :DIL_Q], lse[:, DIL_Q:]])
        if hp == n_pair - 1:
            tile = jnp.concatenate([jnp.broadcast_to(row, (LSE_REP, DIL_Q)) for row in lse_rows], axis=0)
            lse_ref[r0:r0 + DIL_Q, :] = tile.T
            lse_rows.clear()

    lse_rows = []
    bodies = [(sb, hp) for sb in range(n_sub) for hp in range(n_pair)]
    staged = [scores(*b) for b in bodies[:DIL_AHEAD]]
    for n, b in enumerate(bodies):
        if n + DIL_AHEAD < len(bodies):
            staged.append(scores(*bodies[n + DIL_AHEAD]))
        finish(*b, staged[n])


def _dilated(q, k, v, bias, dil):
    B, L, _ = q.shape
    W = WIDTH_A
    lse_spec = pl.BlockSpec((None, min(DIL_TL, L), LANES), lambda b, r, i: (b, i, r))
    tl = min(DIL_TL, L)
    nhalo = L // HALF_WIN
    cur = pl.BlockSpec((None, tl, W), lambda b, r, i: (b, i, r))
    prev = pl.BlockSpec((None, HALF_WIN, W),
                        lambda b, r, i: (b, jnp.maximum(i * (tl // HALF_WIN) - 1, 0), r))
    nxt = pl.BlockSpec((None, HALF_WIN, W),
                       lambda b, r, i: (b, jnp.minimum((i + 1) * (tl // HALF_WIN), nhalo - 1), r))
    bias_spec = pl.BlockSpec(bias.shape, lambda b, r, i: (0,) * bias.ndim)
    return pl.pallas_call(
        _dil_kernel,
        grid=(B, dil, L // tl),
        in_specs=[cur, prev, cur, nxt, prev, cur, nxt, bias_spec],
        out_specs=[cur, lse_spec],
        out_shape=[jax.ShapeDtypeStruct((B, L, dil * W), BF16), jax.ShapeDtypeStruct((B, L, dil * LANES), F32)],
        scratch_shapes=[pltpu.VMEM((tl + 2 * HALF_WIN, W), BF16),
                        pltpu.VMEM((N_HEADS_A // 2, LANES, tl + 2 * HALF_WIN), BF16)],
        compiler_params=_params("arbitrary", "arbitrary", "arbitrary"),
        name=f"dilated_{dil}",
    )(q, k, k, k, v, v, v, bias)


def _diff_kernel(tbl_ref, q_ref, qn_ref, k1_ref, k2_ref, vt_ref, band_ref, bmax_ref, lq1_ref, lk1_ref, lq2_ref,
                 lk2_ref, g_ref, o_ref, m1, l1, a1, m2, l2, a2, sa, sb, mxa, mxb, *, lambda_init):
    h = pl.program_id(1)
    i = pl.program_id(2)
    T = q_ref.shape[0]
    nk = k1_ref.shape[0] // T
    streams = ((k1_ref, m1, l1, a1), (k2_ref, m2, l2, a2))
    for _, m, l, a in streams:
        m[...] = jnp.full(m.shape, NEG, F32)
        l[...] = jnp.zeros(l.shape, F32)
        a[...] = jnp.zeros(a.shape, F32)
    q = q_ref[...]
    contract_last = (((1,), (1,)), ((), ()))

    def scores(j, sbuf, mxbuf, st, wraps=False):
        if wraps:
            wrap = j == nk
            j = jnp.where(wrap, 0, j)
            qq = jnp.where(wrap, qn_ref[...], q)
        else:
            qq = q
        r0 = pl.multiple_of(j * T, T)
        s = lax.dot_general(streams[st][0][pl.ds(r0, T), :], qq, contract_last, preferred_element_type=F32)
        sbuf[st] = s
        mxbuf[st] = jnp.max(s, axis=0, keepdims=True)

    def softmax_pv(j, sbuf, mxbuf, st, near, shift):
        r0 = pl.multiple_of(j * T, T)
        _, m, l, a = streams[st]
        s = sbuf[st]
        if near:
            t = jnp.clip(j - i, -(DIFF_NB + 1), DIFF_NB + 1) + (DIFF_NB + 1)
            s = s + band_ref[t]
            bound = mxbuf[st] + bmax_ref[t]
        else:
            bound = mxbuf[st] + shift
        m_prev = m[...]
        m_new = jnp.maximum(m_prev, bound)
        alpha = jnp.exp2(m_prev - m_new)
        p = jnp.exp2(s - (m_new - shift))
        l[...] = alpha * l[...] + jnp.sum(p, axis=0, keepdims=True)
        a[...] = alpha * a[...] + jnp.dot(vt_ref[:, pl.ds(r0, T)], p.astype(BF16), preferred_element_type=F32)
        m[...] = m_new

    def group(near, shift):
        def body(g, carry):
            j0 = DIFF_GROUP * g
            bufs = ((sa, mxa), (sb, mxb))
            for u in range(DIFF_GROUP):
                cur, nxt = bufs[u % 2], bufs[(u + 1) % 2]
                for st in range(2):
                    scores(j0 + u + 1, *nxt, st, wraps=u == DIFF_GROUP - 1)
                    softmax_pv(j0 + u, *cur, st, near, shift)
            return carry
        return body

    @pl.when(i == 0)
    def _():
        scores(0, sa, mxa, 0)
        scores(0, sa, mxa, 1)

    lo = jnp.maximum((i - DIFF_NB) // DIFF_GROUP, 0)
    hi = jnp.minimum((i + DIFF_NB) // DIFF_GROUP + 1, nk // DIFF_GROUP)
    c_neg = tbl_ref[NUM_BUCKETS // 2 - 1, N_HEADS_A + h] * LOG2E
    c_pos = tbl_ref[NUM_BUCKETS - 1, N_HEADS_A + h] * LOG2E
    lax.fori_loop(0, lo, group(False, c_neg), 0)
    lax.fori_loop(lo, hi, group(True, 0.0), 0)
    lax.fori_loop(hi, nk // DIFF_GROUP, group(False, c_pos), 0)

    lam =(jnp.exp(jnp.sum(lq1_ref[...] * lk1_ref[...], axis=-1, keepdims=True))
           - jnp.exp(jnp.sum(lq2_ref[...] * lk2_ref[...], axis=-1, keepdims=True)) + lambda_init)
    o = a1[...] / l1[...] - lam * (a2[...] / l2[...])
    ms = jnp.mean(o * o, axis=0, keepdims=True)
    o = o * lax.rsqrt(ms + EPS) * (g_ref[...] * (1.0 - lambda_init))
    o_ref[...] = o.T.astype(o_ref.dtype)


def _diff_attention(rel_bias, qb, k1, k2, vbt, band, bmax, lq1, lk1, lq2, lk2, g_col, lambda_init):
    B, S, W = qb.shape
    T = DIFF_T
    qspec = pl.BlockSpec((None, T, LANES), lambda b, h, i: (b, i, h))
    qnext = pl.BlockSpec((None, T, LANES), lambda b, h, i: (b, jnp.minimum(i + 1, S // T - 1), h))
    kspec = pl.BlockSpec((None, S, LANES), lambda b, h, i: (b, 0, h))
    vspec = pl.BlockSpec((None, LANES, S), lambda b, h, i: (b, h, 0))
    band_spec = pl.BlockSpec((None, DIFF_NT, T, T), lambda b, h, i: (h, 0, 0, 0))
    bmax_spec = pl.BlockSpec((None, DIFF_NT, 1, T), lambda b, h, i: (h, 0, 0, 0))
    vec = lambda n: pl.BlockSpec((1, n), lambda b, h, i: (0, 0))
    stat = pltpu.VMEM((1, T), F32)
    acc = pltpu.VMEM((LANES, T), F32)
    return pl.pallas_call(
        functools.partial(_diff_kernel, lambda_init=lambda_init),
        grid=(B, N_HEADS_B, S // T),
        in_specs=[pl.BlockSpec(memory_space=pltpu.SMEM), qspec, qnext, kspec, kspec, vspec, band_spec, bmax_spec,
                  vec(HEAD_DIM), vec(HEAD_DIM), vec(HEAD_DIM), vec(HEAD_DIM),
                  pl.BlockSpec((LANES, 1), lambda b, h, i: (0, 0))],
        out_specs=qspec,
        out_shape=jax.ShapeDtypeStruct((B, S, W), BF16),
        scratch_shapes=[stat, stat, acc, stat, stat, acc, pltpu.VMEM((2, T, T), F32), pltpu.VMEM((2, T, T), F32),
                        pltpu.VMEM((2, 1, T), F32), pltpu.VMEM((2, 1, T), F32)],
        compiler_params=_params("arbitrary", "arbitrary", "arbitrary"),
        name="diff_attn",
    )(rel_bias, qb, qb, k1, k2, vbt, band, bmax, lq1, lk1, lq2, lk2, g_col)


def _out_proj_kernel(o1_ref, o4_ref, o16_ref, s1_ref, s4_ref, s16_ref, ob_ref, x_ref, gate_ref, shift_ref,
                     scale_ref, g_ref, wa_ref, wb_ref, h_ref, hn_ref, nat, nat_lse, oa_buf):
    tm = x_ref.shape[0]
    for n, (dil, o_ref, s_ref) in enumerate(((4, o4_ref, s4_ref), (16, o16_ref, s16_ref))):
        rows = tm // dil
        for r in range(dil):
            nat_lse[n, pl.ds(r, rows, stride=dil), :] = s_ref[:, r * LANES:(r + 1) * LANES]
            for s in range(WIDTH_A // LANES):
                lo = r * WIDTH_A + s * LANES
                nat[n, s, pl.ds(r, rows, stride=dil), :] = o_ref[:, lo:lo + LANES].astype(F32)
    s1, s2, s3 = s1_ref[...], nat_lse[0], nat_lse[1]
    mx = jnp.maximum(jnp.maximum(s1, s2), s3)
    e1, e2, e3 = jnp.exp(s1 - mx), jnp.exp(s2 - mx), jnp.exp(s3 - mx)
    inv = 1.0 / (e1 + e2 + e3)
    weights = (e1 * inv, e2 * inv, e3 * inv)
    lane = lax.broadcasted_iota(jnp.int32, (1, LANES), 1)
    even = lane < HEAD_DIM
    for s in range(WIDTH_A // LANES):
        sl = slice(s * LANES, (s + 1) * LANES)

        def expand(w):
            lo = 2 * s * LSE_REP
            return jnp.where(even, w[:, lo:lo + 1], w[:, lo + LSE_REP:lo + LSE_REP + 1])

        oa = (expand(weights[0]) * o1_ref[:, sl].astype(F32) + expand(weights[1]) * nat[0, s]
              + expand(weights[2]) * nat[1, s])
        oa_buf[:, sl] = oa.astype(BF16)
    mixed = (jnp.dot(oa_buf[...], wa_ref[...], preferred_element_type=F32)
             + jnp.dot(ob_ref[...], wb_ref[...], preferred_element_type=F32))
    h = x_ref[...] + gate_ref[...] * mixed
    h_ref[...] = h
    ms = jnp.mean(h * h, axis=-1, keepdims=True)
    hn = h * lax.rsqrt(ms + EPS) * g_ref[...]
    hn_ref[...] = (hn * (1.0 + scale_ref[...]) + shift_ref[...]).astype(hn_ref.dtype)


def _out_proj(oas, lses, ob, x, gate, shift, scale, g, wa, wb):
    B, S, D = x.shape
    tm = OUT_TM
    half = pl.BlockSpec((None, tm, 512), lambda b, i: (b, i, 0))
    row = pl.BlockSpec((None, tm, D), lambda b, i: (b, i, 0))
    mod = pl.BlockSpec((None, 1, D), lambda b, i: (b, 0, 0))
    const = lambda shape: pl.BlockSpec(shape, lambda b, i: (0,) * len(shape))
    res = lambda dil, w: pl.BlockSpec((None, tm // dil, dil * w), lambda b, i: (b, i, 0))
    outs = [res(1, WIDTH_A), res(4, WIDTH_A), res(16, WIDTH_A)]
    lse_specs = [res(1, LANES), res(4, LANES), res(16, LANES)]
    return pl.pallas_call(
        _out_proj_kernel,
        grid=(B, S // tm),
        in_specs=outs + lse_specs + [half, row, mod, mod, mod, const((1, D)), const(wa.shape), const(wb.shape)],
        out_specs=[row, row],
        out_shape=[jax.ShapeDtypeStruct((B, S, D), F32), jax.ShapeDtypeStruct((B, S, D), BF16)],
        scratch_shapes=[pltpu.VMEM((2, WIDTH_A // LANES, tm, LANES), F32), pltpu.VMEM((2, tm, LANES), F32),
                        pltpu.VMEM((tm, WIDTH_A), BF16)],
        compiler_params=_params("arbitrary", "arbitrary"),
        name="out_proj",
    )(*oas, *lses, ob, x, gate, shift, scale, g, wa, wb)


def _ffn_kernel(hp_ref, hc_ref, hn_ref, h_ref, gate_ref, wu_ref, cw_ref, cb_ref, wd_ref, o_ref, lhs, act):
    i = pl.program_id(1)
    tm = hc_ref.shape[0]
    n = tm + 2 * FFN_HALO
    lhs[0:FFN_HALO, :] = jnp.where(i > 0, hp_ref[...], jnp.zeros_like(hp_ref))
    lhs[FFN_HALO:FFN_HALO + tm, :] = hc_ref[...]
    lhs[FFN_HALO + tm:, :] = jnp.where(i < pl.num_programs(1) - 1, hn_ref[...], jnp.zeros_like(hn_ref))
    x = lhs[...]

    def conv(u, lo):
        cw = cw_ref[:, lo:lo + FFN_FC]
        mid = slice(FFN_HALO, FFN_HALO + tm)
        below = pltpu.roll(u, 1, 0)[mid]
        above = pltpu.roll(u, n - 1, 0)[mid]
        return cw[0:1] * below + cw[1:2] * u[mid] + cw[2:3] * above + cb_ref[:, lo:lo + FFN_FC]

    for c in range(D_FF // FFN_FC):
        lo_v = c * FFN_FC
        lo_g = D_FF + c * FFN_FC
        val = conv(jnp.dot(x, wu_ref[:, lo_v:lo_v + FFN_FC], preferred_element_type=F32), lo_v)
        gt = conv(jnp.dot(x, wu_ref[:, lo_g:lo_g + FFN_FC], preferred_element_type=F32), lo_g)
        act[:, lo_v:lo_v + FFN_FC] = (gt / (1.0 + jnp.exp(-gt)) * val).astype(BF16)
    down = jnp.dot(act[...], wd_ref[...], preferred_element_type=F32)
    o_ref[...] = h_ref[...] + gate_ref[...] * down


def _ffn(hn, h, gate, wu, cw, cb, wd):
    B, S, D = h.shape
    tm = FFN_TM
    per = tm // FFN_HALO
    nh = S // FFN_HALO
    row = lambda dt: pl.BlockSpec((None, tm, D), lambda b, i: (b, i, 0))
    prev = pl.BlockSpec((None, FFN_HALO, D), lambda b, i: (b, jnp.maximum(i * per - 1, 0), 0))
    nxt = pl.BlockSpec((None, FFN_HALO, D), lambda b, i: (b, jnp.minimum((i + 1) * per, nh - 1), 0))
    mod = pl.BlockSpec((None, 1, D), lambda b, i: (b, 0, 0))
    const = lambda shape: pl.BlockSpec(shape, lambda b, i: (0,) * len(shape), pipeline_mode=pl.Buffered(1))
    return pl.pallas_call(
        _ffn_kernel,
        grid=(B, S // tm),
        in_specs=[prev, row(BF16), nxt, row(F32), mod, const(wu.shape), const(cw.shape), const(cb.shape),
                  const(wd.shape)],
        out_specs=row(F32),
        out_shape=jax.ShapeDtypeStruct((B, S, D), F32),
        scratch_shapes=[pltpu.VMEM((tm + 2 * FFN_HALO, D), BF16), pltpu.VMEM((tm, D_FF), BF16)],
        compiler_params=_params("arbitrary", "arbitrary"),
        name="conv_ffn",
    )(hn, hn, hn, h, gate, wu, cw, cb, wd)


def _block_diag_ones():
    r = jnp.arange(MXU_DIM) // HEAD_DIM
    return (r[:, None] == r[None, :]).astype(BF16)


def kernel(x, c, w_ada, b_ada, norm1_g, w_in, q_norm_a, k_norm_a, q_norm_b, k_norm_b, rel_bias, lambda_q1,
           lambda_k1, lambda_q2, lambda_k2, subln_g, w_out, norm2_g, w_up, conv_w, conv_b, w_down):
    B, S, D = x.shape
    depth = w_ada.shape[0]
    h = x.astype(F32)
    c8 = jnp.pad(c.astype(F32), ((0, 8 - B), (0, 0)))
    bd = _block_diag_ones()
    rel_bias = rel_bias.astype(F32)
    band_b, bmax_b = _diff_bias(rel_bias)
    band_a = _dil_bias(rel_bias)
    qscale = HEAD_DIM ** -0.5
    tile8 = lambda g: jnp.tile(g.astype(F32), WIDTH_A // HEAD_DIM)[None, :]
    row = lambda v: v.astype(F32)[None, :]

    for layer in range(depth):
        lambda_init = 0.8 - 0.6 * math.exp(-0.3 * layer)
        mod = _ada(c8, w_ada[layer].astype(F32), row(b_ada[layer]))[:B]
        shift1, scale1, gate1, shift2, scale2, gate2 = [m[:, None, :] for m in jnp.split(mod, 6, axis=-1)]

        qa, ka, va, qa4, ka4, va4, qa16, ka16, va16, qb, k1, k2, vbt = _in_proj(
            h, shift1, scale1, row(norm1_g[layer]), w_in[layer].astype(BF16), bd,
            tile8(q_norm_a[layer]) * (qscale * LOG2E), tile8(k_norm_a[layer]),
            tile8(q_norm_b[layer]) * (qscale * LOG2E), tile8(k_norm_b[layer]))

        oas, lses = [], []
        qkv = {1: (qa, ka, va), 4: (qa4, ka4, va4), 16: (qa16, ka16, va16)}
        for p, (_, dil) in enumerate(DILATED_PATTERNS):
            o, lse = _dilated(*qkv[dil], band_a[p], dil)
            oas.append(o)
            lses.append(lse)

        ob = _diff_attention(rel_bias, qb, k1, k2, vbt, band_b, bmax_b, row(lambda_q1[layer]), row(lambda_k1[layer]),
                             row(lambda_q2[layer]), row(lambda_k2[layer]),
                             subln_g[layer].astype(F32)[:, None], lambda_init)

        w_o = w_out[layer].astype(BF16)
        h, hn = _out_proj(oas, lses, ob, h, gate1, shift2, scale2, row(norm2_g[layer]),
                          w_o[:WIDTH_A], w_o[WIDTH_A:])
        h = _ffn(hn, h, gate2, w_up[layer].astype(BF16), conv_w[layer].astype(F32), row(conv_b[layer]),
                 w_down[layer].astype(BF16))

    return h.astype(x.dtype)
```

```python
import functools
import math

import jax
import jax.numpy as jnp
from jax import lax
from jax.experimental import pallas as pl
from jax.experimental.pallas import tpu as pltpu

F32 = jnp.float32
BF16 = jnp.bfloat16

D_MODEL = 1024
HEAD_DIM = 64
WIDTH_A = 512
WIDTH_B = 512
N_HEADS_A = 8
N_HEADS_B = 4
DILATED_PATTERNS = ((128, 1), (512, 4), (2048, 16))
D_FF = 2816
NUM_BUCKETS = 32
MAX_DISTANCE = 2048
EPS = 1e-6
NEG = -1e30
LOG2E = math.log2(math.e)

LANES = 128
BF16_SUBLANES = 16
MXU_DIM = 256
VMEM_LIMIT = 56 * 1024 * 1024

HALF_WIN = 64
DIL_Q = 128
DIL_K = DIL_Q + 2 * HALF_WIN
DIL_TL = 512
LSE_REP = LANES // N_HEADS_A
DIL_AHEAD = 4

DIFF_TQ = 512
DIFF_TK = 512
DIFF_RATIO = DIFF_TK // DIFF_TQ
BUCKET_SAT = 1024
DIFF_E_LO = -(BUCKET_SAT + DIFF_TK) // DIFF_TQ + 1
DIFF_E_HI = (BUCKET_SAT + DIFF_TQ) // DIFF_TQ - 1
DIFF_NT = DIFF_E_HI - DIFF_E_LO + 3
DIFF_GROUP = 8

PROJ_TM = 512
PROJ_GROUP = WIDTH_A
ADA_TN = 1024
OUT_TM = 512
FFN_TM = 512
FFN_FC = 256
FFN_HALO = BF16_SUBLANES

_BUCKET_THRESHOLDS = (1, 2, 3, 4, 5, 6, 7, 8, 16, 32, 64, 128, 256, 512, 1024)


def _params(*sem):
    return pltpu.CompilerParams(dimension_semantics=sem, vmem_limit_bytes=VMEM_LIMIT)


def _ada_kernel(c_ref, w_ref, b_ref, o_ref):
    c = c_ref[...]
    ca = c / (1.0 + jnp.exp(-c))
    o_ref[...] = jnp.dot(ca, w_ref[...], preferred_element_type=F32) + b_ref[...]


def _ada(c8, w, b):
    n = w.shape[1]
    tn = ADA_TN
    return pl.pallas_call(
        _ada_kernel,
        grid=(n // tn,),
        in_specs=[pl.BlockSpec((8, D_MODEL), lambda j: (0, 0)),
                  pl.BlockSpec((D_MODEL, tn), lambda j: (0, j)),
                  pl.BlockSpec((1, tn), lambda j: (0, j))],
        out_specs=pl.BlockSpec((8, tn), lambda j: (0, j)),
        out_shape=jax.ShapeDtypeStruct((8, n), F32),
        compiler_params=_params("arbitrary"),
        name="ada",
    )(c8, w, b)


def _head_norm(p, bd, gain):
    ss = jnp.dot((p * p).astype(BF16), bd, preferred_element_type=F32)
    return p * lax.rsqrt(ss * (1.0 / HEAD_DIM) + EPS) * gain


def _in_proj_kernel(x_ref, shift_ref, scale_ref, g_ref, w_ref, bd_ref, gqa_ref, gka_ref, gqb_ref, gkb_ref,
                    qa_ref, ka_ref, va_ref, qa4_ref, ka4_ref, va4_ref, qa16_ref, ka16_ref, va16_ref,
                    qb_ref, k1_ref, k2_ref, vbt_ref, pbuf, stage, stage4):
    tm = x_ref.shape[0]
    x = x_ref[...]
    ms = jnp.mean(x * x, axis=-1, keepdims=True)
    hn = x * lax.rsqrt(ms + EPS) * g_ref[...]
    hn = (hn * (1.0 + scale_ref[...]) + shift_ref[...]).astype(BF16)
    pbuf[...] = jnp.dot(hn, w_ref[...], preferred_element_type=F32)
    bd = bd_ref[...]
    lane = lax.broadcasted_iota(jnp.int32, (1, MXU_DIM), 1)
    first = (lane // HEAD_DIM) % 2 == 0

    def proj(g, c):
        lo = g * PROJ_GROUP + c * MXU_DIM
        return pbuf[:, lo:lo + MXU_DIM]

    for c in range(2):
        sl = slice(c * MXU_DIM, (c + 1) * MXU_DIM)
        group_a = (_head_norm(proj(0, c), bd, gqa_ref[:, sl]), _head_norm(proj(1, c), bd, gka_ref[:, sl]),
                   proj(2, c))
        for a, (val, ref) in enumerate(zip(group_a, (qa_ref, ka_ref, va_ref))):
            ref[:, sl] = val.astype(BF16)
            stage[a, 2 * c] = val[:, :LANES]
            stage[a, 2 * c + 1] = val[:, LANES:]
        qb_ref[:, sl] = _head_norm(proj(3, c), bd, gqb_ref[:, sl]).astype(BF16)
        kb = _head_norm(proj(4, c), bd, gkb_ref[:, sl])
        k1_ref[:, sl] = jnp.where(first, kb, 0.0).astype(BF16)
        k2_ref[:, sl] = jnp.where(first, 0.0, kb).astype(BF16)
        vbt_ref[sl, :] = proj(5, c).T.astype(BF16)

    for a, (ref4, ref16) in enumerate(((qa4_ref, qa16_ref), (ka4_ref, ka16_ref), (va4_ref, va16_ref))):
        for lo_r in range(4):
            for s in range(WIDTH_A // LANES):
                part = stage[a, s, pl.ds(lo_r, tm // 4, stride=4), :]
                stage4[a, lo_r, s] = part
                ref4[:, lo_r * WIDTH_A + s * LANES:lo_r * WIDTH_A + (s + 1) * LANES] = part.astype(BF16)
        for hi_r in range(4):
            for lo_r in range(4):
                for s in range(WIDTH_A // LANES):
                    lo = (4 * hi_r + lo_r) * WIDTH_A + s * LANES
                    part = stage4[a, lo_r, s, pl.ds(hi_r, tm // 16, stride=4), :]
                    ref16[:, lo:lo + LANES] = part.astype(BF16)


def _in_proj(x, shift, scale, g, w_bf, bd, gqa, gka, gqb, gkb):
    B, S, D = x.shape
    tm = PROJ_TM
    row = pl.BlockSpec((None, tm, D), lambda b, i: (b, i, 0))
    mod = pl.BlockSpec((None, 1, D), lambda b, i: (b, 0, 0))
    const = lambda shape: pl.BlockSpec(shape, lambda b, i: (0,) * len(shape))
    out = pl.BlockSpec((None, tm, PROJ_GROUP), lambda b, i: (b, i, 0))
    out_t = pl.BlockSpec((None, PROJ_GROUP, tm), lambda b, i: (b, 0, i))
    res = lambda dil: pl.BlockSpec((None, tm // dil, dil * WIDTH_A), lambda b, i: (b, i, 0))
    res_shape = lambda dil: jax.ShapeDtypeStruct((B, S // dil, dil * WIDTH_A), BF16)
    nat_shape = jax.ShapeDtypeStruct((B, S, PROJ_GROUP), BF16)
    gain = const((1, PROJ_GROUP))
    return pl.pallas_call(
        _in_proj_kernel,
        grid=(B, S // tm),
        in_specs=[row, mod, mod, const((1, D)), const(w_bf.shape), const(bd.shape),
                  gain, gain, gain, gain],
        out_specs=[out] * 3 + [res(4)] * 3 + [res(16)] * 3 + [out] * 3 + [out_t],
        out_shape=[nat_shape] * 3 + [res_shape(4)] * 3 + [res_shape(16)] * 3 + [nat_shape] * 3
                  + [jax.ShapeDtypeStruct((B, PROJ_GROUP, S), BF16)],
        scratch_shapes=[pltpu.VMEM((tm, w_bf.shape[1]), F32),
                        pltpu.VMEM((3, WIDTH_A // LANES, tm, LANES), F32),
                        pltpu.VMEM((3, 4, WIDTH_A // LANES, tm // 4, LANES), F32)],
        compiler_params=_params("arbitrary", "arbitrary"),
        name="in_proj",
    )(x, shift, scale, g, w_bf, bd, gqa, gka, gqb, gkb)


def _bias_from_rel(rel, tbl_ref, h):
    n = jnp.abs(rel)
    vneg = jnp.full(rel.shape, tbl_ref[0, h], F32)
    vpos = jnp.full(rel.shape, tbl_ref[NUM_BUCKETS // 2, h], F32)
    for i, th in enumerate(_BUCKET_THRESHOLDS):
        ge = n >= th
        vneg = jnp.where(ge, tbl_ref[i + 1, h], vneg)
        vpos = jnp.where(ge, tbl_ref[NUM_BUCKETS // 2 + i + 1, h], vpos)
    return jnp.where(rel > 0, vpos, vneg)


def _diff_bias_kernel(tbl_ref, o_ref, omax_ref, sub):
    h = pl.program_id(0)
    nk, nq = DIFF_TK // LANES, DIFF_TQ // LANES
    base = (DIFF_E_LO - 1) * nq
    lo_u = base - (nq - 1)
    n_sub = (DIFF_NT - 1) * nq + nk + nq - 1
    k = lax.broadcasted_iota(jnp.int32, (LANES, LANES), 0)
    q = lax.broadcasted_iota(jnp.int32, (LANES, LANES), 1)
    def one_offset(n, carry):
        sub[n] = _bias_from_rel(k - q + (lo_u + n) * LANES, tbl_ref, N_HEADS_A + h) * LOG2E
        return carry

    lax.fori_loop(0, n_sub, one_offset, 0)
    for t in range(DIFF_NT):
        for b in range(nq):
            cols = slice(b * LANES, (b + 1) * LANES)
            for a in range(nk):
                o_ref[t, a * LANES:(a + 1) * LANES, cols] = sub[base + t * nq + a - b - lo_u]
            omax_ref[t, :, cols] = jnp.max(o_ref[t, :, cols], axis=0, keepdims=True)


def _diff_bias(rel_bias):
    n_sub = (DIFF_NT - 1) * (DIFF_TQ // LANES) + DIFF_TK // LANES + DIFF_TQ // LANES - 1
    return pl.pallas_call(
        _diff_bias_kernel,
        grid=(N_HEADS_B,),
        in_specs=[pl.BlockSpec(memory_space=pltpu.SMEM)],
        out_specs=[pl.BlockSpec((None, DIFF_NT, DIFF_TK, DIFF_TQ), lambda h: (h, 0, 0, 0)),
                   pl.BlockSpec((None, DIFF_NT, 1, DIFF_TQ), lambda h: (h, 0, 0, 0))],
        out_shape=[jax.ShapeDtypeStruct((N_HEADS_B, DIFF_NT, DIFF_TK, DIFF_TQ), F32),
                   jax.ShapeDtypeStruct((N_HEADS_B, DIFF_NT, 1, DIFF_TQ), F32)],
        scratch_shapes=[pltpu.VMEM((n_sub, LANES, LANES), F32)],
        compiler_params=_params("arbitrary"),
        name="diff_bias",
    )(rel_bias)


def _dil_bias_kernel(tbl_ref, o_ref):
    p = pl.program_id(0)
    hp = pl.program_id(1)
    shape = (DIL_K, DIL_Q)
    k = lax.broadcasted_iota(jnp.int32, shape, 0)
    q = lax.broadcasted_iota(jnp.int32, shape, 1)
    steps = k - HALF_WIN - q
    dil = lax.shift_left(jnp.int32(1), 2 * p)
    inside = jnp.abs(steps) <= HALF_WIN
    for par in range(2):
        bias = jnp.where(inside, _bias_from_rel(steps * dil, tbl_ref, 2 * hp + par) * LOG2E, NEG)
        ls = slice(par * DIL_Q, (par + 1) * DIL_Q)
        o_ref[0, :, ls] = bias
        o_ref[1, :, ls] = jnp.where(k >= HALF_WIN, bias, NEG)
        o_ref[2, :, ls] = jnp.where(k < DIL_K - HALF_WIN, bias, NEG)


def _dil_bias(rel_bias):
    n_pat = len(DILATED_PATTERNS)
    n_pair = N_HEADS_A // 2
    return pl.pallas_call(
        _dil_bias_kernel,
        grid=(n_pat, n_pair),
        in_specs=[pl.BlockSpec(memory_space=pltpu.SMEM)],
        out_specs=pl.BlockSpec((None, 3, None, DIL_K, 2 * DIL_Q), lambda p, hp: (p, 0, hp, 0, 0)),
        out_shape=jax.ShapeDtypeStruct((n_pat, 3, n_pair, DIL_K, 2 * DIL_Q), F32),
        compiler_params=_params("arbitrary", "arbitrary"),
        name="dil_bias",
    )(rel_bias)


def _dil_kernel(q_ref, kp_ref, kc_ref, kn_ref, vp_ref, vc_ref, vn_ref, bias_ref, o_ref, lse_ref, kbuf, vtbuf):
    tl = q_ref.shape[0]
    i = pl.program_id(2)
    n_sub = tl // DIL_Q
    n_pair = N_HEADS_A // 2
    kbuf[0:HALF_WIN, :] = kp_ref[...]
    kbuf[HALF_WIN:HALF_WIN + tl, :] = kc_ref[...]
    kbuf[HALF_WIN + tl:, :] = kn_ref[...]
    for hp in range(n_pair):
        ls = slice(hp * LANES, (hp + 1) * LANES)
        v = jnp.concatenate([vp_ref[:, ls], vc_ref[:, ls], vn_ref[:, ls]], axis=0)
        vtbuf[hp] = v.astype(F32).T.astype(BF16)
    lane = lax.broadcasted_iota(jnp.int32, (1, LANES), 1)
    even = lane < HEAD_DIM
    contract_last = (((1,), (1,)), ((), ()))
    first_step = i == 0
    last_step = i == pl.num_programs(2) - 1

    def scores(sb, hp):
        r0 = sb * DIL_Q
        ls = slice(hp * LANES, (hp + 1) * LANES)
        if sb == 0:
            variant = jnp.where(first_step, 1, 0)
        elif sb == n_sub - 1:
            variant = jnp.where(last_step, 2, 0)
        else:
            variant = 0
        q = q_ref[r0:r0 + DIL_Q, ls]
        zero = jnp.zeros_like(q)
        qcat = jnp.concatenate([jnp.where(even, q, zero), jnp.where(even, zero, q)], axis=0)
        s = lax.dot_general(kbuf[r0:r0 + DIL_K, ls], qcat, contract_last, preferred_element_type=F32)
        return s + bias_ref[variant, hp]

    def finish(sb, hp, s):
        r0 = sb * DIL_Q
        ls = slice(hp * LANES, (hp + 1) * LANES)
        m = jnp.max(s, axis=0, keepdims=True)
        p = jnp.exp2(s - m)
        l = jnp.sum(p, axis=0, keepdims=True)
        ot = jnp.dot(vtbuf[hp, :, r0:r0 + DIL_K], p.astype(BF16), preferred_element_type=F32) / l
        ot = jnp.concatenate([ot[:HEAD_DIM, :DIL_Q], ot[HEAD_DIM:, DIL_Q:]], axis=0)
        o_ref[r0:r0 + DIL_Q, ls] = ot.T.astype(o_ref.dtype)
        lse = (m + jnp.log2(l)) * (1.0 / LOG2E)
        lse_rows.extend([lse[:, :DIL_Q], lse[:, DIL_Q:]])
        if hp == n_pair - 1:
            tile = jnp.concatenate([jnp.broadcast_to(row, (LSE_REP, DIL_Q)) for row in lse_rows], axis=0)
            lse_ref[r0:r0 + DIL_Q, :] = tile.T
            lse_rows.clear()

    lse_rows = []
    bodies = [(sb, hp) for sb in range(n_sub) for hp in range(n_pair)]
    staged = [scores(*b) for b in bodies[:DIL_AHEAD]]
    for n, b in enumerate(bodies):
        if n + DIL_AHEAD < len(bodies):
            staged.append(scores(*bodies[n + DIL_AHEAD]))
        finish(*b, staged[n])


def _dilated(q, k, v, bias, dil):
    B, L, _ = q.shape
    W = WIDTH_A
    lse_spec = pl.BlockSpec((None, min(DIL_TL, L), LANES), lambda b, r, i: (b, i, r))
    tl = min(DIL_TL, L)
    nhalo = L // HALF_WIN
    cur = pl.BlockSpec((None, tl, W), lambda b, r, i: (b, i, r))
    prev = pl.BlockSpec((None, HALF_WIN, W),
                        lambda b, r, i: (b, jnp.maximum(i * (tl // HALF_WIN) - 1, 0), r))
    nxt = pl.BlockSpec((None, HALF_WIN, W),
                       lambda b, r, i: (b, jnp.minimum((i + 1) * (tl // HALF_WIN), nhalo - 1), r))
    bias_spec = pl.BlockSpec(bias.shape, lambda b, r, i: (0,) * bias.ndim)
    return pl.pallas_call(
        _dil_kernel,
        grid=(B, dil, L // tl),
        in_specs=[cur, prev, cur, nxt, prev, cur, nxt, bias_spec],
        out_specs=[cur, lse_spec],
        out_shape=[jax.ShapeDtypeStruct((B, L, dil * W), BF16), jax.ShapeDtypeStruct((B, L, dil * LANES), F32)],
        scratch_shapes=[pltpu.VMEM((tl + 2 * HALF_WIN, W), BF16),
                        pltpu.VMEM((N_HEADS_A // 2, LANES, tl + 2 * HALF_WIN), BF16)],
        compiler_params=_params("arbitrary", "arbitrary", "arbitrary"),
        name=f"dilated_{dil}",
    )(q, k, k, k, v, v, v, bias)


def _diff_kernel(tbl_ref, q_ref, qn_ref, k1_ref, k2_ref, vt_ref, band_ref, bmax_ref, lq1_ref, lk1_ref, lq2_ref,
                 lk2_ref, g_ref, o_ref, m1, l1, a1, m2, l2, a2, sa, sb, mxa, mxb, *, lambda_init):
    h = pl.program_id(1)
    i = pl.program_id(2)
    T = DIFF_TK
    nk = k1_ref.shape[0] // T
    streams = ((k1_ref, m1, l1, a1), (k2_ref, m2, l2, a2))
    for _, m, l, a in streams:
        m[...] = jnp.full(m.shape, NEG, F32)
        l[...] = jnp.zeros(l.shape, F32)
        a[...] = jnp.zeros(a.shape, F32)
    q = q_ref[...]
    contract_last = (((1,), (1,)), ((), ()))

    def scores(j, sbuf, mxbuf, st, wraps=False):
        if wraps:
            wrap = j == nk
            j = jnp.where(wrap, 0, j)
            qq = jnp.where(wrap, qn_ref[...], q)
        else:
            qq = q
        r0 = pl.multiple_of(j * T, T)
        s = lax.dot_general(streams[st][0][pl.ds(r0, T), :], qq, contract_last, preferred_element_type=F32)
        sbuf[st] = s
        mxbuf[st] = jnp.max(s, axis=0, keepdims=True)

    def softmax_pv(j, sbuf, mxbuf, st, near, shift):
        r0 = pl.multiple_of(j * T, T)
        _, m, l, a = streams[st]
        s = sbuf[st]
        if near:
            t = jnp.clip(DIFF_RATIO * j - i, DIFF_E_LO - 1, DIFF_E_HI + 1) - (DIFF_E_LO - 1)
            s = s + band_ref[t]
            bound = mxbuf[st] + bmax_ref[t]
        else:
            bound = mxbuf[st] + shift
        m_prev = m[...]
        m_new = jnp.maximum(m_prev, bound)
        alpha = jnp.exp2(m_prev - m_new)
        p = jnp.exp2(s - (m_new - shift))
        l[...] = alpha * l[...] + jnp.sum(p, axis=0, keepdims=True)
        a[...] = alpha * a[...] + jnp.dot(vt_ref[:, pl.ds(r0, T)], p.astype(BF16), preferred_element_type=F32)
        m[...] = m_new

    def group(near, shift):
        def body(g, carry):
            j0 = DIFF_GROUP * g
            bufs = ((sa, mxa), (sb, mxb))
            for u in range(DIFF_GROUP):
                cur, nxt = bufs[u % 2], bufs[(u + 1) % 2]
                for st in range(2):
                    scores(j0 + u + 1, *nxt, st, wraps=u == DIFF_GROUP - 1)
                    softmax_pv(j0 + u, *cur, st, near, shift)
            return carry
        return body

    @pl.when(i == 0)
    def _():
        scores(0, sa, mxa, 0)
        scores(0, sa, mxa, 1)

    j_lo = (i + DIFF_E_LO + DIFF_RATIO - 1) // DIFF_RATIO
    j_hi = (i + DIFF_E_HI) // DIFF_RATIO
    lo = jnp.maximum(j_lo // DIFF_GROUP, 0)
    hi = jnp.minimum(j_hi // DIFF_GROUP + 1, nk // DIFF_GROUP)
    c_neg = tbl_ref[NUM_BUCKETS // 2 - 1, N_HEADS_A + h] * LOG2E
    c_pos = tbl_ref[NUM_BUCKETS - 1, N_HEADS_A + h] * LOG2E
    lax.fori_loop(0, lo, group(False, c_neg), 0)
    lax.fori_loop(lo, hi, group(True, 0.0), 0)
    lax.fori_loop(hi, nk // DIFF_GROUP, group(False, c_pos), 0)

    lam =(jnp.exp(jnp.sum(lq1_ref[...] * lk1_ref[...], axis=-1, keepdims=True))
           - jnp.exp(jnp.sum(lq2_ref[...] * lk2_ref[...], axis=-1, keepdims=True)) + lambda_init)
    o = a1[...] / l1[...] - lam * (a2[...] / l2[...])
    ms = jnp.mean(o * o, axis=0, keepdims=True)
    o = o * lax.rsqrt(ms + EPS) * (g_ref[...] * (1.0 - lambda_init))
    o_ref[...] = o.T.astype(o_ref.dtype)


def _diff_attention(rel_bias, qb, k1, k2, vbt, band, bmax, lq1, lk1, lq2, lk2, g_col, lambda_init):
    B, S, W = qb.shape
    T, TK = DIFF_TQ, DIFF_TK
    qspec = pl.BlockSpec((None, T, LANES), lambda b, h, i: (b, i, h))
    qnext = pl.BlockSpec((None, T, LANES), lambda b, h, i: (b, jnp.minimum(i + 1, S // T - 1), h))
    kspec = pl.BlockSpec((None, S, LANES), lambda b, h, i: (b, 0, h))
    vspec = pl.BlockSpec((None, LANES, S), lambda b, h, i: (b, h, 0))
    band_spec = pl.BlockSpec((None, DIFF_NT, TK, T), lambda b, h, i: (h, 0, 0, 0), pipeline_mode=pl.Buffered(1))
    bmax_spec = pl.BlockSpec((None, DIFF_NT, 1, T), lambda b, h, i: (h, 0, 0, 0))
    vec = lambda n: pl.BlockSpec((1, n), lambda b, h, i: (0, 0))
    stat = pltpu.VMEM((1, T), F32)
    acc = pltpu.VMEM((LANES, T), F32)
    return pl.pallas_call(
        functools.partial(_diff_kernel, lambda_init=lambda_init),
        grid=(B, N_HEADS_B, S // T),
        in_specs=[pl.BlockSpec(memory_space=pltpu.SMEM), qspec, qnext, kspec, kspec, vspec, band_spec, bmax_spec,
                  vec(HEAD_DIM), vec(HEAD_DIM), vec(HEAD_DIM), vec(HEAD_DIM),
                  pl.BlockSpec((LANES, 1), lambda b, h, i: (0, 0))],
        out_specs=qspec,
        out_shape=jax.ShapeDtypeStruct((B, S, W), BF16),
        scratch_shapes=[stat, stat, acc, stat, stat, acc, pltpu.VMEM((2, TK, T), F32), pltpu.VMEM((2, TK, T), F32),
                        pltpu.VMEM((2, 1, T), F32), pltpu.VMEM((2, 1, T), F32)],
        compiler_params=_params("arbitrary", "arbitrary", "arbitrary"),
        name="diff_attn",
    )(rel_bias, qb, qb, k1, k2, vbt, band, bmax, lq1, lk1, lq2, lk2, g_col)


def _out_proj_kernel(o1_ref, o4_ref, o16_ref, s1_ref, s4_ref, s16_ref, ob_ref, x_ref, gate_ref, shift_ref,
                     scale_ref, g_ref, wa_ref, wb_ref, sel_ref, h_ref, hn_ref, nat, nat_lse, oa_buf):
    tm = x_ref.shape[0]
    for n, (dil, o_ref, s_ref) in enumerate(((4, o4_ref, s4_ref), (16, o16_ref, s16_ref))):
        rows = tm // dil
        for r in range(dil):
            nat_lse[n, pl.ds(r, rows, stride=dil), :] = s_ref[:, r * LANES:(r + 1) * LANES]
            for s in range(WIDTH_A // LANES):
                lo = r * WIDTH_A + s * LANES
                nat[n, s, pl.ds(r, rows, stride=dil), :] = o_ref[:, lo:lo + LANES].astype(F32)
    s1, s2, s3 = s1_ref[...], nat_lse[0], nat_lse[1]
    mx = jnp.maximum(jnp.maximum(s1, s2), s3)
    e1, e2, e3 = jnp.exp(s1 - mx), jnp.exp(s2 - mx), jnp.exp(s3 - mx)
    inv = 1.0 / (e1 + e2 + e3)
    def expand(w):
        hi = w.astype(BF16)
        lo = (w - hi.astype(F32)).astype(BF16)
        sel = sel_ref[...]
        return (jnp.dot(hi, sel, preferred_element_type=F32) + jnp.dot(lo, sel, preferred_element_type=F32))

    w1, w2, w3 = expand(e1 * inv), expand(e2 * inv), expand(e3 * inv)
    for s in range(WIDTH_A // LANES):
        sl = slice(s * LANES, (s + 1) * LANES)
        oa = w1[:, sl] * o1_ref[:, sl].astype(F32) + w2[:, sl] * nat[0, s] + w3[:, sl] * nat[1, s]
        oa_buf[:, sl] = oa.astype(BF16)
    mixed = (jnp.dot(oa_buf[...], wa_ref[...], preferred_element_type=F32)
             + jnp.dot(ob_ref[...], wb_ref[...], preferred_element_type=F32))
    h = x_ref[...] + gate_ref[...] * mixed
    h_ref[...] = h
    ms = jnp.mean(h * h, axis=-1, keepdims=True)
    hn = h * lax.rsqrt(ms + EPS) * g_ref[...]
    hn_ref[...] = (hn * (1.0 + scale_ref[...]) + shift_ref[...]).astype(hn_ref.dtype)


def _head_select():
    src = jnp.arange(LANES)[:, None]
    head = jnp.arange(WIDTH_A)[None, :] // HEAD_DIM
    return (src == head * LSE_REP).astype(BF16)


def _out_proj(oas, lses, ob, x, gate, shift, scale, g, wa, wb):
    B, S, D = x.shape
    tm = OUT_TM
    sel = _head_select()
    half = pl.BlockSpec((None, tm, WIDTH_B), lambda b, i: (b, i, 0))
    row = pl.BlockSpec((None, tm, D), lambda b, i: (b, i, 0))
    mod = pl.BlockSpec((None, 1, D), lambda b, i: (b, 0, 0))
    const = lambda shape: pl.BlockSpec(shape, lambda b, i: (0,) * len(shape))
    res = lambda dil, w: pl.BlockSpec((None, tm // dil, dil * w), lambda b, i: (b, i, 0))
    outs = [res(1, WIDTH_A), res(4, WIDTH_A), res(16, WIDTH_A)]
    lse_specs = [res(1, LANES), res(4, LANES), res(16, LANES)]
    return pl.pallas_call(
        _out_proj_kernel,
        grid=(B, S // tm),
        in_specs=outs + lse_specs + [half, row, mod, mod, mod, const((1, D)), const(wa.shape), const(wb.shape),
                                     const(sel.shape)],
        out_specs=[row, row],
        out_shape=[jax.ShapeDtypeStruct((B, S, D), F32), jax.ShapeDtypeStruct((B, S, D), BF16)],
        scratch_shapes=[pltpu.VMEM((2, WIDTH_A // LANES, tm, LANES), F32), pltpu.VMEM((2, tm, LANES), F32),
                        pltpu.VMEM((tm, WIDTH_A), BF16)],
        compiler_params=_params("arbitrary", "arbitrary"),
        name="out_proj",
    )(*oas, *lses, ob, x, gate, shift, scale, g, wa, wb, sel)


def _ffn_kernel(hp_ref, hc_ref, hn_ref, h_ref, gate_ref, wu_ref, cw_ref, cb_ref, wd_ref, o_ref, lhs, act):
    i = pl.program_id(1)
    tm = hc_ref.shape[0]
    n = tm + 2 * FFN_HALO
    lhs[0:FFN_HALO, :] = jnp.where(i > 0, hp_ref[...], jnp.zeros_like(hp_ref))
    lhs[FFN_HALO:FFN_HALO + tm, :] = hc_ref[...]
    lhs[FFN_HALO + tm:, :] = jnp.where(i < pl.num_programs(1) - 1, hn_ref[...], jnp.zeros_like(hn_ref))
    x = lhs[...]

    def conv(u, lo):
        cw = cw_ref[:, lo:lo + FFN_FC]
        mid = slice(FFN_HALO, FFN_HALO + tm)
        below = pltpu.roll(u, 1, 0)[mid]
        above = pltpu.roll(u, n - 1, 0)[mid]
        return cw[0:1] * below + cw[1:2] * u[mid] + cw[2:3] * above + cb_ref[:, lo:lo + FFN_FC]

    for c in range(D_FF // FFN_FC):
        lo_v = c * FFN_FC
        lo_g = D_FF + c * FFN_FC
        val = conv(jnp.dot(x, wu_ref[:, lo_v:lo_v + FFN_FC], preferred_element_type=F32), lo_v)
        gt = conv(jnp.dot(x, wu_ref[:, lo_g:lo_g + FFN_FC], preferred_element_type=F32), lo_g)
        act[:, lo_v:lo_v + FFN_FC] = (gt / (1.0 + jnp.exp(-gt)) * val).astype(BF16)
    down = jnp.dot(act[...], wd_ref[...], preferred_element_type=F32)
    o_ref[...] = h_ref[...] + gate_ref[...] * down


def _ffn(hn, h, gate, wu, cw, cb, wd):
    B, S, D = h.shape
    tm = FFN_TM
    per = tm // FFN_HALO
    nh = S // FFN_HALO
    row = lambda dt: pl.BlockSpec((None, tm, D), lambda b, i: (b, i, 0))
    prev = pl.BlockSpec((None, FFN_HALO, D), lambda b, i: (b, jnp.maximum(i * per - 1, 0), 0))
    nxt = pl.BlockSpec((None, FFN_HALO, D), lambda b, i: (b, jnp.minimum((i + 1) * per, nh - 1), 0))
    mod = pl.BlockSpec((None, 1, D), lambda b, i: (b, 0, 0))
    const = lambda shape: pl.BlockSpec(shape, lambda b, i: (0,) * len(shape), pipeline_mode=pl.Buffered(1))
    return pl.pallas_call(
        _ffn_kernel,
        grid=(B, S // tm),
        in_specs=[prev, row(BF16), nxt, row(F32), mod, const(wu.shape), const(cw.shape), const(cb.shape),
                  const(wd.shape)],
        out_specs=row(F32),
        out_shape=jax.ShapeDtypeStruct((B, S, D), F32),
        scratch_shapes=[pltpu.VMEM((tm + 2 * FFN_HALO, D), BF16), pltpu.VMEM((tm, D_FF), BF16)],
        compiler_params=_params("arbitrary", "arbitrary"),
        name="conv_ffn",
    )(hn, hn, hn, h, gate, wu, cw, cb, wd)


def _block_diag_ones():
    r = jnp.arange(MXU_DIM) // HEAD_DIM
    return (r[:, None] == r[None, :]).astype(BF16)


def kernel(x, c, w_ada, b_ada, norm1_g, w_in, q_norm_a, k_norm_a, q_norm_b, k_norm_b, rel_bias, lambda_q1,
           lambda_k1, lambda_q2, lambda_k2, subln_g, w_out, norm2_g, w_up, conv_w, conv_b, w_down):
    B, S, D = x.shape
    depth = w_ada.shape[0]
    h = x.astype(F32)
    c8 = jnp.pad(c.astype(F32), ((0, 8 - B), (0, 0)))
    bd = _block_diag_ones()
    rel_bias = rel_bias.astype(F32)
    band_b, bmax_b = _diff_bias(rel_bias)
    band_a = _dil_bias(rel_bias)
    qscale = HEAD_DIM ** -0.5
    tile8 = lambda g: jnp.tile(g.astype(F32), WIDTH_A // HEAD_DIM)[None, :]
    row = lambda v: v.astype(F32)[None, :]

    for layer in range(depth):
        lambda_init = 0.8 - 0.6 * math.exp(-0.3 * layer)
        mod = _ada(c8, w_ada[layer].astype(F32), row(b_ada[layer]))[:B]
        shift1, scale1, gate1, shift2, scale2, gate2 = [m[:, None, :] for m in jnp.split(mod, 6, axis=-1)]

        qa, ka, va, qa4, ka4, va4, qa16, ka16, va16, qb, k1, k2, vbt = _in_proj(
            h, shift1, scale1, row(norm1_g[layer]), w_in[layer].astype(BF16), bd,
            tile8(q_norm_a[layer]) * (qscale * LOG2E), tile8(k_norm_a[layer]),
            tile8(q_norm_b[layer]) * (qscale * LOG2E), tile8(k_norm_b[layer]))

        oas, lses = [], []
        qkv = {1: (qa, ka, va), 4: (qa4, ka4, va4), 16: (qa16, ka16, va16)}
        for p, (_, dil) in enumerate(DILATED_PATTERNS):
            o, lse = _dilated(*qkv[dil], band_a[p], dil)
            oas.append(o)
            lses.append(lse)

        ob = _diff_attention(rel_bias, qb, k1, k2, vbt, band_b, bmax_b, row(lambda_q1[layer]), row(lambda_k1[layer]),
                             row(lambda_q2[layer]), row(lambda_k2[layer]),
                             subln_g[layer].astype(F32)[:, None], lambda_init)

        w_o = w_out[layer].astype(BF16)
        h, hn = _out_proj(oas, lses, ob, h, gate1, shift2, scale2, row(norm2_g[layer]),
                          w_o[:WIDTH_A], w_o[WIDTH_A:])
        h = _ffn(hn, h, gate2, w_up[layer].astype(BF16), conv_w[layer].astype(F32), row(conv_b[layer]),
                 w_down[layer].astype(BF16))

    return h.astype(x.dtype)
```

```python
import functools
import math

import jax
import jax.numpy as jnp
from jax import lax
from jax.experimental import pallas as pl
from jax.experimental.pallas import tpu as pltpu

F32 = jnp.float32
BF16 = jnp.bfloat16

D_MODEL = 1024
HEAD_DIM = 64
WIDTH_A = 512
WIDTH_B = 512
N_HEADS_A = 8
N_HEADS_B = 4
DILATED_PATTERNS = ((128, 1), (512, 4), (2048, 16))
D_FF = 2816
NUM_BUCKETS = 32
MAX_DISTANCE = 2048
EPS = 1e-6
NEG = -1e30
LOG2E = math.log2(math.e)

LANES = 128
BF16_SUBLANES = 16
MXU_DIM = 256
VMEM_LIMIT = 56 * 1024 * 1024

HALF_WIN = 64
DIL_Q = 128
DIL_K = DIL_Q + 2 * HALF_WIN
DIL_TL = 1024
LSE_REP = LANES // N_HEADS_A
DIL_AHEAD = 4

DIFF_TQ = 512
DIFF_TK = 512
DIFF_RATIO = DIFF_TK // DIFF_TQ
BUCKET_SAT = 1024
DIFF_E_LO = -(BUCKET_SAT + DIFF_TK) // DIFF_TQ + 1
DIFF_E_HI = (BUCKET_SAT + DIFF_TQ) // DIFF_TQ - 1
DIFF_NT = DIFF_E_HI - DIFF_E_LO + 3
DIFF_GROUP = 8

PROJ_TM = 512
PROJ_GROUP = WIDTH_A
ADA_TN = 1024
OUT_TM = 512
FFN_TM = 512
FFN_FC = 256
FFN_HALO = BF16_SUBLANES

_BUCKET_THRESHOLDS = (1, 2, 3, 4, 5, 6, 7, 8, 16, 32, 64, 128, 256, 512, 1024)


def _params(*sem):
    return pltpu.CompilerParams(dimension_semantics=sem, vmem_limit_bytes=VMEM_LIMIT)


def _ada_kernel(c_ref, w_ref, b_ref, o_ref):
    c = c_ref[...]
    ca = c / (1.0 + jnp.exp(-c))
    o_ref[...] = jnp.dot(ca, w_ref[...], preferred_element_type=F32) + b_ref[...]


def _ada(c8, w, b):
    n = w.shape[1]
    tn = ADA_TN
    return pl.pallas_call(
        _ada_kernel,
        grid=(n // tn,),
        in_specs=[pl.BlockSpec((8, D_MODEL), lambda j: (0, 0)),
                  pl.BlockSpec((D_MODEL, tn), lambda j: (0, j)),
                  pl.BlockSpec((1, tn), lambda j: (0, j))],
        out_specs=pl.BlockSpec((8, tn), lambda j: (0, j)),
        out_shape=jax.ShapeDtypeStruct((8, n), F32),
        compiler_params=_params("arbitrary"),
        name="ada",
    )(c8, w, b)


def _head_norm(p, bd, gain):
    ss = jnp.dot((p * p).astype(BF16), bd, preferred_element_type=F32)
    return p * lax.rsqrt(ss * (1.0 / HEAD_DIM) + EPS) * gain


def _in_proj_kernel(x_ref, shift_ref, scale_ref, g_ref, w_ref, bd_ref, gqa_ref, gka_ref, gqb_ref, gkb_ref,
                    qa_ref, ka_ref, va_ref, qa4_ref, ka4_ref, va4_ref, qa16_ref, ka16_ref, va16_ref,
                    qb_ref, k1_ref, k2_ref, vbt_ref, pbuf, stage, stage4):
    tm = x_ref.shape[0]
    x = x_ref[...]
    ms = jnp.mean(x * x, axis=-1, keepdims=True)
    hn = x * lax.rsqrt(ms + EPS) * g_ref[...]
    hn = (hn * (1.0 + scale_ref[...]) + shift_ref[...]).astype(BF16)
    pbuf[...] = jnp.dot(hn, w_ref[...], preferred_element_type=F32)
    bd = bd_ref[...]
    lane = lax.broadcasted_iota(jnp.int32, (1, MXU_DIM), 1)
    first = (lane // HEAD_DIM) % 2 == 0

    def proj(g, c):
        lo = g * PROJ_GROUP + c * MXU_DIM
        return pbuf[:, lo:lo + MXU_DIM]

    for c in range(2):
        sl = slice(c * MXU_DIM, (c + 1) * MXU_DIM)
        group_a = (_head_norm(proj(0, c), bd, gqa_ref[:, sl]), _head_norm(proj(1, c), bd, gka_ref[:, sl]),
                   proj(2, c))
        for a, (val, ref) in enumerate(zip(group_a, (qa_ref, ka_ref, va_ref))):
            ref[:, sl] = val.astype(BF16)
            stage[a, 2 * c] = val[:, :LANES]
            stage[a, 2 * c + 1] = val[:, LANES:]
        qb_ref[:, sl] = _head_norm(proj(3, c), bd, gqb_ref[:, sl]).astype(BF16)
        kb = _head_norm(proj(4, c), bd, gkb_ref[:, sl])
        k1_ref[:, sl] = jnp.where(first, kb, 0.0).astype(BF16)
        k2_ref[:, sl] = jnp.where(first, 0.0, kb).astype(BF16)
        vbt_ref[sl, :] = proj(5, c).T.astype(BF16)

    for a, (ref4, ref16) in enumerate(((qa4_ref, qa16_ref), (ka4_ref, ka16_ref), (va4_ref, va16_ref))):
        for lo_r in range(4):
            for s in range(WIDTH_A // LANES):
                part = stage[a, s, pl.ds(lo_r, tm // 4, stride=4), :]
                stage4[a, lo_r, s] = part
                ref4[:, lo_r * WIDTH_A + s * LANES:lo_r * WIDTH_A + (s + 1) * LANES] = part.astype(BF16)
        for hi_r in range(4):
            for lo_r in range(4):
                for s in range(WIDTH_A // LANES):
                    lo = (4 * hi_r + lo_r) * WIDTH_A + s * LANES
                    part = stage4[a, lo_r, s, pl.ds(hi_r, tm // 16, stride=4), :]
                    ref16[:, lo:lo + LANES] = part.astype(BF16)


def _in_proj(x, shift, scale, g, w_bf, bd, gqa, gka, gqb, gkb):
    B, S, D = x.shape
    tm = PROJ_TM
    row = pl.BlockSpec((None, tm, D), lambda b, i: (b, i, 0))
    mod = pl.BlockSpec((None, 1, D), lambda b, i: (b, 0, 0))
    const = lambda shape: pl.BlockSpec(shape, lambda b, i: (0,) * len(shape))
    out = pl.BlockSpec((None, tm, PROJ_GROUP), lambda b, i: (b, i, 0))
    out_t = pl.BlockSpec((None, PROJ_GROUP, tm), lambda b, i: (b, 0, i))
    res = lambda dil: pl.BlockSpec((None, tm // dil, dil * WIDTH_A), lambda b, i: (b, i, 0))
    res_shape = lambda dil: jax.ShapeDtypeStruct((B, S // dil, dil * WIDTH_A), BF16)
    nat_shape = jax.ShapeDtypeStruct((B, S, PROJ_GROUP), BF16)
    gain = const((1, PROJ_GROUP))
    return pl.pallas_call(
        _in_proj_kernel,
        grid=(B, S // tm),
        in_specs=[row, mod, mod, const((1, D)), const(w_bf.shape), const(bd.shape),
                  gain, gain, gain, gain],
        out_specs=[out] * 3 + [res(4)] * 3 + [res(16)] * 3 + [out] * 3 + [out_t],
        out_shape=[nat_shape] * 3 + [res_shape(4)] * 3 + [res_shape(16)] * 3 + [nat_shape] * 3
                  + [jax.ShapeDtypeStruct((B, PROJ_GROUP, S), BF16)],
        scratch_shapes=[pltpu.VMEM((tm, w_bf.shape[1]), F32),
                        pltpu.VMEM((3, WIDTH_A // LANES, tm, LANES), F32),
                        pltpu.VMEM((3, 4, WIDTH_A // LANES, tm // 4, LANES), F32)],
        compiler_params=_params("arbitrary", "arbitrary"),
        name="in_proj",
    )(x, shift, scale, g, w_bf, bd, gqa, gka, gqb, gkb)


def _bias_from_rel(rel, tbl_ref, h):
    n = jnp.abs(rel)
    vneg = jnp.full(rel.shape, tbl_ref[0, h], F32)
    vpos = jnp.full(rel.shape, tbl_ref[NUM_BUCKETS // 2, h], F32)
    for i, th in enumerate(_BUCKET_THRESHOLDS):
        ge = n >= th
        vneg = jnp.where(ge, tbl_ref[i + 1, h], vneg)
        vpos = jnp.where(ge, tbl_ref[NUM_BUCKETS // 2 + i + 1, h], vpos)
    return jnp.where(rel > 0, vpos, vneg)


def _diff_bias_kernel(tbl_ref, o_ref, omax_ref, sub):
    h = pl.program_id(0)
    nk, nq = DIFF_TK // LANES, DIFF_TQ // LANES
    base = (DIFF_E_LO - 1) * nq
    lo_u = base - (nq - 1)
    n_sub = (DIFF_NT - 1) * nq + nk + nq - 1
    k = lax.broadcasted_iota(jnp.int32, (LANES, LANES), 0)
    q = lax.broadcasted_iota(jnp.int32, (LANES, LANES), 1)
    def one_offset(n, carry):
        sub[n] = _bias_from_rel(k - q + (lo_u + n) * LANES, tbl_ref, N_HEADS_A + h) * LOG2E
        return carry

    lax.fori_loop(0, n_sub, one_offset, 0)
    for t in range(DIFF_NT):
        for b in range(nq):
            cols = slice(b * LANES, (b + 1) * LANES)
            for a in range(nk):
                o_ref[t, a * LANES:(a + 1) * LANES, cols] = sub[base + t * nq + a - b - lo_u]
            omax_ref[t, :, cols] = jnp.max(o_ref[t, :, cols], axis=0, keepdims=True)


def _diff_bias(rel_bias):
    n_sub = (DIFF_NT - 1) * (DIFF_TQ // LANES) + DIFF_TK // LANES + DIFF_TQ // LANES - 1
    return pl.pallas_call(
        _diff_bias_kernel,
        grid=(N_HEADS_B,),
        in_specs=[pl.BlockSpec(memory_space=pltpu.SMEM)],
        out_specs=[pl.BlockSpec((None, DIFF_NT, DIFF_TK, DIFF_TQ), lambda h: (h, 0, 0, 0)),
                   pl.BlockSpec((None, DIFF_NT, 1, DIFF_TQ), lambda h: (h, 0, 0, 0))],
        out_shape=[jax.ShapeDtypeStruct((N_HEADS_B, DIFF_NT, DIFF_TK, DIFF_TQ), F32),
                   jax.ShapeDtypeStruct((N_HEADS_B, DIFF_NT, 1, DIFF_TQ), F32)],
        scratch_shapes=[pltpu.VMEM((n_sub, LANES, LANES), F32)],
        compiler_params=_params("arbitrary"),
        name="diff_bias",
    )(rel_bias)


def _dil_bias_kernel(tbl_ref, o_ref):
    p = pl.program_id(0)
    hp = pl.program_id(1)
    shape = (DIL_K, DIL_Q)
    k = lax.broadcasted_iota(jnp.int32, shape, 0)
    q = lax.broadcasted_iota(jnp.int32, shape, 1)
    steps = k - HALF_WIN - q
    dil = lax.shift_left(jnp.int32(1), 2 * p)
    inside = jnp.abs(steps) <= HALF_WIN
    for par in range(2):
        bias = jnp.where(inside, _bias_from_rel(steps * dil, tbl_ref, 2 * hp + par) * LOG2E, NEG)
        ls = slice(par * DIL_Q, (par + 1) * DIL_Q)
        o_ref[0, :, ls] = bias
        o_ref[1, :, ls] = jnp.where(k >= HALF_WIN, bias, NEG)
        o_ref[2, :, ls] = jnp.where(k < DIL_K - HALF_WIN, bias, NEG)


def _dil_bias(rel_bias):
    n_pat = len(DILATED_PATTERNS)
    n_pair = N_HEADS_A // 2
    return pl.pallas_call(
        _dil_bias_kernel,
        grid=(n_pat, n_pair),
        in_specs=[pl.BlockSpec(memory_space=pltpu.SMEM)],
        out_specs=pl.BlockSpec((None, 3, None, DIL_K, 2 * DIL_Q), lambda p, hp: (p, 0, hp, 0, 0)),
        out_shape=jax.ShapeDtypeStruct((n_pat, 3, n_pair, DIL_K, 2 * DIL_Q), F32),
        compiler_params=_params("arbitrary", "arbitrary"),
        name="dil_bias",
    )(rel_bias)


def _dil_kernel(q_ref, kp_ref, kc_ref, kn_ref, vp_ref, vc_ref, vn_ref, bias_ref, o_ref, lse_ref, kbuf, vtbuf):
    tl = q_ref.shape[0]
    i = pl.program_id(2)
    n_sub = tl // DIL_Q
    n_pair = N_HEADS_A // 2
    kbuf[0:HALF_WIN, :] = kp_ref[...]
    kbuf[HALF_WIN:HALF_WIN + tl, :] = kc_ref[...]
    kbuf[HALF_WIN + tl:, :] = kn_ref[...]
    for hp in range(n_pair):
        ls = slice(hp * LANES, (hp + 1) * LANES)
        v = jnp.concatenate([vp_ref[:, ls], vc_ref[:, ls], vn_ref[:, ls]], axis=0)
        vtbuf[hp] = v.astype(F32).T.astype(BF16)
    lane = lax.broadcasted_iota(jnp.int32, (1, LANES), 1)
    even = lane < HEAD_DIM
    contract_last = (((1,), (1,)), ((), ()))
    first_step = i == 0
    last_step = i == pl.num_programs(2) - 1

    def scores(sb, hp):
        r0 = sb * DIL_Q
        ls = slice(hp * LANES, (hp + 1) * LANES)
        if sb == 0:
            variant = jnp.where(first_step, 1, 0)
        elif sb == n_sub - 1:
            variant = jnp.where(last_step, 2, 0)
        else:
            variant = 0
        q = q_ref[r0:r0 + DIL_Q, ls]
        zero = jnp.zeros_like(q)
        qcat = jnp.concatenate([jnp.where(even, q, zero), jnp.where(even, zero, q)], axis=0)
        s = lax.dot_general(kbuf[r0:r0 + DIL_K, ls], qcat, contract_last, preferred_element_type=F32)
        return s + bias_ref[variant, hp]

    def finish(sb, hp, s):
        r0 = sb * DIL_Q
        ls = slice(hp * LANES, (hp + 1) * LANES)
        m = jnp.max(s, axis=0, keepdims=True)
        p = jnp.exp2(s - m)
        l = jnp.sum(p, axis=0, keepdims=True)
        ot = jnp.dot(vtbuf[hp, :, r0:r0 + DIL_K], p.astype(BF16), preferred_element_type=F32) / l
        ot = jnp.concatenate([ot[:HEAD_DIM, :DIL_Q], ot[HEAD_DIM:, DIL_Q:]], axis=0)
        o_ref[r0:r0 + DIL_Q, ls] = ot.T.astype(o_ref.dtype)
        lse = (m + jnp.log2(l)) * (1.0 / LOG2E)
        lse_rows.extend([lse[:, :DIL_Q], lse[:, DIL_Q:]])
        if hp == n_pair - 1:
            tile = jnp.concatenate([jnp.broadcast_to(row, (LSE_REP, DIL_Q)) for row in lse_rows], axis=0)
            lse_ref[r0:r0 + DIL_Q, :] = tile.T
            lse_rows.clear()

    lse_rows = []
    bodies = [(sb, hp) for sb in range(n_sub) for hp in range(n_pair)]
    staged = [scores(*b) for b in bodies[:DIL_AHEAD]]
    for n, b in enumerate(bodies):
        if n + DIL_AHEAD < len(bodies):
            staged.append(scores(*bodies[n + DIL_AHEAD]))
        finish(*b, staged[n])


def _dilated(q, k, v, bias, dil):
    B, L, _ = q.shape
    W = WIDTH_A
    lse_spec = pl.BlockSpec((None, min(DIL_TL, L), LANES), lambda b, r, i: (b, i, r))
    tl = min(DIL_TL, L)
    nhalo = L // HALF_WIN
    cur = pl.BlockSpec((None, tl, W), lambda b, r, i: (b, i, r))
    prev = pl.BlockSpec((None, HALF_WIN, W),
                        lambda b, r, i: (b, jnp.maximum(i * (tl // HALF_WIN) - 1, 0), r))
    nxt = pl.BlockSpec((None, HALF_WIN, W),
                       lambda b, r, i: (b, jnp.minimum((i + 1) * (tl // HALF_WIN), nhalo - 1), r))
    bias_spec = pl.BlockSpec(bias.shape, lambda b, r, i: (0,) * bias.ndim)
    return pl.pallas_call(
        _dil_kernel,
        grid=(B, dil, L // tl),
        in_specs=[cur, prev, cur, nxt, prev, cur, nxt, bias_spec],
        out_specs=[cur, lse_spec],
        out_shape=[jax.ShapeDtypeStruct((B, L, dil * W), BF16), jax.ShapeDtypeStruct((B, L, dil * LANES), F32)],
        scratch_shapes=[pltpu.VMEM((tl + 2 * HALF_WIN, W), BF16),
                        pltpu.VMEM((N_HEADS_A // 2, LANES, tl + 2 * HALF_WIN), BF16)],
        compiler_params=_params("arbitrary", "arbitrary", "arbitrary"),
        name=f"dilated_{dil}",
    )(q, k, k, k, v, v, v, bias)


def _diff_kernel(tbl_ref, q_ref, qn_ref, k1_ref, k2_ref, vt_ref, band_ref, bmax_ref, lq1_ref, lk1_ref, lq2_ref,
                 lk2_ref, g_ref, o_ref, m1, l1, a1, m2, l2, a2, sa, sb, mxa, mxb, *, lambda_init):
    h = pl.program_id(1)
    i = pl.program_id(2)
    T = DIFF_TK
    nk = k1_ref.shape[0] // T
    streams = ((k1_ref, m1, l1, a1), (k2_ref, m2, l2, a2))
    for _, m, l, a in streams:
        m[...] = jnp.full(m.shape, NEG, F32)
        l[...] = jnp.zeros(l.shape, F32)
        a[...] = jnp.zeros(a.shape, F32)
    q = q_ref[...]
    contract_last = (((1,), (1,)), ((), ()))

    def scores(j, sbuf, mxbuf, st, wraps=False):
        if wraps:
            wrap = j == nk
            j = jnp.where(wrap, 0, j)
            qq = jnp.where(wrap, qn_ref[...], q)
        else:
            qq = q
        r0 = pl.multiple_of(j * T, T)
        s = lax.dot_general(streams[st][0][pl.ds(r0, T), :], qq, contract_last, preferred_element_type=F32)
        sbuf[st] = s
        mxbuf[st] = jnp.max(s, axis=0, keepdims=True)

    def softmax_pv(j, sbuf, mxbuf, st, near, shift):
        r0 = pl.multiple_of(j * T, T)
        _, m, l, a = streams[st]
        s = sbuf[st]
        if near:
            t = jnp.clip(DIFF_RATIO * j - i, DIFF_E_LO - 1, DIFF_E_HI + 1) - (DIFF_E_LO - 1)
            s = s + band_ref[t]
            bound = mxbuf[st] + bmax_ref[t]
        else:
            bound = mxbuf[st] + shift
        m_prev = m[...]
        m_new = jnp.maximum(m_prev, bound)
        alpha = jnp.exp2(m_prev - m_new)
        p = jnp.exp2(s - (m_new - shift))
        l[...] = alpha * l[...] + jnp.sum(p, axis=0, keepdims=True)
        a[...] = alpha * a[...] + jnp.dot(vt_ref[:, pl.ds(r0, T)], p.astype(BF16), preferred_element_type=F32)
        m[...] = m_new

    def group(near, shift):
        def body(g, carry):
            j0 = DIFF_GROUP * g
            bufs = ((sa, mxa), (sb, mxb))
            for u in range(DIFF_GROUP):
                cur, nxt = bufs[u % 2], bufs[(u + 1) % 2]
                for st in range(2):
                    scores(j0 + u + 1, *nxt, st, wraps=u == DIFF_GROUP - 1)
                    softmax_pv(j0 + u, *cur, st, near, shift)
            return carry
        return body

    @pl.when(i == 0)
    def _():
        scores(0, sa, mxa, 0)
        scores(0, sa, mxa, 1)

    j_lo = (i + DIFF_E_LO + DIFF_RATIO - 1) // DIFF_RATIO
    j_hi = (i + DIFF_E_HI) // DIFF_RATIO
    lo = jnp.maximum(j_lo // DIFF_GROUP, 0)
    hi = jnp.minimum(j_hi // DIFF_GROUP + 1, nk // DIFF_GROUP)
    c_neg = tbl_ref[NUM_BUCKETS // 2 - 1, N_HEADS_A + h] * LOG2E
    c_pos = tbl_ref[NUM_BUCKETS - 1, N_HEADS_A + h] * LOG2E
    lax.fori_loop(0, lo, group(False, c_neg), 0)
    lax.fori_loop(lo, hi, group(True, 0.0), 0)
    lax.fori_loop(hi, nk // DIFF_GROUP, group(False, c_pos), 0)

    lam =(jnp.exp(jnp.sum(lq1_ref[...] * lk1_ref[...], axis=-1, keepdims=True))
           - jnp.exp(jnp.sum(lq2_ref[...] * lk2_ref[...], axis=-1, keepdims=True)) + lambda_init)
    o = a1[...] / l1[...] - lam * (a2[...] / l2[...])
    ms = jnp.mean(o * o, axis=0, keepdims=True)
    o = o * lax.rsqrt(ms + EPS) * (g_ref[...] * (1.0 - lambda_init))
    o_ref[...] = o.T.astype(o_ref.dtype)


def _diff_attention(rel_bias, qb, k1, k2, vbt, band, bmax, lq1, lk1, lq2, lk2, g_col, lambda_init):
    B, S, W = qb.shape
    T, TK = DIFF_TQ, DIFF_TK
    qspec = pl.BlockSpec((None, T, LANES), lambda b, h, i: (b, i, h))
    qnext = pl.BlockSpec((None, T, LANES), lambda b, h, i: (b, jnp.minimum(i + 1, S // T - 1), h))
    kspec = pl.BlockSpec((None, S, LANES), lambda b, h, i: (b, 0, h))
    vspec = pl.BlockSpec((None, LANES, S), lambda b, h, i: (b, h, 0))
    band_spec = pl.BlockSpec((None, DIFF_NT, TK, T), lambda b, h, i: (h, 0, 0, 0))
    bmax_spec = pl.BlockSpec((None, DIFF_NT, 1, T), lambda b, h, i: (h, 0, 0, 0))
    vec = lambda n: pl.BlockSpec((1, n), lambda b, h, i: (0, 0))
    stat = pltpu.VMEM((1, T), F32)
    acc = pltpu.VMEM((LANES, T), F32)
    return pl.pallas_call(
        functools.partial(_diff_kernel, lambda_init=lambda_init),
        grid=(B, N_HEADS_B, S // T),
        in_specs=[pl.BlockSpec(memory_space=pltpu.SMEM), qspec, qnext, kspec, kspec, vspec, band_spec, bmax_spec,
                  vec(HEAD_DIM), vec(HEAD_DIM), vec(HEAD_DIM), vec(HEAD_DIM),
                  pl.BlockSpec((LANES, 1), lambda b, h, i: (0, 0))],
        out_specs=qspec,
        out_shape=jax.ShapeDtypeStruct((B, S, W), BF16),
        scratch_shapes=[stat, stat, acc, stat, stat, acc, pltpu.VMEM((2, TK, T), F32), pltpu.VMEM((2, TK, T), F32),
                        pltpu.VMEM((2, 1, T), F32), pltpu.VMEM((2, 1, T), F32)],
        compiler_params=_params("arbitrary", "arbitrary", "arbitrary"),
        name="diff_attn",
    )(rel_bias, qb, qb, k1, k2, vbt, band, bmax, lq1, lk1, lq2, lk2, g_col)


def _out_proj_kernel(o1_ref, o4_ref, o16_ref, s1_ref, s4_ref, s16_ref, ob_ref, x_ref, gate_ref, shift_ref,
                     scale_ref, g_ref, wa_ref, wb_ref, sel_ref, h_ref, hn_ref, nat, nat_lse, oa_buf):
    tm = x_ref.shape[0]
    for n, (dil, o_ref, s_ref) in enumerate(((4, o4_ref, s4_ref), (16, o16_ref, s16_ref))):
        rows = tm // dil
        for r in range(dil):
            nat_lse[n, pl.ds(r, rows, stride=dil), :] = s_ref[:, r * LANES:(r + 1) * LANES]
            for s in range(WIDTH_A // LANES):
                lo = r * WIDTH_A + s * LANES
                nat[n, s, pl.ds(r, rows, stride=dil), :] = o_ref[:, lo:lo + LANES].astype(F32)
    s1, s2, s3 = s1_ref[...], nat_lse[0], nat_lse[1]
    mx = jnp.maximum(jnp.maximum(s1, s2), s3)
    e1, e2, e3 = jnp.exp(s1 - mx), jnp.exp(s2 - mx), jnp.exp(s3 - mx)
    inv = 1.0 / (e1 + e2 + e3)
    def expand(w):
        hi = w.astype(BF16)
        lo = (w - hi.astype(F32)).astype(BF16)
        sel = sel_ref[...]
        return (jnp.dot(hi, sel, preferred_element_type=F32) + jnp.dot(lo, sel, preferred_element_type=F32))

    w1, w2, w3 = expand(e1 * inv), expand(e2 * inv), expand(e3 * inv)
    for s in range(WIDTH_A // LANES):
        sl = slice(s * LANES, (s + 1) * LANES)
        oa = w1[:, sl] * o1_ref[:, sl].astype(F32) + w2[:, sl] * nat[0, s] + w3[:, sl] * nat[1, s]
        oa_buf[:, sl] = oa.astype(BF16)
    mixed = (jnp.dot(oa_buf[...], wa_ref[...], preferred_element_type=F32)
             + jnp.dot(ob_ref[...], wb_ref[...], preferred_element_type=F32))
    h = x_ref[...] + gate_ref[...] * mixed
    h_ref[...] = h
    ms = jnp.mean(h * h, axis=-1, keepdims=True)
    hn = h * lax.rsqrt(ms + EPS) * g_ref[...]
    hn_ref[...] = (hn * (1.0 + scale_ref[...]) + shift_ref[...]).astype(hn_ref.dtype)


def _head_select():
    src = jnp.arange(LANES)[:, None]
    head = jnp.arange(WIDTH_A)[None, :] // HEAD_DIM
    return (src == head * LSE_REP).astype(BF16)


def _out_proj(oas, lses, ob, x, gate, shift, scale, g, wa, wb):
    B, S, D = x.shape
    tm = OUT_TM
    sel = _head_select()
    half = pl.BlockSpec((None, tm, WIDTH_B), lambda b, i: (b, i, 0))
    row = pl.BlockSpec((None, tm, D), lambda b, i: (b, i, 0))
    mod = pl.BlockSpec((None, 1, D), lambda b, i: (b, 0, 0))
    const = lambda shape: pl.BlockSpec(shape, lambda b, i: (0,) * len(shape))
    res = lambda dil, w: pl.BlockSpec((None, tm // dil, dil * w), lambda b, i: (b, i, 0))
    outs = [res(1, WIDTH_A), res(4, WIDTH_A), res(16, WIDTH_A)]
    lse_specs = [res(1, LANES), res(4, LANES), res(16, LANES)]
    return pl.pallas_call(
        _out_proj_kernel,
        grid=(B, S // tm),
        in_specs=outs + lse_specs + [half, row, mod, mod, mod, const((1, D)), const(wa.shape), const(wb.shape),
                                     const(sel.shape)],
        out_specs=[row, row],
        out_shape=[jax.ShapeDtypeStruct((B, S, D), F32), jax.ShapeDtypeStruct((B, S, D), BF16)],
        scratch_shapes=[pltpu.VMEM((2, WIDTH_A // LANES, tm, LANES), F32), pltpu.VMEM((2, tm, LANES), F32),
                        pltpu.VMEM((tm, WIDTH_A), BF16)],
        compiler_params=_params("arbitrary", "arbitrary"),
        name="out_proj",
    )(*oas, *lses, ob, x, gate, shift, scale, g, wa, wb, sel)


def _ffn_kernel(hp_ref, hc_ref, hn_ref, h_ref, gate_ref, wu_ref, cw_ref, cb_ref, wd_ref, o_ref, lhs, act):
    i = pl.program_id(1)
    tm = hc_ref.shape[0]
    n = tm + 2 * FFN_HALO
    lhs[0:FFN_HALO, :] = jnp.where(i > 0, hp_ref[...], jnp.zeros_like(hp_ref))
    lhs[FFN_HALO:FFN_HALO + tm, :] = hc_ref[...]
    lhs[FFN_HALO + tm:, :] = jnp.where(i < pl.num_programs(1) - 1, hn_ref[...], jnp.zeros_like(hn_ref))
    x = lhs[...]

    def conv(u, lo):
        cw = cw_ref[:, lo:lo + FFN_FC]
        mid = slice(FFN_HALO, FFN_HALO + tm)
        below = pltpu.roll(u, 1, 0)[mid]
        above = pltpu.roll(u, n - 1, 0)[mid]
        return cw[0:1] * below + cw[1:2] * u[mid] + cw[2:3] * above + cb_ref[:, lo:lo + FFN_FC]

    for c in range(D_FF // FFN_FC):
        lo_v = c * FFN_FC
        lo_g = D_FF + c * FFN_FC
        val = conv(jnp.dot(x, wu_ref[:, lo_v:lo_v + FFN_FC], preferred_element_type=F32), lo_v)
        gt = conv(jnp.dot(x, wu_ref[:, lo_g:lo_g + FFN_FC], preferred_element_type=F32), lo_g)
        act[:, lo_v:lo_v + FFN_FC] = (gt / (1.0 + jnp.exp(-gt)) * val).astype(BF16)
    down = jnp.dot(act[...], wd_ref[...], preferred_element_type=F32)
    o_ref[...] = h_ref[...] + gate_ref[...] * down


def _ffn(hn, h, gate, wu, cw, cb, wd):
    B, S, D = h.shape
    tm = FFN_TM
    per = tm // FFN_HALO
    nh = S // FFN_HALO
    row = lambda dt: pl.BlockSpec((None, tm, D), lambda b, i: (b, i, 0))
    prev = pl.BlockSpec((None, FFN_HALO, D), lambda b, i: (b, jnp.maximum(i * per - 1, 0), 0))
    nxt = pl.BlockSpec((None, FFN_HALO, D), lambda b, i: (b, jnp.minimum((i + 1) * per, nh - 1), 0))
    mod = pl.BlockSpec((None, 1, D), lambda b, i: (b, 0, 0))
    const = lambda shape: pl.BlockSpec(shape, lambda b, i: (0,) * len(shape), pipeline_mode=pl.Buffered(1))
    return pl.pallas_call(
        _ffn_kernel,
        grid=(B, S // tm),
        in_specs=[prev, row(BF16), nxt, row(F32), mod, const(wu.shape), const(cw.shape), const(cb.shape),
                  const(wd.shape)],
        out_specs=row(F32),
        out_shape=jax.ShapeDtypeStruct((B, S, D), F32),
        scratch_shapes=[pltpu.VMEM((tm + 2 * FFN_HALO, D), BF16), pltpu.VMEM((tm, D_FF), BF16)],
        compiler_params=_params("arbitrary", "arbitrary"),
        name="conv_ffn",
    )(hn, hn, hn, h, gate, wu, cw, cb, wd)


def _block_diag_ones():
    r = jnp.arange(MXU_DIM) // HEAD_DIM
    return (r[:, None] == r[None, :]).astype(BF16)


def kernel(x, c, w_ada, b_ada, norm1_g, w_in, q_norm_a, k_norm_a, q_norm_b, k_norm_b, rel_bias, lambda_q1,
           lambda_k1, lambda_q2, lambda_k2, subln_g, w_out, norm2_g, w_up, conv_w, conv_b, w_down):
    B, S, D = x.shape
    depth = w_ada.shape[0]
    h = x.astype(F32)
    c8 = jnp.pad(c.astype(F32), ((0, 8 - B), (0, 0)))
    bd = _block_diag_ones()
    rel_bias = rel_bias.astype(F32)
    band_b, bmax_b = _diff_bias(rel_bias)
    band_a = _dil_bias(rel_bias)
    qscale = HEAD_DIM ** -0.5
    tile8 = lambda g: jnp.tile(g.astype(F32), WIDTH_A // HEAD_DIM)[None, :]
    row = lambda v: v.astype(F32)[None, :]

    for layer in range(depth):
        lambda_init = 0.8 - 0.6 * math.exp(-0.3 * layer)
        mod = _ada(c8, w_ada[layer].astype(F32), row(b_ada[layer]))[:B]
        shift1, scale1, gate1, shift2, scale2, gate2 = [m[:, None, :] for m in jnp.split(mod, 6, axis=-1)]

        qa, ka, va, qa4, ka4, va4, qa16, ka16, va16, qb, k1, k2, vbt = _in_proj(
            h, shift1, scale1, row(norm1_g[layer]), w_in[layer].astype(BF16), bd,
            tile8(q_norm_a[layer]) * (qscale * LOG2E), tile8(k_norm_a[layer]),
            tile8(q_norm_b[layer]) * (qscale * LOG2E), tile8(k_norm_b[layer]))

        oas, lses = [], []
        qkv = {1: (qa, ka, va), 4: (qa4, ka4, va4), 16: (qa16, ka16, va16)}
        for p, (_, dil) in enumerate(DILATED_PATTERNS):
            o, lse = _dilated(*qkv[dil], band_a[p], dil)
            oas.append(o)
            lses.append(lse)

        ob = _diff_attention(rel_bias, qb, k1, k2, vbt, band_b, bmax_b, row(lambda_q1[layer]), row(lambda_k1[layer]),
                             row(lambda_q2[layer]), row(lambda_k2[layer]),
                             subln_g[layer].astype(F32)[:, None], lambda_init)

        w_o = w_out[layer].astype(BF16)
        h, hn = _out_proj(oas, lses, ob, h, gate1, shift2, scale2, row(norm2_g[layer]),
                          w_o[:WIDTH_A], w_o[WIDTH_A:])
        h = _ffn(hn, h, gate2, w_up[layer].astype(BF16), conv_w[layer].astype(F32), row(conv_b[layer]),
                 w_down[layer].astype(BF16))

    return h.astype(x.dtype)
```

```python
import functools
import math

import jax
import jax.numpy as jnp
from jax import lax
from jax.experimental import pallas as pl
from jax.experimental.pallas import tpu as pltpu

F32 = jnp.float32
BF16 = jnp.bfloat16

D_MODEL = 1024
HEAD_DIM = 64
WIDTH_A = 512
WIDTH_B = 512
N_HEADS_A = 8
N_HEADS_B = 4
DILATED_PATTERNS = ((128, 1), (512, 4), (2048, 16))
D_FF = 2816
NUM_BUCKETS = 32
MAX_DISTANCE = 2048
EPS = 1e-6
NEG = -1e30
LOG2E = math.log2(math.e)

LANES = 128
BF16_SUBLANES = 16
MXU_DIM = 256
VMEM_LIMIT = 56 * 1024 * 1024

HALF_WIN = 64
DIL_Q = 128
DIL_K = DIL_Q + 2 * HALF_WIN
DIL_TL = 1024
LSE_REP = LANES // N_HEADS_A
DIL_AHEAD = 4

DIFF_TQ = 512
DIFF_TK = 512
DIFF_RATIO = DIFF_TK // DIFF_TQ
BUCKET_SAT = 1024
DIFF_E_LO = -(BUCKET_SAT + DIFF_TK) // DIFF_TQ + 1
DIFF_E_HI = (BUCKET_SAT + DIFF_TQ) // DIFF_TQ - 1
DIFF_NT = DIFF_E_HI - DIFF_E_LO + 3
DIFF_GROUP = 8

PROJ_TM = 512
PROJ_GROUP = WIDTH_A
ADA_TN = 1024
OUT_TM = 1024
FFN_TM = 1024
FFN_FC = 256
FFN_HALO = BF16_SUBLANES

_BUCKET_THRESHOLDS = (1, 2, 3, 4, 5, 6, 7, 8, 16, 32, 64, 128, 256, 512, 1024)


def _params(*sem):
    return pltpu.CompilerParams(dimension_semantics=sem, vmem_limit_bytes=VMEM_LIMIT)


def _ada_kernel(c_ref, w_ref, b_ref, o_ref):
    c = c_ref[...]
    ca = c / (1.0 + jnp.exp(-c))
    o_ref[...] = jnp.dot(ca, w_ref[...], preferred_element_type=F32) + b_ref[...]


def _ada(c8, w, b):
    n = w.shape[1]
    tn = ADA_TN
    return pl.pallas_call(
        _ada_kernel,
        grid=(n // tn,),
        in_specs=[pl.BlockSpec((8, D_MODEL), lambda j: (0, 0)),
                  pl.BlockSpec((D_MODEL, tn), lambda j: (0, j)),
                  pl.BlockSpec((1, tn), lambda j: (0, j))],
        out_specs=pl.BlockSpec((8, tn), lambda j: (0, j)),
        out_shape=jax.ShapeDtypeStruct((8, n), F32),
        compiler_params=_params("arbitrary"),
        name="ada",
    )(c8, w, b)


def _head_norm(p, bd, gain):
    ss = jnp.dot((p * p).astype(BF16), bd, preferred_element_type=F32)
    return p * lax.rsqrt(ss * (1.0 / HEAD_DIM) + EPS) * gain


def _in_proj_kernel(x_ref, shift_ref, scale_ref, g_ref, w_ref, bd_ref, gqa_ref, gka_ref, gqb_ref, gkb_ref,
                    qa_ref, ka_ref, va_ref, qa4_ref, ka4_ref, va4_ref, qa16_ref, ka16_ref, va16_ref,
                    qb_ref, k1_ref, k2_ref, vbt_ref, pbuf, stage, stage4):
    tm = x_ref.shape[0]
    x = x_ref[...]
    ms = jnp.mean(x * x, axis=-1, keepdims=True)
    hn = x * lax.rsqrt(ms + EPS) * g_ref[...]
    hn = (hn * (1.0 + scale_ref[...]) + shift_ref[...]).astype(BF16)
    pbuf[...] = jnp.dot(hn, w_ref[...], preferred_element_type=F32)
    bd = bd_ref[...]
    lane = lax.broadcasted_iota(jnp.int32, (1, MXU_DIM), 1)
    first = (lane // HEAD_DIM) % 2 == 0

    def proj(g, c):
        lo = g * PROJ_GROUP + c * MXU_DIM
        return pbuf[:, lo:lo + MXU_DIM]

    for c in range(2):
        sl = slice(c * MXU_DIM, (c + 1) * MXU_DIM)
        group_a = (_head_norm(proj(0, c), bd, gqa_ref[:, sl]), _head_norm(proj(1, c), bd, gka_ref[:, sl]),
                   proj(2, c))
        for a, (val, ref) in enumerate(zip(group_a, (qa_ref, ka_ref, va_ref))):
            ref[:, sl] = val.astype(BF16)
            stage[a, 2 * c] = val[:, :LANES]
            stage[a, 2 * c + 1] = val[:, LANES:]
        qb_ref[:, sl] = _head_norm(proj(3, c), bd, gqb_ref[:, sl]).astype(BF16)
        kb = _head_norm(proj(4, c), bd, gkb_ref[:, sl])
        k1_ref[:, sl] = jnp.where(first, kb, 0.0).astype(BF16)
        k2_ref[:, sl] = jnp.where(first, 0.0, kb).astype(BF16)
        vbt_ref[sl, :] = proj(5, c).T.astype(BF16)

    for a, (ref4, ref16) in enumerate(((qa4_ref, qa16_ref), (ka4_ref, ka16_ref), (va4_ref, va16_ref))):
        for lo_r in range(4):
            for s in range(WIDTH_A // LANES):
                part = stage[a, s, pl.ds(lo_r, tm // 4, stride=4), :]
                stage4[a, lo_r, s] = part
                ref4[:, lo_r * WIDTH_A + s * LANES:lo_r * WIDTH_A + (s + 1) * LANES] = part.astype(BF16)
        for hi_r in range(4):
            for lo_r in range(4):
                for s in range(WIDTH_A // LANES):
                    lo = (4 * hi_r + lo_r) * WIDTH_A + s * LANES
                    part = stage4[a, lo_r, s, pl.ds(hi_r, tm // 16, stride=4), :]
                    ref16[:, lo:lo + LANES] = part.astype(BF16)


def _in_proj(x, shift, scale, g, w_bf, bd, gqa, gka, gqb, gkb):
    B, S, D = x.shape
    tm = PROJ_TM
    row = pl.BlockSpec((None, tm, D), lambda b, i: (b, i, 0))
    mod = pl.BlockSpec((None, 1, D), lambda b, i: (b, 0, 0))
    const = lambda shape: pl.BlockSpec(shape, lambda b, i: (0,) * len(shape))
    out = pl.BlockSpec((None, tm, PROJ_GROUP), lambda b, i: (b, i, 0))
    out_t = pl.BlockSpec((None, PROJ_GROUP, tm), lambda b, i: (b, 0, i))
    res = lambda dil: pl.BlockSpec((None, tm // dil, dil * WIDTH_A), lambda b, i: (b, i, 0))
    res_shape = lambda dil: jax.ShapeDtypeStruct((B, S // dil, dil * WIDTH_A), BF16)
    nat_shape = jax.ShapeDtypeStruct((B, S, PROJ_GROUP), BF16)
    gain = const((1, PROJ_GROUP))
    return pl.pallas_call(
        _in_proj_kernel,
        grid=(B, S // tm),
        in_specs=[row, mod, mod, const((1, D)), const(w_bf.shape), const(bd.shape),
                  gain, gain, gain, gain],
        out_specs=[out] * 3 + [res(4)] * 3 + [res(16)] * 3 + [out] * 3 + [out_t],
        out_shape=[nat_shape] * 3 + [res_shape(4)] * 3 + [res_shape(16)] * 3 + [nat_shape] * 3
                  + [jax.ShapeDtypeStruct((B, PROJ_GROUP, S), BF16)],
        scratch_shapes=[pltpu.VMEM((tm, w_bf.shape[1]), F32),
                        pltpu.VMEM((3, WIDTH_A // LANES, tm, LANES), F32),
                        pltpu.VMEM((3, 4, WIDTH_A // LANES, tm // 4, LANES), F32)],
        compiler_params=_params("arbitrary", "arbitrary"),
        name="in_proj",
    )(x, shift, scale, g, w_bf, bd, gqa, gka, gqb, gkb)


def _bias_from_rel(rel, tbl_ref, h):
    n = jnp.abs(rel)
    vneg = jnp.full(rel.shape, tbl_ref[0, h], F32)
    vpos = jnp.full(rel.shape, tbl_ref[NUM_BUCKETS // 2, h], F32)
    for i, th in enumerate(_BUCKET_THRESHOLDS):
        ge = n >= th
        vneg = jnp.where(ge, tbl_ref[i + 1, h], vneg)
        vpos = jnp.where(ge, tbl_ref[NUM_BUCKETS // 2 + i + 1, h], vpos)
    return jnp.where(rel > 0, vpos, vneg)


def _diff_bias_kernel(tbl_ref, o_ref, omax_ref, sub):
    h = pl.program_id(0)
    nk, nq = DIFF_TK // LANES, DIFF_TQ // LANES
    base = (DIFF_E_LO - 1) * nq
    lo_u = base - (nq - 1)
    n_sub = (DIFF_NT - 1) * nq + nk + nq - 1
    k = lax.broadcasted_iota(jnp.int32, (LANES, LANES), 0)
    q = lax.broadcasted_iota(jnp.int32, (LANES, LANES), 1)
    def one_offset(n, carry):
        sub[n] = _bias_from_rel(k - q + (lo_u + n) * LANES, tbl_ref, N_HEADS_A + h) * LOG2E
        return carry

    lax.fori_loop(0, n_sub, one_offset, 0)
    for t in range(DIFF_NT):
        for b in range(nq):
            cols = slice(b * LANES, (b + 1) * LANES)
            for a in range(nk):
                o_ref[t, a * LANES:(a + 1) * LANES, cols] = sub[base + t * nq + a - b - lo_u]
            omax_ref[t, :, cols] = jnp.max(o_ref[t, :, cols], axis=0, keepdims=True)


def _diff_bias(rel_bias):
    n_sub = (DIFF_NT - 1) * (DIFF_TQ // LANES) + DIFF_TK // LANES + DIFF_TQ // LANES - 1
    return pl.pallas_call(
        _diff_bias_kernel,
        grid=(N_HEADS_B,),
        in_specs=[pl.BlockSpec(memory_space=pltpu.SMEM)],
        out_specs=[pl.BlockSpec((None, DIFF_NT, DIFF_TK, DIFF_TQ), lambda h: (h, 0, 0, 0)),
                   pl.BlockSpec((None, DIFF_NT, 1, DIFF_TQ), lambda h: (h, 0, 0, 0))],
        out_shape=[jax.ShapeDtypeStruct((N_HEADS_B, DIFF_NT, DIFF_TK, DIFF_TQ), F32),
                   jax.ShapeDtypeStruct((N_HEADS_B, DIFF_NT, 1, DIFF_TQ), F32)],
        scratch_shapes=[pltpu.VMEM((n_sub, LANES, LANES), F32)],
        compiler_params=_params("arbitrary"),
        name="diff_bias",
    )(rel_bias)


def _dil_bias_kernel(tbl_ref, o_ref):
    p = pl.program_id(0)
    hp = pl.program_id(1)
    shape = (DIL_K, DIL_Q)
    k = lax.broadcasted_iota(jnp.int32, shape, 0)
    q = lax.broadcasted_iota(jnp.int32, shape, 1)
    steps = k - HALF_WIN - q
    dil = lax.shift_left(jnp.int32(1), 2 * p)
    inside = jnp.abs(steps) <= HALF_WIN
    for par in range(2):
        bias = jnp.where(inside, _bias_from_rel(steps * dil, tbl_ref, 2 * hp + par) * LOG2E, NEG)
        ls = slice(par * DIL_Q, (par + 1) * DIL_Q)
        o_ref[0, :, ls] = bias
        o_ref[1, :, ls] = jnp.where(k >= HALF_WIN, bias, NEG)
        o_ref[2, :, ls] = jnp.where(k < DIL_K - HALF_WIN, bias, NEG)


def _dil_bias(rel_bias):
    n_pat = len(DILATED_PATTERNS)
    n_pair = N_HEADS_A // 2
    return pl.pallas_call(
        _dil_bias_kernel,
        grid=(n_pat, n_pair),
        in_specs=[pl.BlockSpec(memory_space=pltpu.SMEM)],
        out_specs=pl.BlockSpec((None, 3, None, DIL_K, 2 * DIL_Q), lambda p, hp: (p, 0, hp, 0, 0)),
        out_shape=jax.ShapeDtypeStruct((n_pat, 3, n_pair, DIL_K, 2 * DIL_Q), F32),
        compiler_params=_params("arbitrary", "arbitrary"),
        name="dil_bias",
    )(rel_bias)


def _dil_kernel(q_ref, kp_ref, kc_ref, kn_ref, vp_ref, vc_ref, vn_ref, bias_ref, o_ref, lse_ref, kbuf, vtbuf):
    tl = q_ref.shape[0]
    i = pl.program_id(2)
    n_sub = tl // DIL_Q
    n_pair = N_HEADS_A // 2
    kbuf[0:HALF_WIN, :] = kp_ref[...]
    kbuf[HALF_WIN:HALF_WIN + tl, :] = kc_ref[...]
    kbuf[HALF_WIN + tl:, :] = kn_ref[...]
    for hp in range(n_pair):
        ls = slice(hp * LANES, (hp + 1) * LANES)
        v = jnp.concatenate([vp_ref[:, ls], vc_ref[:, ls], vn_ref[:, ls]], axis=0)
        vtbuf[hp] = v.astype(F32).T.astype(BF16)
    lane = lax.broadcasted_iota(jnp.int32, (1, LANES), 1)
    even = lane < HEAD_DIM
    contract_last = (((1,), (1,)), ((), ()))
    first_step = i == 0
    last_step = i == pl.num_programs(2) - 1

    def scores(sb, hp):
        r0 = sb * DIL_Q
        ls = slice(hp * LANES, (hp + 1) * LANES)
        if sb == 0:
            variant = jnp.where(first_step, 1, 0)
        elif sb == n_sub - 1:
            variant = jnp.where(last_step, 2, 0)
        else:
            variant = 0
        q = q_ref[r0:r0 + DIL_Q, ls]
        zero = jnp.zeros_like(q)
        qcat = jnp.concatenate([jnp.where(even, q, zero), jnp.where(even, zero, q)], axis=0)
        s = lax.dot_general(kbuf[r0:r0 + DIL_K, ls], qcat, contract_last, preferred_element_type=F32)
        return s + bias_ref[variant, hp]

    def finish(sb, hp, s):
        r0 = sb * DIL_Q
        ls = slice(hp * LANES, (hp + 1) * LANES)
        m = jnp.max(s, axis=0, keepdims=True)
        p = jnp.exp2(s - m)
        l = jnp.sum(p, axis=0, keepdims=True)
        ot = jnp.dot(vtbuf[hp, :, r0:r0 + DIL_K], p.astype(BF16), preferred_element_type=F32) / l
        ot = jnp.concatenate([ot[:HEAD_DIM, :DIL_Q], ot[HEAD_DIM:, DIL_Q:]], axis=0)
        o_ref[r0:r0 + DIL_Q, ls] = ot.T.astype(o_ref.dtype)
        lse = (m + jnp.log2(l)) * (1.0 / LOG2E)
        lse_rows.extend([lse[:, :DIL_Q], lse[:, DIL_Q:]])
        if hp == n_pair - 1:
            tile = jnp.concatenate([jnp.broadcast_to(row, (LSE_REP, DIL_Q)) for row in lse_rows], axis=0)
            lse_ref[r0:r0 + DIL_Q, :] = tile.T
            lse_rows.clear()

    lse_rows = []
    bodies = [(sb, hp) for sb in range(n_sub) for hp in range(n_pair)]
    staged = [scores(*b) for b in bodies[:DIL_AHEAD]]
    for n, b in enumerate(bodies):
        if n + DIL_AHEAD < len(bodies):
            staged.append(scores(*bodies[n + DIL_AHEAD]))
        finish(*b, staged[n])


def _dilated(q, k, v, bias, dil):
    B, L, _ = q.shape
    W = WIDTH_A
    lse_spec = pl.BlockSpec((None, min(DIL_TL, L), LANES), lambda b, r, i: (b, i, r))
    tl = min(DIL_TL, L)
    nhalo = L // HALF_WIN
    cur = pl.BlockSpec((None, tl, W), lambda b, r, i: (b, i, r))
    prev = pl.BlockSpec((None, HALF_WIN, W),
                        lambda b, r, i: (b, jnp.maximum(i * (tl // HALF_WIN) - 1, 0), r))
    nxt = pl.BlockSpec((None, HALF_WIN, W),
                       lambda b, r, i: (b, jnp.minimum((i + 1) * (tl // HALF_WIN), nhalo - 1), r))
    bias_spec = pl.BlockSpec(bias.shape, lambda b, r, i: (0,) * bias.ndim)
    return pl.pallas_call(
        _dil_kernel,
        grid=(B, dil, L // tl),
        in_specs=[cur, prev, cur, nxt, prev, cur, nxt, bias_spec],
        out_specs=[cur, lse_spec],
        out_shape=[jax.ShapeDtypeStruct((B, L, dil * W), BF16), jax.ShapeDtypeStruct((B, L, dil * LANES), F32)],
        scratch_shapes=[pltpu.VMEM((tl + 2 * HALF_WIN, W), BF16),
                        pltpu.VMEM((N_HEADS_A // 2, LANES, tl + 2 * HALF_WIN), BF16)],
        compiler_params=_params("arbitrary", "arbitrary", "arbitrary"),
        name=f"dilated_{dil}",
    )(q, k, k, k, v, v, v, bias)


def _diff_kernel(tbl_ref, q_ref, qn_ref, k1_ref, k2_ref, vt_ref, band_ref, bmax_ref, lq1_ref, lk1_ref, lq2_ref,
                 lk2_ref, g_ref, o_ref, m1, l1, a1, m2, l2, a2, sa, sb, mxa, mxb, *, lambda_init):
    h = pl.program_id(1)
    i = pl.program_id(2)
    T = DIFF_TK
    nk = k1_ref.shape[0] // T
    streams = ((k1_ref, m1, l1, a1), (k2_ref, m2, l2, a2))
    for _, m, l, a in streams:
        m[...] = jnp.full(m.shape, NEG, F32)
        l[...] = jnp.zeros(l.shape, F32)
        a[...] = jnp.zeros(a.shape, F32)
    q = q_ref[...]
    contract_last = (((1,), (1,)), ((), ()))

    def scores(j, sbuf, mxbuf, st, wraps=False):
        if wraps:
            wrap = j == nk
            j = jnp.where(wrap, 0, j)
            qq = jnp.where(wrap, qn_ref[...], q)
        else:
            qq = q
        r0 = pl.multiple_of(j * T, T)
        s = lax.dot_general(streams[st][0][pl.ds(r0, T), :], qq, contract_last, preferred_element_type=F32)
        sbuf[st] = s
        mxbuf[st] = jnp.max(s, axis=0, keepdims=True)

    def softmax_pv(j, sbuf, mxbuf, st, near, shift):
        r0 = pl.multiple_of(j * T, T)
        _, m, l, a = streams[st]
        s = sbuf[st]
        if near:
            t = jnp.clip(DIFF_RATIO * j - i, DIFF_E_LO - 1, DIFF_E_HI + 1) - (DIFF_E_LO - 1)
            s = s + band_ref[t]
            bound = mxbuf[st] + bmax_ref[t]
        else:
            bound = mxbuf[st] + shift
        m_prev = m[...]
        m_new = jnp.maximum(m_prev, bound)
        alpha = jnp.exp2(m_prev - m_new)
        p = jnp.exp2(s - (m_new - shift))
        l[...] = alpha * l[...] + jnp.sum(p, axis=0, keepdims=True)
        a[...] = alpha * a[...] + jnp.dot(vt_ref[:, pl.ds(r0, T)], p.astype(BF16), preferred_element_type=F32)
        m[...] = m_new

    def group(near, shift):
        def body(g, carry):
            j0 = DIFF_GROUP * g
            bufs = ((sa, mxa), (sb, mxb))
            for u in range(DIFF_GROUP):
                cur, nxt = bufs[u % 2], bufs[(u + 1) % 2]
                for st in range(2):
                    scores(j0 + u + 1, *nxt, st, wraps=u == DIFF_GROUP - 1)
                    softmax_pv(j0 + u, *cur, st, near, shift)
            return carry
        return body

    @pl.when(i == 0)
    def _():
        scores(0, sa, mxa, 0)
        scores(0, sa, mxa, 1)

    j_lo = (i + DIFF_E_LO + DIFF_RATIO - 1) // DIFF_RATIO
    j_hi = (i + DIFF_E_HI) // DIFF_RATIO
    lo = jnp.maximum(j_lo // DIFF_GROUP, 0)
    hi = jnp.minimum(j_hi // DIFF_GROUP + 1, nk // DIFF_GROUP)
    c_neg = tbl_ref[NUM_BUCKETS // 2 - 1, N_HEADS_A + h] * LOG2E
    c_pos = tbl_ref[NUM_BUCKETS - 1, N_HEADS_A + h] * LOG2E
    lax.fori_loop(0, lo, group(False, c_neg), 0)
    lax.fori_loop(lo, hi, group(True, 0.0), 0)
    lax.fori_loop(hi, nk // DIFF_GROUP, group(False, c_pos), 0)

    lam =(jnp.exp(jnp.sum(lq1_ref[...] * lk1_ref[...], axis=-1, keepdims=True))
           - jnp.exp(jnp.sum(lq2_ref[...] * lk2_ref[...], axis=-1, keepdims=True)) + lambda_init)
    o = a1[...] / l1[...] - lam * (a2[...] / l2[...])
    ms = jnp.mean(o * o, axis=0, keepdims=True)
    o = o * lax.rsqrt(ms + EPS) * (g_ref[...] * (1.0 - lambda_init))
    o_ref[...] = o.T.astype(o_ref.dtype)


def _diff_attention(rel_bias, qb, k1, k2, vbt, band, bmax, lq1, lk1, lq2, lk2, g_col, lambda_init):
    B, S, W = qb.shape
    T, TK = DIFF_TQ, DIFF_TK
    qspec = pl.BlockSpec((None, T, LANES), lambda b, h, i: (b, i, h))
    qnext = pl.BlockSpec((None, T, LANES), lambda b, h, i: (b, jnp.minimum(i + 1, S // T - 1), h))
    kspec = pl.BlockSpec((None, S, LANES), lambda b, h, i: (b, 0, h))
    vspec = pl.BlockSpec((None, LANES, S), lambda b, h, i: (b, h, 0))
    band_spec = pl.BlockSpec((None, DIFF_NT, TK, T), lambda b, h, i: (h, 0, 0, 0))
    bmax_spec = pl.BlockSpec((None, DIFF_NT, 1, T), lambda b, h, i: (h, 0, 0, 0))
    vec = lambda n: pl.BlockSpec((1, n), lambda b, h, i: (0, 0))
    stat = pltpu.VMEM((1, T), F32)
    acc = pltpu.VMEM((LANES, T), F32)
    return pl.pallas_call(
        functools.partial(_diff_kernel, lambda_init=lambda_init),
        grid=(B, N_HEADS_B, S // T),
        in_specs=[pl.BlockSpec(memory_space=pltpu.SMEM), qspec, qnext, kspec, kspec, vspec, band_spec, bmax_spec,
                  vec(HEAD_DIM), vec(HEAD_DIM), vec(HEAD_DIM), vec(HEAD_DIM),
                  pl.BlockSpec((LANES, 1), lambda b, h, i: (0, 0))],
        out_specs=qspec,
        out_shape=jax.ShapeDtypeStruct((B, S, W), BF16),
        scratch_shapes=[stat, stat, acc, stat, stat, acc, pltpu.VMEM((2, TK, T), F32), pltpu.VMEM((2, TK, T), F32),
                        pltpu.VMEM((2, 1, T), F32), pltpu.VMEM((2, 1, T), F32)],
        compiler_params=_params("arbitrary", "arbitrary", "arbitrary"),
        name="diff_attn",
    )(rel_bias, qb, qb, k1, k2, vbt, band, bmax, lq1, lk1, lq2, lk2, g_col)


def _out_proj_kernel(o1_ref, o4_ref, o16_ref, s1_ref, s4_ref, s16_ref, ob_ref, x_ref, gate_ref, shift_ref,
                     scale_ref, g_ref, wa_ref, wb_ref, sel_ref, h_ref, hn_ref, nat, nat_lse, oa_buf):
    tm = x_ref.shape[0]
    for n, (dil, o_ref, s_ref) in enumerate(((4, o4_ref, s4_ref), (16, o16_ref, s16_ref))):
        rows = tm // dil
        for r in range(dil):
            nat_lse[n, pl.ds(r, rows, stride=dil), :] = s_ref[:, r * LANES:(r + 1) * LANES]
            for s in range(WIDTH_A // LANES):
                lo = r * WIDTH_A + s * LANES
                nat[n, s, pl.ds(r, rows, stride=dil), :] = o_ref[:, lo:lo + LANES].astype(F32)
    s1, s2, s3 = s1_ref[...], nat_lse[0], nat_lse[1]
    mx = jnp.maximum(jnp.maximum(s1, s2), s3)
    e1, e2, e3 = jnp.exp(s1 - mx), jnp.exp(s2 - mx), jnp.exp(s3 - mx)
    inv = 1.0 / (e1 + e2 + e3)
    def expand(w):
        hi = w.astype(BF16)
        lo = (w - hi.astype(F32)).astype(BF16)
        sel = sel_ref[...]
        return (jnp.dot(hi, sel, preferred_element_type=F32) + jnp.dot(lo, sel, preferred_element_type=F32))

    w1, w2, w3 = expand(e1 * inv), expand(e2 * inv), expand(e3 * inv)
    for s in range(WIDTH_A // LANES):
        sl = slice(s * LANES, (s + 1) * LANES)
        oa = w1[:, sl] * o1_ref[:, sl].astype(F32) + w2[:, sl] * nat[0, s] + w3[:, sl] * nat[1, s]
        oa_buf[:, sl] = oa.astype(BF16)
    mixed = (jnp.dot(oa_buf[...], wa_ref[...], preferred_element_type=F32)
             + jnp.dot(ob_ref[...], wb_ref[...], preferred_element_type=F32))
    h = x_ref[...] + gate_ref[...] * mixed
    h_ref[...] = h
    ms = jnp.mean(h * h, axis=-1, keepdims=True)
    hn = h * lax.rsqrt(ms + EPS) * g_ref[...]
    hn_ref[...] = (hn * (1.0 + scale_ref[...]) + shift_ref[...]).astype(hn_ref.dtype)


def _head_select():
    src = jnp.arange(LANES)[:, None]
    head = jnp.arange(WIDTH_A)[None, :] // HEAD_DIM
    return (src == head * LSE_REP).astype(BF16)


def _out_proj(oas, lses, ob, x, gate, shift, scale, g, wa, wb):
    B, S, D = x.shape
    tm = OUT_TM
    sel = _head_select()
    half = pl.BlockSpec((None, tm, WIDTH_B), lambda b, i: (b, i, 0))
    row = pl.BlockSpec((None, tm, D), lambda b, i: (b, i, 0))
    mod = pl.BlockSpec((None, 1, D), lambda b, i: (b, 0, 0))
    const = lambda shape: pl.BlockSpec(shape, lambda b, i: (0,) * len(shape))
    res = lambda dil, w: pl.BlockSpec((None, tm // dil, dil * w), lambda b, i: (b, i, 0))
    outs = [res(1, WIDTH_A), res(4, WIDTH_A), res(16, WIDTH_A)]
    lse_specs = [res(1, LANES), res(4, LANES), res(16, LANES)]
    return pl.pallas_call(
        _out_proj_kernel,
        grid=(B, S // tm),
        in_specs=outs + lse_specs + [half, row, mod, mod, mod, const((1, D)), const(wa.shape), const(wb.shape),
                                     const(sel.shape)],
        out_specs=[row, row],
        out_shape=[jax.ShapeDtypeStruct((B, S, D), F32), jax.ShapeDtypeStruct((B, S, D), BF16)],
        scratch_shapes=[pltpu.VMEM((2, WIDTH_A // LANES, tm, LANES), F32), pltpu.VMEM((2, tm, LANES), F32),
                        pltpu.VMEM((tm, WIDTH_A), BF16)],
        compiler_params=_params("arbitrary", "arbitrary"),
        name="out_proj",
    )(*oas, *lses, ob, x, gate, shift, scale, g, wa, wb, sel)


def _ffn_kernel(hp_ref, hc_ref, hn_ref, h_ref, gate_ref, wu_ref, cw_ref, cb_ref, wd_ref, o_ref, lhs, act):
    i = pl.program_id(1)
    tm = hc_ref.shape[0]
    n = tm + 2 * FFN_HALO
    lhs[0:FFN_HALO, :] = jnp.where(i > 0, hp_ref[...], jnp.zeros_like(hp_ref))
    lhs[FFN_HALO:FFN_HALO + tm, :] = hc_ref[...]
    lhs[FFN_HALO + tm:, :] = jnp.where(i < pl.num_programs(1) - 1, hn_ref[...], jnp.zeros_like(hn_ref))
    x = lhs[...]

    def conv(u, lo):
        cw = cw_ref[:, lo:lo + FFN_FC]
        mid = slice(FFN_HALO, FFN_HALO + tm)
        below = pltpu.roll(u, 1, 0)[mid]
        above = pltpu.roll(u, n - 1, 0)[mid]
        return cw[0:1] * below + cw[1:2] * u[mid] + cw[2:3] * above + cb_ref[:, lo:lo + FFN_FC]

    for c in range(D_FF // FFN_FC):
        lo_v = c * FFN_FC
        lo_g = D_FF + c * FFN_FC
        val = conv(jnp.dot(x, wu_ref[:, lo_v:lo_v + FFN_FC], preferred_element_type=F32), lo_v)
        gt = conv(jnp.dot(x, wu_ref[:, lo_g:lo_g + FFN_FC], preferred_element_type=F32), lo_g)
        act[:, lo_v:lo_v + FFN_FC] = (gt / (1.0 + jnp.exp(-gt)) * val).astype(BF16)
    down = jnp.dot(act[...], wd_ref[...], preferred_element_type=F32)
    o_ref[...] = h_ref[...] + gate_ref[...] * down


def _ffn(hn, h, gate, wu, cw, cb, wd):
    B, S, D = h.shape
    tm = FFN_TM
    per = tm // FFN_HALO
    nh = S // FFN_HALO
    row = lambda dt: pl.BlockSpec((None, tm, D), lambda b, i: (b, i, 0))
    prev = pl.BlockSpec((None, FFN_HALO, D), lambda b, i: (b, jnp.maximum(i * per - 1, 0), 0))
    nxt = pl.BlockSpec((None, FFN_HALO, D), lambda b, i: (b, jnp.minimum((i + 1) * per, nh - 1), 0))
    mod = pl.BlockSpec((None, 1, D), lambda b, i: (b, 0, 0))
    const = lambda shape: pl.BlockSpec(shape, lambda b, i: (0,) * len(shape), pipeline_mode=pl.Buffered(1))
    return pl.pallas_call(
        _ffn_kernel,
        grid=(B, S // tm),
        in_specs=[prev, row(BF16), nxt, row(F32), mod, const(wu.shape), const(cw.shape), const(cb.shape),
                  const(wd.shape)],
        out_specs=row(F32),
        out_shape=jax.ShapeDtypeStruct((B, S, D), F32),
        scratch_shapes=[pltpu.VMEM((tm + 2 * FFN_HALO, D), BF16), pltpu.VMEM((tm, D_FF), BF16)],
        compiler_params=_params("arbitrary", "arbitrary"),
        name="conv_ffn",
    )(hn, hn, hn, h, gate, wu, cw, cb, wd)


def _block_diag_ones():
    r = jnp.arange(MXU_DIM) // HEAD_DIM
    return (r[:, None] == r[None, :]).astype(BF16)


def kernel(x, c, w_ada, b_ada, norm1_g, w_in, q_norm_a, k_norm_a, q_norm_b, k_norm_b, rel_bias, lambda_q1,
           lambda_k1, lambda_q2, lambda_k2, subln_g, w_out, norm2_g, w_up, conv_w, conv_b, w_down):
    B, S, D = x.shape
    depth = w_ada.shape[0]
    h = x.astype(F32)
    c8 = jnp.pad(c.astype(F32), ((0, 8 - B), (0, 0)))
    bd = _block_diag_ones()
    rel_bias = rel_bias.astype(F32)
    band_b, bmax_b = _diff_bias(rel_bias)
    band_a = _dil_bias(rel_bias)
    qscale = HEAD_DIM ** -0.5
    tile8 = lambda g: jnp.tile(g.astype(F32), WIDTH_A // HEAD_DIM)[None, :]
    row = lambda v: v.astype(F32)[None, :]

    for layer in range(depth):
        lambda_init = 0.8 - 0.6 * math.exp(-0.3 * layer)
        mod = _ada(c8, w_ada[layer].astype(F32), row(b_ada[layer]))[:B]
        shift1, scale1, gate1, shift2, scale2, gate2 = [m[:, None, :] for m in jnp.split(mod, 6, axis=-1)]

        qa, ka, va, qa4, ka4, va4, qa16, ka16, va16, qb, k1, k2, vbt = _in_proj(
            h, shift1, scale1, row(norm1_g[layer]), w_in[layer].astype(BF16), bd,
            tile8(q_norm_a[layer]) * (qscale * LOG2E), tile8(k_norm_a[layer]),
            tile8(q_norm_b[layer]) * (qscale * LOG2E), tile8(k_norm_b[layer]))

        oas, lses = [], []
        qkv = {1: (qa, ka, va), 4: (qa4, ka4, va4), 16: (qa16, ka16, va16)}
        for p, (_, dil) in enumerate(DILATED_PATTERNS):
            o, lse = _dilated(*qkv[dil], band_a[p], dil)
            oas.append(o)
            lses.append(lse)

        ob = _diff_attention(rel_bias, qb, k1, k2, vbt, band_b, bmax_b, row(lambda_q1[layer]), row(lambda_k1[layer]),
                             row(lambda_q2[layer]), row(lambda_k2[layer]),
                             subln_g[layer].astype(F32)[:, None], lambda_init)

        w_o = w_out[layer].astype(BF16)
        h, hn = _out_proj(oas, lses, ob, h, gate1, shift2, scale2, row(norm2_g[layer]),
                          w_o[:WIDTH_A], w_o[WIDTH_A:])
        h = _ffn(hn, h, gate2, w_up[layer].astype(BF16), conv_w[layer].astype(F32), row(conv_b[layer]),
                 w_down[layer].astype(BF16))

    return h.astype(x.dtype)
```

```python
import functools
import math

import jax
import jax.numpy as jnp
from jax import lax
from jax.experimental import pallas as pl
from jax.experimental.pallas import tpu as pltpu

F32 = jnp.float32
BF16 = jnp.bfloat16

D_MODEL = 1024
HEAD_DIM = 64
WIDTH_A = 512
WIDTH_B = 512
N_HEADS_A = 8
N_HEADS_B = 4
DILATED_PATTERNS = ((128, 1), (512, 4), (2048, 16))
D_FF = 2816
NUM_BUCKETS = 32
MAX_DISTANCE = 2048
EPS = 1e-6
NEG = -1e30
LOG2E = math.log2(math.e)

LANES = 128
BF16_SUBLANES = 16
MXU_DIM = 256
VMEM_LIMIT = 56 * 1024 * 1024

HALF_WIN = 64
DIL_Q = 128
DIL_K = DIL_Q + 2 * HALF_WIN
DIL_TL = 1024
LSE_REP = LANES // N_HEADS_A
DIL_AHEAD = 4

_MAX_EXACT = NUM_BUCKETS // 4
assert MAX_DISTANCE // _MAX_EXACT == 2 ** (NUM_BUCKETS // 2 - _MAX_EXACT)
_BUCKET_THRESHOLDS = tuple(range(1, _MAX_EXACT + 1)) + tuple(
    _MAX_EXACT * 2 ** k for k in range(1, NUM_BUCKETS // 2 - _MAX_EXACT))
BUCKET_SAT = _BUCKET_THRESHOLDS[-1]

DIFF_TQ = 512
DIFF_TK = 512
DIFF_RATIO = DIFF_TK // DIFF_TQ
DIFF_E_LO = -(BUCKET_SAT + DIFF_TK) // DIFF_TQ + 1
DIFF_E_HI = (BUCKET_SAT + DIFF_TQ) // DIFF_TQ - 1
DIFF_NT = DIFF_E_HI - DIFF_E_LO + 3
DIFF_GROUP = 8

PROJ_TM = 512
PROJ_GROUP = WIDTH_A
ADA_TN = 1024
OUT_TM = 1024
FFN_TM = 1024
FFN_FC = 256
FFN_HALO = BF16_SUBLANES


def _params(*sem):
    return pltpu.CompilerParams(dimension_semantics=sem, vmem_limit_bytes=VMEM_LIMIT)


def _ada_kernel(c_ref, w_ref, b_ref, o_ref):
    c = c_ref[...]
    ca = c / (1.0 + jnp.exp(-c))
    o_ref[...] = jnp.dot(ca, w_ref[...], preferred_element_type=F32) + b_ref[...]


def _ada(c8, w, b):
    n = w.shape[1]
    tn = ADA_TN
    return pl.pallas_call(
        _ada_kernel,
        grid=(n // tn,),
        in_specs=[pl.BlockSpec((8, D_MODEL), lambda j: (0, 0)),
                  pl.BlockSpec((D_MODEL, tn), lambda j: (0, j)),
                  pl.BlockSpec((1, tn), lambda j: (0, j))],
        out_specs=pl.BlockSpec((8, tn), lambda j: (0, j)),
        out_shape=jax.ShapeDtypeStruct((8, n), F32),
        compiler_params=_params("arbitrary"),
        name="ada",
    )(c8, w, b)


def _head_norm(p, bd, gain):
    ss = jnp.dot((p * p).astype(BF16), bd, preferred_element_type=F32)
    return p * lax.rsqrt(ss * (1.0 / HEAD_DIM) + EPS) * gain


def _in_proj_kernel(x_ref, shift_ref, scale_ref, g_ref, w_ref, bd_ref, gqa_ref, gka_ref, gqb_ref, gkb_ref,
                    qa_ref, ka_ref, va_ref, qa4_ref, ka4_ref, va4_ref, qa16_ref, ka16_ref, va16_ref,
                    qb_ref, k1_ref, k2_ref, vbt_ref, pbuf, stage, stage4):
    tm = x_ref.shape[0]
    x = x_ref[...]
    ms = jnp.mean(x * x, axis=-1, keepdims=True)
    hn = x * lax.rsqrt(ms + EPS) * g_ref[...]
    hn = (hn * (1.0 + scale_ref[...]) + shift_ref[...]).astype(BF16)
    pbuf[...] = jnp.dot(hn, w_ref[...], preferred_element_type=F32)
    bd = bd_ref[...]
    lane = lax.broadcasted_iota(jnp.int32, (1, MXU_DIM), 1)
    first = (lane // HEAD_DIM) % 2 == 0

    def proj(g, c):
        lo = g * PROJ_GROUP + c * MXU_DIM
        return pbuf[:, lo:lo + MXU_DIM]

    for c in range(2):
        sl = slice(c * MXU_DIM, (c + 1) * MXU_DIM)
        group_a = (_head_norm(proj(0, c), bd, gqa_ref[:, sl]), _head_norm(proj(1, c), bd, gka_ref[:, sl]),
                   proj(2, c))
        for a, (val, ref) in enumerate(zip(group_a, (qa_ref, ka_ref, va_ref))):
            ref[:, sl] = val.astype(BF16)
            stage[a, 2 * c] = val[:, :LANES]
            stage[a, 2 * c + 1] = val[:, LANES:]
        qb_ref[:, sl] = _head_norm(proj(3, c), bd, gqb_ref[:, sl]).astype(BF16)
        kb = _head_norm(proj(4, c), bd, gkb_ref[:, sl])
        k1_ref[:, sl] = jnp.where(first, kb, 0.0).astype(BF16)
        k2_ref[:, sl] = jnp.where(first, 0.0, kb).astype(BF16)
        vbt_ref[sl, :] = proj(5, c).T.astype(BF16)

    for a, (ref4, ref16) in enumerate(((qa4_ref, qa16_ref), (ka4_ref, ka16_ref), (va4_ref, va16_ref))):
        for lo_r in range(4):
            for s in range(WIDTH_A // LANES):
                part = stage[a, s, pl.ds(lo_r, tm // 4, stride=4), :]
                stage4[a, lo_r, s] = part
                ref4[:, lo_r * WIDTH_A + s * LANES:lo_r * WIDTH_A + (s + 1) * LANES] = part.astype(BF16)
        for hi_r in range(4):
            for lo_r in range(4):
                for s in range(WIDTH_A // LANES):
                    lo = (4 * hi_r + lo_r) * WIDTH_A + s * LANES
                    part = stage4[a, lo_r, s, pl.ds(hi_r, tm // 16, stride=4), :]
                    ref16[:, lo:lo + LANES] = part.astype(BF16)


def _in_proj(x, shift, scale, g, w_bf, bd, gqa, gka, gqb, gkb):
    B, S, D = x.shape
    tm = PROJ_TM
    row = pl.BlockSpec((None, tm, D), lambda b, i: (b, i, 0))
    mod = pl.BlockSpec((None, 1, D), lambda b, i: (b, 0, 0))
    const = lambda shape: pl.BlockSpec(shape, lambda b, i: (0,) * len(shape))
    out = pl.BlockSpec((None, tm, PROJ_GROUP), lambda b, i: (b, i, 0))
    out_t = pl.BlockSpec((None, PROJ_GROUP, tm), lambda b, i: (b, 0, i))
    res = lambda dil: pl.BlockSpec((None, tm // dil, dil * WIDTH_A), lambda b, i: (b, i, 0))
    res_shape = lambda dil: jax.ShapeDtypeStruct((B, S // dil, dil * WIDTH_A), BF16)
    nat_shape = jax.ShapeDtypeStruct((B, S, PROJ_GROUP), BF16)
    gain = const((1, PROJ_GROUP))
    return pl.pallas_call(
        _in_proj_kernel,
        grid=(B, S // tm),
        in_specs=[row, mod, mod, const((1, D)), const(w_bf.shape), const(bd.shape),
                  gain, gain, gain, gain],
        out_specs=[out] * 3 + [res(4)] * 3 + [res(16)] * 3 + [out] * 3 + [out_t],
        out_shape=[nat_shape] * 3 + [res_shape(4)] * 3 + [res_shape(16)] * 3 + [nat_shape] * 3
                  + [jax.ShapeDtypeStruct((B, PROJ_GROUP, S), BF16)],
        scratch_shapes=[pltpu.VMEM((tm, w_bf.shape[1]), F32),
                        pltpu.VMEM((3, WIDTH_A // LANES, tm, LANES), F32),
                        pltpu.VMEM((3, 4, WIDTH_A // LANES, tm // 4, LANES), F32)],
        compiler_params=_params("arbitrary", "arbitrary"),
        name="in_proj",
    )(x, shift, scale, g, w_bf, bd, gqa, gka, gqb, gkb)


def _bias_from_rel(rel, tbl_ref, h):
    n = jnp.abs(rel)
    vneg = jnp.full(rel.shape, tbl_ref[0, h], F32)
    vpos = jnp.full(rel.shape, tbl_ref[NUM_BUCKETS // 2, h], F32)
    for i, th in enumerate(_BUCKET_THRESHOLDS):
        ge = n >= th
        vneg = jnp.where(ge, tbl_ref[i + 1, h], vneg)
        vpos = jnp.where(ge, tbl_ref[NUM_BUCKETS // 2 + i + 1, h], vpos)
    return jnp.where(rel > 0, vpos, vneg)


def _diff_bias_kernel(tbl_ref, o_ref, omax_ref, sub):
    h = pl.program_id(0)
    nk, nq = DIFF_TK // LANES, DIFF_TQ // LANES
    base = (DIFF_E_LO - 1) * nq
    lo_u = base - (nq - 1)
    n_sub = (DIFF_NT - 1) * nq + nk + nq - 1
    k = lax.broadcasted_iota(jnp.int32, (LANES, LANES), 0)
    q = lax.broadcasted_iota(jnp.int32, (LANES, LANES), 1)
    def one_offset(n, carry):
        sub[n] = _bias_from_rel(k - q + (lo_u + n) * LANES, tbl_ref, N_HEADS_A + h) * LOG2E
        return carry

    lax.fori_loop(0, n_sub, one_offset, 0)
    for t in range(DIFF_NT):
        for b in range(nq):
            cols = slice(b * LANES, (b + 1) * LANES)
            for a in range(nk):
                o_ref[t, a * LANES:(a + 1) * LANES, cols] = sub[base + t * nq + a - b - lo_u]
            omax_ref[t, :, cols] = jnp.max(o_ref[t, :, cols], axis=0, keepdims=True)


def _diff_bias(rel_bias):
    n_sub = (DIFF_NT - 1) * (DIFF_TQ // LANES) + DIFF_TK // LANES + DIFF_TQ // LANES - 1
    return pl.pallas_call(
        _diff_bias_kernel,
        grid=(N_HEADS_B,),
        in_specs=[pl.BlockSpec(memory_space=pltpu.SMEM)],
        out_specs=[pl.BlockSpec((None, DIFF_NT, DIFF_TK, DIFF_TQ), lambda h: (h, 0, 0, 0)),
                   pl.BlockSpec((None, DIFF_NT, 1, DIFF_TQ), lambda h: (h, 0, 0, 0))],
        out_shape=[jax.ShapeDtypeStruct((N_HEADS_B, DIFF_NT, DIFF_TK, DIFF_TQ), F32),
                   jax.ShapeDtypeStruct((N_HEADS_B, DIFF_NT, 1, DIFF_TQ), F32)],
        scratch_shapes=[pltpu.VMEM((n_sub, LANES, LANES), F32)],
        compiler_params=_params("arbitrary"),
        name="diff_bias",
    )(rel_bias)


def _dil_bias_kernel(tbl_ref, o_ref):
    p = pl.program_id(0)
    hp = pl.program_id(1)
    shape = (DIL_K, DIL_Q)
    k = lax.broadcasted_iota(jnp.int32, shape, 0)
    q = lax.broadcasted_iota(jnp.int32, shape, 1)
    steps = k - HALF_WIN - q
    dil = lax.shift_left(jnp.int32(1), 2 * p)
    inside = jnp.abs(steps) <= HALF_WIN
    for par in range(2):
        bias = jnp.where(inside, _bias_from_rel(steps * dil, tbl_ref, 2 * hp + par) * LOG2E, NEG)
        ls = slice(par * DIL_Q, (par + 1) * DIL_Q)
        o_ref[0, :, ls] = bias
        o_ref[1, :, ls] = jnp.where(k >= HALF_WIN, bias, NEG)
        o_ref[2, :, ls] = jnp.where(k < DIL_K - HALF_WIN, bias, NEG)


def _dil_bias(rel_bias):
    n_pat = len(DILATED_PATTERNS)
    n_pair = N_HEADS_A // 2
    return pl.pallas_call(
        _dil_bias_kernel,
        grid=(n_pat, n_pair),
        in_specs=[pl.BlockSpec(memory_space=pltpu.SMEM)],
        out_specs=pl.BlockSpec((None, 3, None, DIL_K, 2 * DIL_Q), lambda p, hp: (p, 0, hp, 0, 0)),
        out_shape=jax.ShapeDtypeStruct((n_pat, 3, n_pair, DIL_K, 2 * DIL_Q), F32),
        compiler_params=_params("arbitrary", "arbitrary"),
        name="dil_bias",
    )(rel_bias)


def _dil_kernel(q_ref, kp_ref, kc_ref, kn_ref, vp_ref, vc_ref, vn_ref, bias_ref, o_ref, lse_ref, kbuf, vtbuf):
    tl = q_ref.shape[0]
    i = pl.program_id(2)
    n_sub = tl // DIL_Q
    n_pair = N_HEADS_A // 2
    kbuf[0:HALF_WIN, :] = kp_ref[...]
    kbuf[HALF_WIN:HALF_WIN + tl, :] = kc_ref[...]
    kbuf[HALF_WIN + tl:, :] = kn_ref[...]
    for hp in range(n_pair):
        ls = slice(hp * LANES, (hp + 1) * LANES)
        v = jnp.concatenate([vp_ref[:, ls], vc_ref[:, ls], vn_ref[:, ls]], axis=0)
        vtbuf[hp] = v.astype(F32).T.astype(BF16)
    lane = lax.broadcasted_iota(jnp.int32, (1, LANES), 1)
    even = lane < HEAD_DIM
    contract_last = (((1,), (1,)), ((), ()))
    first_step = i == 0
    last_step = i == pl.num_programs(2) - 1

    def scores(sb, hp):
        r0 = sb * DIL_Q
        ls = slice(hp * LANES, (hp + 1) * LANES)
        if sb == 0:
            variant = jnp.where(first_step, 1, 0)
        elif sb == n_sub - 1:
            variant = jnp.where(last_step, 2, 0)
        else:
            variant = 0
        q = q_ref[r0:r0 + DIL_Q, ls]
        zero = jnp.zeros_like(q)
        qcat = jnp.concatenate([jnp.where(even, q, zero), jnp.where(even, zero, q)], axis=0)
        s = lax.dot_general(kbuf[r0:r0 + DIL_K, ls], qcat, contract_last, preferred_element_type=F32)
        return s + bias_ref[variant, hp]

    def finish(sb, hp, s):
        r0 = sb * DIL_Q
        ls = slice(hp * LANES, (hp + 1) * LANES)
        m = jnp.max(s, axis=0, keepdims=True)
        p = jnp.exp2(s - m)
        l = jnp.sum(p, axis=0, keepdims=True)
        ot = jnp.dot(vtbuf[hp, :, r0:r0 + DIL_K], p.astype(BF16), preferred_element_type=F32) / l
        ot = jnp.concatenate([ot[:HEAD_DIM, :DIL_Q], ot[HEAD_DIM:, DIL_Q:]], axis=0)
        o_ref[r0:r0 + DIL_Q, ls] = ot.T.astype(o_ref.dtype)
        lse = (m + jnp.log2(l)) * (1.0 / LOG2E)
        lse_rows.extend([lse[:, :DIL_Q], lse[:, DIL_Q:]])
        if hp == n_pair - 1:
            tile = jnp.concatenate([jnp.broadcast_to(row, (LSE_REP, DIL_Q)) for row in lse_rows], axis=0)
            lse_ref[r0:r0 + DIL_Q, :] = tile.T
            lse_rows.clear()

    lse_rows = []
    bodies = [(sb, hp) for sb in range(n_sub) for hp in range(n_pair)]
    staged = [scores(*b) for b in bodies[:DIL_AHEAD]]
    for n, b in enumerate(bodies):
        if n + DIL_AHEAD < len(bodies):
            staged.append(scores(*bodies[n + DIL_AHEAD]))
        finish(*b, staged[n])


def _dilated(q, k, v, bias, dil):
    B, L, _ = q.shape
    W = WIDTH_A
    lse_spec = pl.BlockSpec((None, min(DIL_TL, L), LANES), lambda b, r, i: (b, i, r))
    tl = min(DIL_TL, L)
    nhalo = L // HALF_WIN
    cur = pl.BlockSpec((None, tl, W), lambda b, r, i: (b, i, r))
    prev = pl.BlockSpec((None, HALF_WIN, W),
                        lambda b, r, i: (b, jnp.maximum(i * (tl // HALF_WIN) - 1, 0), r))
    nxt = pl.BlockSpec((None, HALF_WIN, W),
                       lambda b, r, i: (b, jnp.minimum((i + 1) * (tl // HALF_WIN), nhalo - 1), r))
    bias_spec = pl.BlockSpec(bias.shape, lambda b, r, i: (0,) * bias.ndim)
    return pl.pallas_call(
        _dil_kernel,
        grid=(B, dil, L // tl),
        in_specs=[cur, prev, cur, nxt, prev, cur, nxt, bias_spec],
        out_specs=[cur, lse_spec],
        out_shape=[jax.ShapeDtypeStruct((B, L, dil * W), BF16), jax.ShapeDtypeStruct((B, L, dil * LANES), F32)],
        scratch_shapes=[pltpu.VMEM((tl + 2 * HALF_WIN, W), BF16),
                        pltpu.VMEM((N_HEADS_A // 2, LANES, tl + 2 * HALF_WIN), BF16)],
        compiler_params=_params("arbitrary", "arbitrary", "arbitrary"),
        name=f"dilated_{dil}",
    )(q, k, k, k, v, v, v, bias)


def _diff_kernel(tbl_ref, q_ref, qn_ref, k1_ref, k2_ref, vt_ref, band_ref, bmax_ref, lq1_ref, lk1_ref, lq2_ref,
                 lk2_ref, g_ref, o_ref, m1, l1, a1, m2, l2, a2, sa, sb, mxa, mxb, *, lambda_init):
    h = pl.program_id(1)
    i = pl.program_id(2)
    T = DIFF_TK
    nk = k1_ref.shape[0] // T
    streams = ((k1_ref, m1, l1, a1), (k2_ref, m2, l2, a2))
    for _, m, l, a in streams:
        m[...] = jnp.full(m.shape, NEG, F32)
        l[...] = jnp.zeros(l.shape, F32)
        a[...] = jnp.zeros(a.shape, F32)
    q = q_ref[...]
    contract_last = (((1,), (1,)), ((), ()))

    def scores(j, sbuf, mxbuf, st, wraps=False):
        if wraps:
            wrap = j == nk
            j = jnp.where(wrap, 0, j)
            qq = jnp.where(wrap, qn_ref[...], q)
        else:
            qq = q
        r0 = pl.multiple_of(j * T, T)
        s = lax.dot_general(streams[st][0][pl.ds(r0, T), :], qq, contract_last, preferred_element_type=F32)
        sbuf[st] = s
        mxbuf[st] = jnp.max(s, axis=0, keepdims=True)

    def softmax_pv(j, sbuf, mxbuf, st, near, shift):
        r0 = pl.multiple_of(j * T, T)
        _, m, l, a = streams[st]
        s = sbuf[st]
        if near:
            t = jnp.clip(DIFF_RATIO * j - i, DIFF_E_LO - 1, DIFF_E_HI + 1) - (DIFF_E_LO - 1)
            s = s + band_ref[t]
            bound = mxbuf[st] + bmax_ref[t]
        else:
            bound = mxbuf[st] + shift
        m_prev = m[...]
        m_new = jnp.maximum(m_prev, bound)
        alpha = jnp.exp2(m_prev - m_new)
        p = jnp.exp2(s - (m_new - shift))
        l[...] = alpha * l[...] + jnp.sum(p, axis=0, keepdims=True)
        a[...] = alpha * a[...] + jnp.dot(vt_ref[:, pl.ds(r0, T)], p.astype(BF16), preferred_element_type=F32)
        m[...] = m_new

    def group(near, shift):
        def body(g, carry):
            j0 = DIFF_GROUP * g
            bufs = ((sa, mxa), (sb, mxb))
            for u in range(DIFF_GROUP):
                cur, nxt = bufs[u % 2], bufs[(u + 1) % 2]
                for st in range(2):
                    scores(j0 + u + 1, *nxt, st, wraps=u == DIFF_GROUP - 1)
                    softmax_pv(j0 + u, *cur, st, near, shift)
            return carry
        return body

    @pl.when(i == 0)
    def _():
        scores(0, sa, mxa, 0)
        scores(0, sa, mxa, 1)

    j_lo = (i + DIFF_E_LO + DIFF_RATIO - 1) // DIFF_RATIO
    j_hi = (i + DIFF_E_HI) // DIFF_RATIO
    lo = jnp.maximum(j_lo // DIFF_GROUP, 0)
    hi = jnp.minimum(j_hi // DIFF_GROUP + 1, nk // DIFF_GROUP)
    c_neg = tbl_ref[NUM_BUCKETS // 2 - 1, N_HEADS_A + h] * LOG2E
    c_pos = tbl_ref[NUM_BUCKETS - 1, N_HEADS_A + h] * LOG2E
    lax.fori_loop(0, lo, group(False, c_neg), 0)
    lax.fori_loop(lo, hi, group(True, 0.0), 0)
    lax.fori_loop(hi, nk // DIFF_GROUP, group(False, c_pos), 0)

    lam = (jnp.exp(jnp.sum(lq1_ref[...] * lk1_ref[...], axis=-1, keepdims=True))
           - jnp.exp(jnp.sum(lq2_ref[...] * lk2_ref[...], axis=-1, keepdims=True)) + lambda_init)
    o = a1[...] / l1[...] - lam * (a2[...] / l2[...])
    ms = jnp.mean(o * o, axis=0, keepdims=True)
    o = o * lax.rsqrt(ms + EPS) * (g_ref[...] * (1.0 - lambda_init))
    o_ref[...] = o.T.astype(o_ref.dtype)


def _diff_attention(rel_bias, qb, k1, k2, vbt, band, bmax, lq1, lk1, lq2, lk2, g_col, lambda_init):
    B, S, W = qb.shape
    T, TK = DIFF_TQ, DIFF_TK
    qspec = pl.BlockSpec((None, T, LANES), lambda b, h, i: (b, i, h))
    qnext = pl.BlockSpec((None, T, LANES), lambda b, h, i: (b, jnp.minimum(i + 1, S // T - 1), h))
    kspec = pl.BlockSpec((None, S, LANES), lambda b, h, i: (b, 0, h))
    vspec = pl.BlockSpec((None, LANES, S), lambda b, h, i: (b, h, 0))
    band_spec = pl.BlockSpec((None, DIFF_NT, TK, T), lambda b, h, i: (h, 0, 0, 0))
    bmax_spec = pl.BlockSpec((None, DIFF_NT, 1, T), lambda b, h, i: (h, 0, 0, 0))
    vec = lambda n: pl.BlockSpec((1, n), lambda b, h, i: (0, 0))
    stat = pltpu.VMEM((1, T), F32)
    acc = pltpu.VMEM((LANES, T), F32)
    return pl.pallas_call(
        functools.partial(_diff_kernel, lambda_init=lambda_init),
        grid=(B, N_HEADS_B, S // T),
        in_specs=[pl.BlockSpec(memory_space=pltpu.SMEM), qspec, qnext, kspec, kspec, vspec, band_spec, bmax_spec,
                  vec(HEAD_DIM), vec(HEAD_DIM), vec(HEAD_DIM), vec(HEAD_DIM),
                  pl.BlockSpec((LANES, 1), lambda b, h, i: (0, 0))],
        out_specs=qspec,
        out_shape=jax.ShapeDtypeStruct((B, S, W), BF16),
        scratch_shapes=[stat, stat, acc, stat, stat, acc, pltpu.VMEM((2, TK, T), F32), pltpu.VMEM((2, TK, T), F32),
                        pltpu.VMEM((2, 1, T), F32), pltpu.VMEM((2, 1, T), F32)],
        compiler_params=_params("arbitrary", "arbitrary", "arbitrary"),
        name="diff_attn",
    )(rel_bias, qb, qb, k1, k2, vbt, band, bmax, lq1, lk1, lq2, lk2, g_col)


def _out_proj_kernel(o1_ref, o4_ref, o16_ref, s1_ref, s4_ref, s16_ref, ob_ref, x_ref, gate_ref, shift_ref,
                     scale_ref, g_ref, wa_ref, wb_ref, sel_ref, h_ref, hn_ref, nat, nat_lse, oa_buf):
    tm = x_ref.shape[0]
    for n, (dil, o_ref, s_ref) in enumerate(((4, o4_ref, s4_ref), (16, o16_ref, s16_ref))):
        rows = tm // dil
        for r in range(dil):
            nat_lse[n, pl.ds(r, rows, stride=dil), :] = s_ref[:, r * LANES:(r + 1) * LANES]
            for s in range(WIDTH_A // LANES):
                lo = r * WIDTH_A + s * LANES
                nat[n, s, pl.ds(r, rows, stride=dil), :] = o_ref[:, lo:lo + LANES].astype(F32)
    s1, s2, s3 = s1_ref[...], nat_lse[0], nat_lse[1]
    mx = jnp.maximum(jnp.maximum(s1, s2), s3)
    e1, e2, e3 = jnp.exp(s1 - mx), jnp.exp(s2 - mx), jnp.exp(s3 - mx)
    inv = 1.0 / (e1 + e2 + e3)
    def expand(w):
        hi = w.astype(BF16)
        lo = (w - hi.astype(F32)).astype(BF16)
        sel = sel_ref[...]
        return (jnp.dot(hi, sel, preferred_element_type=F32) + jnp.dot(lo, sel, preferred_element_type=F32))

    w1, w2, w3 = expand(e1 * inv), expand(e2 * inv), expand(e3 * inv)
    for s in range(WIDTH_A // LANES):
        sl = slice(s * LANES, (s + 1) * LANES)
        oa = w1[:, sl] * o1_ref[:, sl].astype(F32) + w2[:, sl] * nat[0, s] + w3[:, sl] * nat[1, s]
        oa_buf[:, sl] = oa.astype(BF16)
    mixed = (jnp.dot(oa_buf[...], wa_ref[...], preferred_element_type=F32)
             + jnp.dot(ob_ref[...], wb_ref[...], preferred_element_type=F32))
    h = x_ref[...] + gate_ref[...] * mixed
    h_ref[...] = h
    ms = jnp.mean(h * h, axis=-1, keepdims=True)
    hn = h * lax.rsqrt(ms + EPS) * g_ref[...]
    hn_ref[...] = (hn * (1.0 + scale_ref[...]) + shift_ref[...]).astype(hn_ref.dtype)


def _head_select():
    src = jnp.arange(LANES)[:, None]
    head = jnp.arange(WIDTH_A)[None, :] // HEAD_DIM
    return (src == head * LSE_REP).astype(BF16)


def _out_proj(oas, lses, ob, x, gate, shift, scale, g, wa, wb):
    B, S, D = x.shape
    tm = OUT_TM
    sel = _head_select()
    half = pl.BlockSpec((None, tm, WIDTH_B), lambda b, i: (b, i, 0))
    row = pl.BlockSpec((None, tm, D), lambda b, i: (b, i, 0))
    mod = pl.BlockSpec((None, 1, D), lambda b, i: (b, 0, 0))
    const = lambda shape: pl.BlockSpec(shape, lambda b, i: (0,) * len(shape))
    res = lambda dil, w: pl.BlockSpec((None, tm // dil, dil * w), lambda b, i: (b, i, 0))
    outs = [res(1, WIDTH_A), res(4, WIDTH_A), res(16, WIDTH_A)]
    lse_specs = [res(1, LANES), res(4, LANES), res(16, LANES)]
    return pl.pallas_call(
        _out_proj_kernel,
        grid=(B, S // tm),
        in_specs=outs + lse_specs + [half, row, mod, mod, mod, const((1, D)), const(wa.shape), const(wb.shape),
                                     const(sel.shape)],
        out_specs=[row, row],
        out_shape=[jax.ShapeDtypeStruct((B, S, D), F32), jax.ShapeDtypeStruct((B, S, D), BF16)],
        scratch_shapes=[pltpu.VMEM((2, WIDTH_A // LANES, tm, LANES), F32), pltpu.VMEM((2, tm, LANES), F32),
                        pltpu.VMEM((tm, WIDTH_A), BF16)],
        compiler_params=_params("arbitrary", "arbitrary"),
        name="out_proj",
    )(*oas, *lses, ob, x, gate, shift, scale, g, wa, wb, sel)


def _ffn_kernel(hp_ref, hc_ref, hn_ref, h_ref, gate_ref, wu_ref, cw_ref, cb_ref, wd_ref, o_ref, lhs, act):
    i = pl.program_id(1)
    tm = hc_ref.shape[0]
    n = tm + 2 * FFN_HALO
    lhs[0:FFN_HALO, :] = jnp.where(i > 0, hp_ref[...], jnp.zeros_like(hp_ref))
    lhs[FFN_HALO:FFN_HALO + tm, :] = hc_ref[...]
    lhs[FFN_HALO + tm:, :] = jnp.where(i < pl.num_programs(1) - 1, hn_ref[...], jnp.zeros_like(hn_ref))
    x = lhs[...]

    def conv(u, lo):
        cw = cw_ref[:, lo:lo + FFN_FC]
        mid = slice(FFN_HALO, FFN_HALO + tm)
        below = pltpu.roll(u, 1, 0)[mid]
        above = pltpu.roll(u, n - 1, 0)[mid]
        return cw[0:1] * below + cw[1:2] * u[mid] + cw[2:3] * above + cb_ref[:, lo:lo + FFN_FC]

    for c in range(D_FF // FFN_FC):
        lo_v = c * FFN_FC
        lo_g = D_FF + c * FFN_FC
        val = conv(jnp.dot(x, wu_ref[:, lo_v:lo_v + FFN_FC], preferred_element_type=F32), lo_v)
        gt = conv(jnp.dot(x, wu_ref[:, lo_g:lo_g + FFN_FC], preferred_element_type=F32), lo_g)
        act[:, lo_v:lo_v + FFN_FC] = (gt / (1.0 + jnp.exp(-gt)) * val).astype(BF16)
    down = jnp.dot(act[...], wd_ref[...], preferred_element_type=F32)
    o_ref[...] = h_ref[...] + gate_ref[...] * down


def _ffn(hn, h, gate, wu, cw, cb, wd):
    B, S, D = h.shape
    tm = FFN_TM
    per = tm // FFN_HALO
    nh = S // FFN_HALO
    row = lambda dt: pl.BlockSpec((None, tm, D), lambda b, i: (b, i, 0))
    prev = pl.BlockSpec((None, FFN_HALO, D), lambda b, i: (b, jnp.maximum(i * per - 1, 0), 0))
    nxt = pl.BlockSpec((None, FFN_HALO, D), lambda b, i: (b, jnp.minimum((i + 1) * per, nh - 1), 0))
    mod = pl.BlockSpec((None, 1, D), lambda b, i: (b, 0, 0))
    const = lambda shape: pl.BlockSpec(shape, lambda b, i: (0,) * len(shape), pipeline_mode=pl.Buffered(1))
    return pl.pallas_call(
        _ffn_kernel,
        grid=(B, S // tm),
        in_specs=[prev, row(BF16), nxt, row(F32), mod, const(wu.shape), const(cw.shape), const(cb.shape),
                  const(wd.shape)],
        out_specs=row(F32),
        out_shape=jax.ShapeDtypeStruct((B, S, D), F32),
        scratch_shapes=[pltpu.VMEM((tm + 2 * FFN_HALO, D), BF16), pltpu.VMEM((tm, D_FF), BF16)],
        compiler_params=_params("arbitrary", "arbitrary"),
        name="conv_ffn",
    )(hn, hn, hn, h, gate, wu, cw, cb, wd)


def _block_diag_ones():
    r = jnp.arange(MXU_DIM) // HEAD_DIM
    return (r[:, None] == r[None, :]).astype(BF16)


def kernel(x, c, w_ada, b_ada, norm1_g, w_in, q_norm_a, k_norm_a, q_norm_b, k_norm_b, rel_bias, lambda_q1,
           lambda_k1, lambda_q2, lambda_k2, subln_g, w_out, norm2_g, w_up, conv_w, conv_b, w_down):
    B, S, D = x.shape
    depth = w_ada.shape[0]
    h = x.astype(F32)
    c8 = jnp.pad(c.astype(F32), ((0, 8 - B), (0, 0)))
    bd = _block_diag_ones()
    rel_bias = rel_bias.astype(F32)
    band_b, bmax_b = _diff_bias(rel_bias)
    band_a = _dil_bias(rel_bias)
    qscale = HEAD_DIM ** -0.5
    tile8 = lambda g: jnp.tile(g.astype(F32), WIDTH_A // HEAD_DIM)[None, :]
    row = lambda v: v.astype(F32)[None, :]

    for layer in range(depth):
        lambda_init = 0.8 - 0.6 * math.exp(-0.3 * layer)
        mod = _ada(c8, w_ada[layer].astype(F32), row(b_ada[layer]))[:B]
        shift1, scale1, gate1, shift2, scale2, gate2 = [m[:, None, :] for m in jnp.split(mod, 6, axis=-1)]

        qa, ka, va, qa4, ka4, va4, qa16, ka16, va16, qb, k1, k2, vbt = _in_proj(
            h, shift1, scale1, row(norm1_g[layer]), w_in[layer].astype(BF16), bd,
            tile8(q_norm_a[layer]) * (qscale * LOG2E), tile8(k_norm_a[layer]),
            tile8(q_norm_b[layer]) * (qscale * LOG2E), tile8(k_norm_b[layer]))

        oas, lses = [], []
        qkv = {1: (qa, ka, va), 4: (qa4, ka4, va4), 16: (qa16, ka16, va16)}
        for p, (_, dil) in enumerate(DILATED_PATTERNS):
            o, lse = _dilated(*qkv[dil], band_a[p], dil)
            oas.append(o)
            lses.append(lse)

        ob = _diff_attention(rel_bias, qb, k1, k2, vbt, band_b, bmax_b, row(lambda_q1[layer]), row(lambda_k1[layer]),
                             row(lambda_q2[layer]), row(lambda_k2[layer]),
                             subln_g[layer].astype(F32)[:, None], lambda_init)

        w_o = w_out[layer].astype(BF16)
        h, hn = _out_proj(oas, lses, ob, h, gate1, shift2, scale2, row(norm2_g[layer]),
                          w_o[:WIDTH_A], w_o[WIDTH_A:])
        h = _ffn(hn, h, gate2, w_up[layer].astype(BF16), conv_w[layer].astype(F32), row(conv_b[layer]),
                 w_down[layer].astype(BF16))

    return h.astype(x.dtype)
```

```python
import functools
import math

import jax
import jax.numpy as jnp
from jax import lax
from jax.experimental import pallas as pl
from jax.experimental.pallas import tpu as pltpu

F32 = jnp.float32
BF16 = jnp.bfloat16

D_MODEL = 1024
HEAD_DIM = 64
WIDTH_A = 512
WIDTH_B = 512
N_HEADS_A = 8
N_HEADS_B = 4
DILATED_PATTERNS = ((128, 1), (512, 4), (2048, 16))
D_FF = 2816
NUM_BUCKETS = 32
MAX_DISTANCE = 2048
EPS = 1e-6
NEG = -1e30
LOG2E = math.log2(math.e)

LANES = 128
BF16_SUBLANES = 16
MXU_DIM = 256
VMEM_LIMIT = 56 * 1024 * 1024

HALF_WIN = 64
DIL_Q = 128
DIL_K = DIL_Q + 2 * HALF_WIN
DIL_TL = 1024
LSE_REP = LANES // N_HEADS_A
DIL_AHEAD = 4

_MAX_EXACT = NUM_BUCKETS // 4
assert MAX_DISTANCE // _MAX_EXACT == 2 ** (NUM_BUCKETS // 2 - _MAX_EXACT)
_BUCKET_THRESHOLDS = tuple(range(1, _MAX_EXACT + 1)) + tuple(
    _MAX_EXACT * 2 ** k for k in range(1, NUM_BUCKETS // 2 - _MAX_EXACT))
BUCKET_SAT = _BUCKET_THRESHOLDS[-1]

DIFF_TQ = 512
DIFF_TK = 512
DIFF_RATIO = DIFF_TK // DIFF_TQ
DIFF_E_LO = -(BUCKET_SAT + DIFF_TK) // DIFF_TQ + 1
DIFF_E_HI = (BUCKET_SAT + DIFF_TQ) // DIFF_TQ - 1
DIFF_NT = DIFF_E_HI - DIFF_E_LO + 3
DIFF_GROUP = 8

PROJ_TM = 512
PROJ_GROUP = WIDTH_A
ADA_TN = 1024
OUT_TM = 1024
FFN_TM = 1024
FFN_FC = 256
FFN_HALO = BF16_SUBLANES


def _params(*sem):
    return pltpu.CompilerParams(dimension_semantics=sem, vmem_limit_bytes=VMEM_LIMIT)


def _ada_kernel(c_ref, w_ref, b_ref, o_ref):
    c = c_ref[...]
    ca = c / (1.0 + jnp.exp(-c))
    o_ref[...] = jnp.dot(ca, w_ref[...], preferred_element_type=F32) + b_ref[...]


def _ada(c8, w, b):
    n = w.shape[1]
    tn = ADA_TN
    return pl.pallas_call(
        _ada_kernel,
        grid=(n // tn,),
        in_specs=[pl.BlockSpec((8, D_MODEL), lambda j: (0, 0)),
                  pl.BlockSpec((D_MODEL, tn), lambda j: (0, j)),
                  pl.BlockSpec((1, tn), lambda j: (0, j))],
        out_specs=pl.BlockSpec((8, tn), lambda j: (0, j)),
        out_shape=jax.ShapeDtypeStruct((8, n), F32),
        compiler_params=_params("arbitrary"),
        name="ada",
    )(c8, w, b)


def _head_norm(p, bd, gain):
    ss = jnp.dot((p * p).astype(BF16), bd, preferred_element_type=F32)
    return p * lax.rsqrt(ss * (1.0 / HEAD_DIM) + EPS) * gain


def _in_proj_kernel(x_ref, shift_ref, scale_ref, g_ref, w_ref, bd_ref, gqa_ref, gka_ref, gqb_ref, gkb_ref,
                    qa_ref, ka_ref, va_ref, qa4_ref, ka4_ref, va4_ref, qa16_ref, ka16_ref, va16_ref,
                    qb_ref, k1_ref, k2_ref, vbt_ref, pbuf, stage, stage4):
    tm = x_ref.shape[0]
    x = x_ref[...]
    ms = jnp.mean(x * x, axis=-1, keepdims=True)
    hn = x * lax.rsqrt(ms + EPS) * g_ref[...]
    hn = (hn * (1.0 + scale_ref[...]) + shift_ref[...]).astype(BF16)
    pbuf[...] = jnp.dot(hn, w_ref[...], preferred_element_type=F32)
    bd = bd_ref[...]
    lane = lax.broadcasted_iota(jnp.int32, (1, MXU_DIM), 1)
    first = (lane // HEAD_DIM) % 2 == 0

    def proj(g, c):
        lo = g * PROJ_GROUP + c * MXU_DIM
        return pbuf[:, lo:lo + MXU_DIM]

    for c in range(2):
        sl = slice(c * MXU_DIM, (c + 1) * MXU_DIM)
        group_a = (_head_norm(proj(0, c), bd, gqa_ref[:, sl]), _head_norm(proj(1, c), bd, gka_ref[:, sl]),
                   proj(2, c))
        for a, (val, ref) in enumerate(zip(group_a, (qa_ref, ka_ref, va_ref))):
            ref[:, sl] = val.astype(BF16)
            stage[a, 2 * c] = val[:, :LANES]
            stage[a, 2 * c + 1] = val[:, LANES:]
        qb_ref[:, sl] = _head_norm(proj(3, c), bd, gqb_ref[:, sl]).astype(BF16)
        kb = _head_norm(proj(4, c), bd, gkb_ref[:, sl])
        k1_ref[:, sl] = jnp.where(first, kb, 0.0).astype(BF16)
        k2_ref[:, sl] = jnp.where(first, 0.0, kb).astype(BF16)
        vbt_ref[sl, :] = proj(5, c).T.astype(BF16)

    for a, (ref4, ref16) in enumerate(((qa4_ref, qa16_ref), (ka4_ref, ka16_ref), (va4_ref, va16_ref))):
        for lo_r in range(4):
            for s in range(WIDTH_A // LANES):
                part = stage[a, s, pl.ds(lo_r, tm // 4, stride=4), :]
                stage4[a, lo_r, s] = part
                ref4[:, lo_r * WIDTH_A + s * LANES:lo_r * WIDTH_A + (s + 1) * LANES] = part.astype(BF16)
        for hi_r in range(4):
            for lo_r in range(4):
                for s in range(WIDTH_A // LANES):
                    lo = (4 * hi_r + lo_r) * WIDTH_A + s * LANES
                    part = stage4[a, lo_r, s, pl.ds(hi_r, tm // 16, stride=4), :]
                    ref16[:, lo:lo + LANES] = part.astype(BF16)


def _in_proj(x, shift, scale, g, w_bf, bd, gqa, gka, gqb, gkb):
    B, S, D = x.shape
    tm = PROJ_TM
    row = pl.BlockSpec((None, tm, D), lambda b, i: (b, i, 0))
    mod = pl.BlockSpec((None, 1, D), lambda b, i: (b, 0, 0))
    const = lambda shape: pl.BlockSpec(shape, lambda b, i: (0,) * len(shape))
    out = pl.BlockSpec((None, tm, PROJ_GROUP), lambda b, i: (b, i, 0))
    out_t = pl.BlockSpec((None, PROJ_GROUP, tm), lambda b, i: (b, 0, i))
    res = lambda dil: pl.BlockSpec((None, tm // dil, dil * WIDTH_A), lambda b, i: (b, i, 0))
    res_shape = lambda dil: jax.ShapeDtypeStruct((B, S // dil, dil * WIDTH_A), BF16)
    nat_shape = jax.ShapeDtypeStruct((B, S, PROJ_GROUP), BF16)
    gain = const((1, PROJ_GROUP))
    return pl.pallas_call(
        _in_proj_kernel,
        grid=(B, S // tm),
        in_specs=[row, mod, mod, const((1, D)), const(w_bf.shape), const(bd.shape),
                  gain, gain, gain, gain],
        out_specs=[out] * 3 + [res(4)] * 3 + [res(16)] * 3 + [out] * 3 + [out_t],
        out_shape=[nat_shape] * 3 + [res_shape(4)] * 3 + [res_shape(16)] * 3 + [nat_shape] * 3
                  + [jax.ShapeDtypeStruct((B, PROJ_GROUP, S), BF16)],
        scratch_shapes=[pltpu.VMEM((tm, w_bf.shape[1]), F32),
                        pltpu.VMEM((3, WIDTH_A // LANES, tm, LANES), F32),
                        pltpu.VMEM((3, 4, WIDTH_A // LANES, tm // 4, LANES), F32)],
        compiler_params=_params("arbitrary", "arbitrary"),
        name="in_proj",
    )(x, shift, scale, g, w_bf, bd, gqa, gka, gqb, gkb)


def _bias_from_rel(rel, tbl_ref, h):
    n = jnp.abs(rel)
    vneg = jnp.full(rel.shape, tbl_ref[0, h], F32)
    vpos = jnp.full(rel.shape, tbl_ref[NUM_BUCKETS // 2, h], F32)
    for i, th in enumerate(_BUCKET_THRESHOLDS):
        ge = n >= th
        vneg = jnp.where(ge, tbl_ref[i + 1, h], vneg)
        vpos = jnp.where(ge, tbl_ref[NUM_BUCKETS // 2 + i + 1, h], vpos)
    return jnp.where(rel > 0, vpos, vneg)


def _diff_bias_kernel(tbl_ref, o_ref, omax_ref, sub):
    h = pl.program_id(0)
    nk, nq = DIFF_TK // LANES, DIFF_TQ // LANES
    base = (DIFF_E_LO - 1) * nq
    lo_u = base - (nq - 1)
    n_sub = (DIFF_NT - 1) * nq + nk + nq - 1
    k = lax.broadcasted_iota(jnp.int32, (LANES, LANES), 0)
    q = lax.broadcasted_iota(jnp.int32, (LANES, LANES), 1)
    def one_offset(n, carry):
        sub[n] = _bias_from_rel(k - q + (lo_u + n) * LANES, tbl_ref, N_HEADS_A + h) * LOG2E
        return carry

    lax.fori_loop(0, n_sub, one_offset, 0)
    for t in range(DIFF_NT):
        for b in range(nq):
            cols = slice(b * LANES, (b + 1) * LANES)
            for a in range(nk):
                o_ref[t, a * LANES:(a + 1) * LANES, cols] = sub[base + t * nq + a - b - lo_u]
            omax_ref[t, :, cols] = jnp.max(o_ref[t, :, cols], axis=0, keepdims=True)


def _diff_bias(rel_bias):
    n_sub = (DIFF_NT - 1) * (DIFF_TQ // LANES) + DIFF_TK // LANES + DIFF_TQ // LANES - 1
    return pl.pallas_call(
        _diff_bias_kernel,
        grid=(N_HEADS_B,),
        in_specs=[pl.BlockSpec(memory_space=pltpu.SMEM)],
        out_specs=[pl.BlockSpec((None, DIFF_NT, DIFF_TK, DIFF_TQ), lambda h: (h, 0, 0, 0)),
                   pl.BlockSpec((None, DIFF_NT, 1, DIFF_TQ), lambda h: (h, 0, 0, 0))],
        out_shape=[jax.ShapeDtypeStruct((N_HEADS_B, DIFF_NT, DIFF_TK, DIFF_TQ), F32),
                   jax.ShapeDtypeStruct((N_HEADS_B, DIFF_NT, 1, DIFF_TQ), F32)],
        scratch_shapes=[pltpu.VMEM((n_sub, LANES, LANES), F32)],
        compiler_params=_params("arbitrary"),
        name="diff_bias",
    )(rel_bias)


def _dil_bias_kernel(tbl_ref, o_ref):
    p = pl.program_id(0)
    hp = pl.program_id(1)
    shape = (DIL_K, DIL_Q)
    k = lax.broadcasted_iota(jnp.int32, shape, 0)
    q = lax.broadcasted_iota(jnp.int32, shape, 1)
    steps = k - HALF_WIN - q
    dil = lax.shift_left(jnp.int32(1), 2 * p)
    inside = jnp.abs(steps) <= HALF_WIN
    for par in range(2):
        bias = jnp.where(inside, _bias_from_rel(steps * dil, tbl_ref, 2 * hp + par) * LOG2E, NEG)
        ls = slice(par * DIL_Q, (par + 1) * DIL_Q)
        o_ref[0, :, ls] = bias
        o_ref[1, :, ls] = jnp.where(k >= HALF_WIN, bias, NEG)
        o_ref[2, :, ls] = jnp.where(k < DIL_K - HALF_WIN, bias, NEG)


def _dil_bias(rel_bias):
    n_pat = len(DILATED_PATTERNS)
    n_pair = N_HEADS_A // 2
    return pl.pallas_call(
        _dil_bias_kernel,
        grid=(n_pat, n_pair),
        in_specs=[pl.BlockSpec(memory_space=pltpu.SMEM)],
        out_specs=pl.BlockSpec((None, 3, None, DIL_K, 2 * DIL_Q), lambda p, hp: (p, 0, hp, 0, 0)),
        out_shape=jax.ShapeDtypeStruct((n_pat, 3, n_pair, DIL_K, 2 * DIL_Q), F32),
        compiler_params=_params("arbitrary", "arbitrary"),
        name="dil_bias",
    )(rel_bias)


def _dil_kernel(q_ref, kp_ref, kc_ref, kn_ref, vp_ref, vc_ref, vn_ref, bias_ref, o_ref, lse_ref, kbuf, vtbuf):
    tl = q_ref.shape[0]
    i = pl.program_id(2)
    n_sub = tl // DIL_Q
    pairs_per_res = N_HEADS_A // 2
    n_pair = q_ref.shape[1] // LANES
    kbuf[0:HALF_WIN, :] = kp_ref[...]
    kbuf[HALF_WIN:HALF_WIN + tl, :] = kc_ref[...]
    kbuf[HALF_WIN + tl:, :] = kn_ref[...]
    for hp in range(n_pair):
        ls = slice(hp * LANES, (hp + 1) * LANES)
        v = jnp.concatenate([vp_ref[:, ls], vc_ref[:, ls], vn_ref[:, ls]], axis=0)
        vtbuf[hp] = v.astype(F32).T.astype(BF16)
    lane = lax.broadcasted_iota(jnp.int32, (1, LANES), 1)
    even = lane < HEAD_DIM
    contract_last = (((1,), (1,)), ((), ()))
    first_step = i == 0
    last_step = i == pl.num_programs(2) - 1

    def scores(sb, hp):
        r0 = sb * DIL_Q
        ls = slice(hp * LANES, (hp + 1) * LANES)
        if sb == 0:
            variant = jnp.where(first_step, 1, 0)
        elif sb == n_sub - 1:
            variant = jnp.where(last_step, 2, 0)
        else:
            variant = 0
        q = q_ref[r0:r0 + DIL_Q, ls]
        zero = jnp.zeros_like(q)
        qcat = jnp.concatenate([jnp.where(even, q, zero), jnp.where(even, zero, q)], axis=0)
        s = lax.dot_general(kbuf[r0:r0 + DIL_K, ls], qcat, contract_last, preferred_element_type=F32)
        return s + bias_ref[variant, hp % pairs_per_res]

    def finish(sb, hp, s):
        r0 = sb * DIL_Q
        ls = slice(hp * LANES, (hp + 1) * LANES)
        m = jnp.max(s, axis=0, keepdims=True)
        p = jnp.exp2(s - m)
        l = jnp.sum(p, axis=0, keepdims=True)
        ot = jnp.dot(vtbuf[hp, :, r0:r0 + DIL_K], p.astype(BF16), preferred_element_type=F32) / l
        ot = jnp.concatenate([ot[:HEAD_DIM, :DIL_Q], ot[HEAD_DIM:, DIL_Q:]], axis=0)
        o_ref[r0:r0 + DIL_Q, ls] = ot.T.astype(o_ref.dtype)
        lse = (m + jnp.log2(l)) * (1.0 / LOG2E)
        lse_rows.extend([lse[:, :DIL_Q], lse[:, DIL_Q:]])
        if hp % pairs_per_res == pairs_per_res - 1:
            res = hp // pairs_per_res
            tile = jnp.concatenate([jnp.broadcast_to(row, (LSE_REP, DIL_Q)) for row in lse_rows], axis=0)
            lse_ref[r0:r0 + DIL_Q, res * LANES:(res + 1) * LANES] = tile.T
            lse_rows.clear()

    lse_rows = []
    bodies = [(sb, hp) for sb in range(n_sub) for hp in range(n_pair)]
    staged = [scores(*b) for b in bodies[:DIL_AHEAD]]
    for n, b in enumerate(bodies):
        if n + DIL_AHEAD < len(bodies):
            staged.append(scores(*bodies[n + DIL_AHEAD]))
        finish(*b, staged[n])


def _dilated(q, k, v, bias, dil):
    B, L, _ = q.shape
    tl = min(DIL_TL, L)
    rps = min(dil, DIL_TL // tl)
    W = rps * WIDTH_A
    lse_spec = pl.BlockSpec((None, tl, rps * LANES), lambda b, r, i: (b, i, r))
    nhalo = L // HALF_WIN
    cur = pl.BlockSpec((None, tl, W), lambda b, r, i: (b, i, r))
    prev = pl.BlockSpec((None, HALF_WIN, W),
                        lambda b, r, i: (b, jnp.maximum(i * (tl // HALF_WIN) - 1, 0), r))
    nxt = pl.BlockSpec((None, HALF_WIN, W),
                       lambda b, r, i: (b, jnp.minimum((i + 1) * (tl // HALF_WIN), nhalo - 1), r))
    bias_spec = pl.BlockSpec(bias.shape, lambda b, r, i: (0,) * bias.ndim)
    return pl.pallas_call(
        _dil_kernel,
        grid=(B, dil // rps, L // tl),
        in_specs=[cur, prev, cur, nxt, prev, cur, nxt, bias_spec],
        out_specs=[cur, lse_spec],
        out_shape=[jax.ShapeDtypeStruct((B, L, dil * WIDTH_A), BF16),
                   jax.ShapeDtypeStruct((B, L, dil * LANES), F32)],
        scratch_shapes=[pltpu.VMEM((tl + 2 * HALF_WIN, W), BF16),
                        pltpu.VMEM((W // LANES, LANES, tl + 2 * HALF_WIN), BF16)],
        compiler_params=_params("arbitrary", "arbitrary", "arbitrary"),
        name=f"dilated_{dil}",
    )(q, k, k, k, v, v, v, bias)


def _diff_kernel(tbl_ref, q_ref, qn_ref, k1_ref, k2_ref, vt_ref, band_ref, bmax_ref, lq1_ref, lk1_ref, lq2_ref,
                 lk2_ref, g_ref, o_ref, m1, l1, a1, m2, l2, a2, sa, sb, mxa, mxb, *, lambda_init):
    h = pl.program_id(1)
    i = pl.program_id(2)
    T = DIFF_TK
    nk = k1_ref.shape[0] // T
    streams = ((k1_ref, m1, l1, a1), (k2_ref, m2, l2, a2))
    for _, m, l, a in streams:
        m[...] = jnp.full(m.shape, NEG, F32)
        l[...] = jnp.zeros(l.shape, F32)
        a[...] = jnp.zeros(a.shape, F32)
    q = q_ref[...]
    contract_last = (((1,), (1,)), ((), ()))

    def scores(j, sbuf, mxbuf, st, wraps=False):
        if wraps:
            wrap = j == nk
            j = jnp.where(wrap, 0, j)
            qq = jnp.where(wrap, qn_ref[...], q)
        else:
            qq = q
        r0 = pl.multiple_of(j * T, T)
        s = lax.dot_general(streams[st][0][pl.ds(r0, T), :], qq, contract_last, preferred_element_type=F32)
        sbuf[st] = s
        mxbuf[st] = jnp.max(s, axis=0, keepdims=True)

    def softmax_pv(j, sbuf, mxbuf, st, near, shift):
        r0 = pl.multiple_of(j * T, T)
        _, m, l, a = streams[st]
        s = sbuf[st]
        if near:
            t = jnp.clip(DIFF_RATIO * j - i, DIFF_E_LO - 1, DIFF_E_HI + 1) - (DIFF_E_LO - 1)
            s = s + band_ref[t]
            bound = mxbuf[st] + bmax_ref[t]
        else:
            bound = mxbuf[st] + shift
        m_prev = m[...]
        m_new = jnp.maximum(m_prev, bound)
        alpha = jnp.exp2(m_prev - m_new)
        p = jnp.exp2(s - (m_new - shift))
        l[...] = alpha * l[...] + jnp.sum(p, axis=0, keepdims=True)
        a[...] = alpha * a[...] + jnp.dot(vt_ref[:, pl.ds(r0, T)], p.astype(BF16), preferred_element_type=F32)
        m[...] = m_new

    def group(near, shift):
        def body(g, carry):
            j0 = DIFF_GROUP * g
            bufs = ((sa, mxa), (sb, mxb))
            for u in range(DIFF_GROUP):
                cur, nxt = bufs[u % 2], bufs[(u + 1) % 2]
                for st in range(2):
                    scores(j0 + u + 1, *nxt, st, wraps=u == DIFF_GROUP - 1)
                    softmax_pv(j0 + u, *cur, st, near, shift)
            return carry
        return body

    @pl.when(i == 0)
    def _():
        scores(0, sa, mxa, 0)
        scores(0, sa, mxa, 1)

    j_lo = (i + DIFF_E_LO + DIFF_RATIO - 1) // DIFF_RATIO
    j_hi = (i + DIFF_E_HI) // DIFF_RATIO
    lo = jnp.maximum(j_lo // DIFF_GROUP, 0)
    hi = jnp.minimum(j_hi // DIFF_GROUP + 1, nk // DIFF_GROUP)
    c_neg = tbl_ref[NUM_BUCKETS // 2 - 1, N_HEADS_A + h] * LOG2E
    c_pos = tbl_ref[NUM_BUCKETS - 1, N_HEADS_A + h] * LOG2E
    lax.fori_loop(0, lo, group(False, c_neg), 0)
    lax.fori_loop(lo, hi, group(True, 0.0), 0)
    lax.fori_loop(hi, nk // DIFF_GROUP, group(False, c_pos), 0)

    lam = (jnp.exp(jnp.sum(lq1_ref[...] * lk1_ref[...], axis=-1, keepdims=True))
           - jnp.exp(jnp.sum(lq2_ref[...] * lk2_ref[...], axis=-1, keepdims=True)) + lambda_init)
    o = a1[...] / l1[...] - lam * (a2[...] / l2[...])
    ms = jnp.mean(o * o, axis=0, keepdims=True)
    o = o * lax.rsqrt(ms + EPS) * (g_ref[...] * (1.0 - lambda_init))
    o_ref[...] = o.T.astype(o_ref.dtype)


def _diff_attention(rel_bias, qb, k1, k2, vbt, band, bmax, lq1, lk1, lq2, lk2, g_col, lambda_init):
    B, S, W = qb.shape
    T, TK = DIFF_TQ, DIFF_TK
    qspec = pl.BlockSpec((None, T, LANES), lambda b, h, i: (b, i, h))
    qnext = pl.BlockSpec((None, T, LANES), lambda b, h, i: (b, jnp.minimum(i + 1, S // T - 1), h))
    kspec = pl.BlockSpec((None, S, LANES), lambda b, h, i: (b, 0, h))
    vspec = pl.BlockSpec((None, LANES, S), lambda b, h, i: (b, h, 0))
    band_spec = pl.BlockSpec((None, DIFF_NT, TK, T), lambda b, h, i: (h, 0, 0, 0))
    bmax_spec = pl.BlockSpec((None, DIFF_NT, 1, T), lambda b, h, i: (h, 0, 0, 0))
    vec = lambda n: pl.BlockSpec((1, n), lambda b, h, i: (0, 0))
    stat = pltpu.VMEM((1, T), F32)
    acc = pltpu.VMEM((LANES, T), F32)
    return pl.pallas_call(
        functools.partial(_diff_kernel, lambda_init=lambda_init),
        grid=(B, N_HEADS_B, S // T),
        in_specs=[pl.BlockSpec(memory_space=pltpu.SMEM), qspec, qnext, kspec, kspec, vspec, band_spec, bmax_spec,
                  vec(HEAD_DIM), vec(HEAD_DIM), vec(HEAD_DIM), vec(HEAD_DIM),
                  pl.BlockSpec((LANES, 1), lambda b, h, i: (0, 0))],
        out_specs=qspec,
        out_shape=jax.ShapeDtypeStruct((B, S, W), BF16),
        scratch_shapes=[stat, stat, acc, stat, stat, acc, pltpu.VMEM((2, TK, T), F32), pltpu.VMEM((2, TK, T), F32),
                        pltpu.VMEM((2, 1, T), F32), pltpu.VMEM((2, 1, T), F32)],
        compiler_params=_params("arbitrary", "arbitrary", "arbitrary"),
        name="diff_attn",
    )(rel_bias, qb, qb, k1, k2, vbt, band, bmax, lq1, lk1, lq2, lk2, g_col)


def _out_proj_kernel(o1_ref, o4_ref, o16_ref, s1_ref, s4_ref, s16_ref, ob_ref, x_ref, gate_ref, shift_ref,
                     scale_ref, g_ref, wa_ref, wb_ref, sel_ref, h_ref, hn_ref, nat, nat_lse, oa_buf):
    tm = x_ref.shape[0]
    for n, (dil, o_ref, s_ref) in enumerate(((4, o4_ref, s4_ref), (16, o16_ref, s16_ref))):
        rows = tm // dil
        for r in range(dil):
            nat_lse[n, pl.ds(r, rows, stride=dil), :] = s_ref[:, r * LANES:(r + 1) * LANES]
            for s in range(WIDTH_A // LANES):
                lo = r * WIDTH_A + s * LANES
                nat[n, s, pl.ds(r, rows, stride=dil), :] = o_ref[:, lo:lo + LANES].astype(F32)
    s1, s2, s3 = s1_ref[...], nat_lse[0], nat_lse[1]
    mx = jnp.maximum(jnp.maximum(s1, s2), s3)
    e1, e2, e3 = jnp.exp(s1 - mx), jnp.exp(s2 - mx), jnp.exp(s3 - mx)
    inv = 1.0 / (e1 + e2 + e3)
    def expand(w):
        hi = w.astype(BF16)
        lo = (w - hi.astype(F32)).astype(BF16)
        sel = sel_ref[...]
        return (jnp.dot(hi, sel, preferred_element_type=F32) + jnp.dot(lo, sel, preferred_element_type=F32))

    w1, w2, w3 = expand(e1 * inv), expand(e2 * inv), expand(e3 * inv)
    for s in range(WIDTH_A // LANES):
        sl = slice(s * LANES, (s + 1) * LANES)
        oa = w1[:, sl] * o1_ref[:, sl].astype(F32) + w2[:, sl] * nat[0, s] + w3[:, sl] * nat[1, s]
        oa_buf[:, sl] = oa.astype(BF16)
    mixed = (jnp.dot(oa_buf[...], wa_ref[...], preferred_element_type=F32)
             + jnp.dot(ob_ref[...], wb_ref[...], preferred_element_type=F32))
    h = x_ref[...] + gate_ref[...] * mixed
    h_ref[...] = h
    ms = jnp.mean(h * h, axis=-1, keepdims=True)
    hn = h * lax.rsqrt(ms + EPS) * g_ref[...]
    hn_ref[...] = (hn * (1.0 + scale_ref[...]) + shift_ref[...]).astype(hn_ref.dtype)


def _head_select():
    src = jnp.arange(LANES)[:, None]
    head = jnp.arange(WIDTH_A)[None, :] // HEAD_DIM
    return (src == head * LSE_REP).astype(BF16)


def _out_proj(oas, lses, ob, x, gate, shift, scale, g, wa, wb):
    B, S, D = x.shape
    tm = OUT_TM
    sel = _head_select()
    half = pl.BlockSpec((None, tm, WIDTH_B), lambda b, i: (b, i, 0))
    row = pl.BlockSpec((None, tm, D), lambda b, i: (b, i, 0))
    mod = pl.BlockSpec((None, 1, D), lambda b, i: (b, 0, 0))
    const = lambda shape: pl.BlockSpec(shape, lambda b, i: (0,) * len(shape))
    res = lambda dil, w: pl.BlockSpec((None, tm // dil, dil * w), lambda b, i: (b, i, 0))
    outs = [res(1, WIDTH_A), res(4, WIDTH_A), res(16, WIDTH_A)]
    lse_specs = [res(1, LANES), res(4, LANES), res(16, LANES)]
    return pl.pallas_call(
        _out_proj_kernel,
        grid=(B, S // tm),
        in_specs=outs + lse_specs + [half, row, mod, mod, mod, const((1, D)), const(wa.shape), const(wb.shape),
                                     const(sel.shape)],
        out_specs=[row, row],
        out_shape=[jax.ShapeDtypeStruct((B, S, D), F32), jax.ShapeDtypeStruct((B, S, D), BF16)],
        scratch_shapes=[pltpu.VMEM((2, WIDTH_A // LANES, tm, LANES), F32), pltpu.VMEM((2, tm, LANES), F32),
                        pltpu.VMEM((tm, WIDTH_A), BF16)],
        compiler_params=_params("arbitrary", "arbitrary"),
        name="out_proj",
    )(*oas, *lses, ob, x, gate, shift, scale, g, wa, wb, sel)


def _ffn_kernel(hp_ref, hc_ref, hn_ref, h_ref, gate_ref, wu_ref, cw_ref, cb_ref, wd_ref, o_ref, lhs, act):
    i = pl.program_id(1)
    tm = hc_ref.shape[0]
    n = tm + 2 * FFN_HALO
    lhs[0:FFN_HALO, :] = jnp.where(i > 0, hp_ref[...], jnp.zeros_like(hp_ref))
    lhs[FFN_HALO:FFN_HALO + tm, :] = hc_ref[...]
    lhs[FFN_HALO + tm:, :] = jnp.where(i < pl.num_programs(1) - 1, hn_ref[...], jnp.zeros_like(hn_ref))
    x = lhs[...]

    def conv(u, lo):
        cw = cw_ref[:, lo:lo + FFN_FC]
        mid = slice(FFN_HALO, FFN_HALO + tm)
        below = pltpu.roll(u, 1, 0)[mid]
        above = pltpu.roll(u, n - 1, 0)[mid]
        return cw[0:1] * below + cw[1:2] * u[mid] + cw[2:3] * above + cb_ref[:, lo:lo + FFN_FC]

    for c in range(D_FF // FFN_FC):
        lo_v = c * FFN_FC
        lo_g = D_FF + c * FFN_FC
        val = conv(jnp.dot(x, wu_ref[:, lo_v:lo_v + FFN_FC], preferred_element_type=F32), lo_v)
        gt = conv(jnp.dot(x, wu_ref[:, lo_g:lo_g + FFN_FC], preferred_element_type=F32), lo_g)
        act[:, lo_v:lo_v + FFN_FC] = (gt / (1.0 + jnp.exp(-gt)) * val).astype(BF16)
    down = jnp.dot(act[...], wd_ref[...], preferred_element_type=F32)
    o_ref[...] = h_ref[...] + gate_ref[...] * down


def _ffn(hn, h, gate, wu, cw, cb, wd):
    B, S, D = h.shape
    tm = FFN_TM
    per = tm // FFN_HALO
    nh = S // FFN_HALO
    row = lambda dt: pl.BlockSpec((None, tm, D), lambda b, i: (b, i, 0))
    prev = pl.BlockSpec((None, FFN_HALO, D), lambda b, i: (b, jnp.maximum(i * per - 1, 0), 0))
    nxt = pl.BlockSpec((None, FFN_HALO, D), lambda b, i: (b, jnp.minimum((i + 1) * per, nh - 1), 0))
    mod = pl.BlockSpec((None, 1, D), lambda b, i: (b, 0, 0))
    const = lambda shape: pl.BlockSpec(shape, lambda b, i: (0,) * len(shape), pipeline_mode=pl.Buffered(1))
    return pl.pallas_call(
        _ffn_kernel,
        grid=(B, S // tm),
        in_specs=[prev, row(BF16), nxt, row(F32), mod, const(wu.shape), const(cw.shape), const(cb.shape),
                  const(wd.shape)],
        out_specs=row(F32),
        out_shape=jax.ShapeDtypeStruct((B, S, D), F32),
        scratch_shapes=[pltpu.VMEM((tm + 2 * FFN_HALO, D), BF16), pltpu.VMEM((tm, D_FF), BF16)],
        compiler_params=_params("arbitrary", "arbitrary"),
        name="conv_ffn",
    )(hn, hn, hn, h, gate, wu, cw, cb, wd)


def _block_diag_ones():
    r = jnp.arange(MXU_DIM) // HEAD_DIM
    return (r[:, None] == r[None, :]).astype(BF16)


def kernel(x, c, w_ada, b_ada, norm1_g, w_in, q_norm_a, k_norm_a, q_norm_b, k_norm_b, rel_bias, lambda_q1,
           lambda_k1, lambda_q2, lambda_k2, subln_g, w_out, norm2_g, w_up, conv_w, conv_b, w_down):
    B, S, D = x.shape
    depth = w_ada.shape[0]
    h = x.astype(F32)
    c8 = jnp.pad(c.astype(F32), ((0, 8 - B), (0, 0)))
    bd = _block_diag_ones()
    rel_bias = rel_bias.astype(F32)
    band_b, bmax_b = _diff_bias(rel_bias)
    band_a = _dil_bias(rel_bias)
    qscale = HEAD_DIM ** -0.5
    tile8 = lambda g: jnp.tile(g.astype(F32), WIDTH_A // HEAD_DIM)[None, :]
    row = lambda v: v.astype(F32)[None, :]

    for layer in range(depth):
        lambda_init = 0.8 - 0.6 * math.exp(-0.3 * layer)
        mod = _ada(c8, w_ada[layer].astype(F32), row(b_ada[layer]))[:B]
        shift1, scale1, gate1, shift2, scale2, gate2 = [m[:, None, :] for m in jnp.split(mod, 6, axis=-1)]

        qa, ka, va, qa4, ka4, va4, qa16, ka16, va16, qb, k1, k2, vbt = _in_proj(
            h, shift1, scale1, row(norm1_g[layer]), w_in[layer].astype(BF16), bd,
            tile8(q_norm_a[layer]) * (qscale * LOG2E), tile8(k_norm_a[layer]),
            tile8(q_norm_b[layer]) * (qscale * LOG2E), tile8(k_norm_b[layer]))

        oas, lses = [], []
        qkv = {1: (qa, ka, va), 4: (qa4, ka4, va4), 16: (qa16, ka16, va16)}
        for p, (_, dil) in enumerate(DILATED_PATTERNS):
            o, lse = _dilated(*qkv[dil], band_a[p], dil)
            oas.append(o)
            lses.append(lse)

        ob = _diff_attention(rel_bias, qb, k1, k2, vbt, band_b, bmax_b, row(lambda_q1[layer]), row(lambda_k1[layer]),
                             row(lambda_q2[layer]), row(lambda_k2[layer]),
                             subln_g[layer].astype(F32)[:, None], lambda_init)

        w_o = w_out[layer].astype(BF16)
        h, hn = _out_proj(oas, lses, ob, h, gate1, shift2, scale2, row(norm2_g[layer]),
                          w_o[:WIDTH_A], w_o[WIDTH_A:])
        h = _ffn(hn, h, gate2, w_up[layer].astype(BF16), conv_w[layer].astype(F32), row(conv_b[layer]),
                 w_down[layer].astype(BF16))

    return h.astype(x.dtype)
```

```python
import functools
import math

import jax
import jax.numpy as jnp
from jax import lax
from jax.experimental import pallas as pl
from jax.experimental.pallas import tpu as pltpu

F32 = jnp.float32
BF16 = jnp.bfloat16

D_MODEL = 1024
HEAD_DIM = 64
WIDTH_A = 512
WIDTH_B = 512
N_HEADS_A = 8
N_HEADS_B = 4
DILATED_PATTERNS = ((128, 1), (512, 4), (2048, 16))
D_FF = 2816
NUM_BUCKETS = 32
MAX_DISTANCE = 2048
EPS = 1e-6
NEG = -1e30
LOG2E = math.log2(math.e)

LANES = 128
BF16_SUBLANES = 16
MXU_DIM = 256
VMEM_LIMIT = 56 * 1024 * 1024

HALF_WIN = 64
DIL_Q = 128
DIL_K = DIL_Q + 2 * HALF_WIN
DIL_TL = 2048
LSE_REP = LANES // N_HEADS_A
DIL_AHEAD = 4

_MAX_EXACT = NUM_BUCKETS // 4
assert MAX_DISTANCE // _MAX_EXACT == 2 ** (NUM_BUCKETS // 2 - _MAX_EXACT)
_BUCKET_THRESHOLDS = tuple(range(1, _MAX_EXACT + 1)) + tuple(
    _MAX_EXACT * 2 ** k for k in range(1, NUM_BUCKETS // 2 - _MAX_EXACT))
BUCKET_SAT = _BUCKET_THRESHOLDS[-1]

DIFF_TQ = 512
DIFF_TK = 512
DIFF_RATIO = DIFF_TK // DIFF_TQ
DIFF_E_LO = -(BUCKET_SAT + DIFF_TK) // DIFF_TQ + 1
DIFF_E_HI = (BUCKET_SAT + DIFF_TQ) // DIFF_TQ - 1
DIFF_NT = DIFF_E_HI - DIFF_E_LO + 3
DIFF_GROUP = 8

PROJ_TM = 512
PROJ_GROUP = WIDTH_A
ADA_TN = 1024
OUT_TM = 1024
FFN_TM = 1024
FFN_FC = 256
FFN_HALO = BF16_SUBLANES


def _params(*sem):
    return pltpu.CompilerParams(dimension_semantics=sem, vmem_limit_bytes=VMEM_LIMIT)


def _ada_kernel(c_ref, w_ref, b_ref, o_ref):
    c = c_ref[...]
    ca = c / (1.0 + jnp.exp(-c))
    o_ref[...] = jnp.dot(ca, w_ref[...], preferred_element_type=F32) + b_ref[...]


def _ada(c8, w, b):
    n = w.shape[1]
    tn = ADA_TN
    return pl.pallas_call(
        _ada_kernel,
        grid=(n // tn,),
        in_specs=[pl.BlockSpec((8, D_MODEL), lambda j: (0, 0)),
                  pl.BlockSpec((D_MODEL, tn), lambda j: (0, j)),
                  pl.BlockSpec((1, tn), lambda j: (0, j))],
        out_specs=pl.BlockSpec((8, tn), lambda j: (0, j)),
        out_shape=jax.ShapeDtypeStruct((8, n), F32),
        compiler_params=_params("arbitrary"),
        name="ada",
    )(c8, w, b)


def _head_norm(p, bd, gain):
    ss = jnp.dot((p * p).astype(BF16), bd, preferred_element_type=F32)
    return p * lax.rsqrt(ss * (1.0 / HEAD_DIM) + EPS) * gain


def _in_proj_kernel(x_ref, shift_ref, scale_ref, g_ref, w_ref, bd_ref, gqa_ref, gka_ref, gqb_ref, gkb_ref,
                    qa_ref, ka_ref, va_ref, qa4_ref, ka4_ref, va4_ref, qa16_ref, ka16_ref, va16_ref,
                    qb_ref, k1_ref, k2_ref, vbt_ref, pbuf, stage, stage4):
    tm = x_ref.shape[0]
    x = x_ref[...]
    ms = jnp.mean(x * x, axis=-1, keepdims=True)
    hn = x * lax.rsqrt(ms + EPS) * g_ref[...]
    hn = (hn * (1.0 + scale_ref[...]) + shift_ref[...]).astype(BF16)
    pbuf[...] = jnp.dot(hn, w_ref[...], preferred_element_type=F32)
    bd = bd_ref[...]
    lane = lax.broadcasted_iota(jnp.int32, (1, MXU_DIM), 1)
    first = (lane // HEAD_DIM) % 2 == 0

    def proj(g, c):
        lo = g * PROJ_GROUP + c * MXU_DIM
        return pbuf[:, lo:lo + MXU_DIM]

    for c in range(2):
        sl = slice(c * MXU_DIM, (c + 1) * MXU_DIM)
        group_a = (_head_norm(proj(0, c), bd, gqa_ref[:, sl]), _head_norm(proj(1, c), bd, gka_ref[:, sl]),
                   proj(2, c))
        for a, (val, ref) in enumerate(zip(group_a, (qa_ref, ka_ref, va_ref))):
            ref[:, sl] = val.astype(BF16)
            stage[a, 2 * c] = val[:, :LANES]
            stage[a, 2 * c + 1] = val[:, LANES:]
        qb_ref[:, sl] = _head_norm(proj(3, c), bd, gqb_ref[:, sl]).astype(BF16)
        kb = _head_norm(proj(4, c), bd, gkb_ref[:, sl])
        k1_ref[:, sl] = jnp.where(first, kb, 0.0).astype(BF16)
        k2_ref[:, sl] = jnp.where(first, 0.0, kb).astype(BF16)
        vbt_ref[sl, :] = proj(5, c).T.astype(BF16)

    for a, (ref4, ref16) in enumerate(((qa4_ref, qa16_ref), (ka4_ref, ka16_ref), (va4_ref, va16_ref))):
        for lo_r in range(4):
            for s in range(WIDTH_A // LANES):
                part = stage[a, s, pl.ds(lo_r, tm // 4, stride=4), :]
                stage4[a, lo_r, s] = part
                ref4[:, lo_r * WIDTH_A + s * LANES:lo_r * WIDTH_A + (s + 1) * LANES] = part.astype(BF16)
        for hi_r in range(4):
            for lo_r in range(4):
                for s in range(WIDTH_A // LANES):
                    lo = (4 * hi_r + lo_r) * WIDTH_A + s * LANES
                    part = stage4[a, lo_r, s, pl.ds(hi_r, tm // 16, stride=4), :]
                    ref16[:, lo:lo + LANES] = part.astype(BF16)


def _in_proj(x, shift, scale, g, w_bf, bd, gqa, gka, gqb, gkb):
    B, S, D = x.shape
    tm = PROJ_TM
    row = pl.BlockSpec((None, tm, D), lambda b, i: (b, i, 0))
    mod = pl.BlockSpec((None, 1, D), lambda b, i: (b, 0, 0))
    const = lambda shape: pl.BlockSpec(shape, lambda b, i: (0,) * len(shape))
    out = pl.BlockSpec((None, tm, PROJ_GROUP), lambda b, i: (b, i, 0))
    out_t = pl.BlockSpec((None, PROJ_GROUP, tm), lambda b, i: (b, 0, i))
    res = lambda dil: pl.BlockSpec((None, tm // dil, dil * WIDTH_A), lambda b, i: (b, i, 0))
    res_shape = lambda dil: jax.ShapeDtypeStruct((B, S // dil, dil * WIDTH_A), BF16)
    nat_shape = jax.ShapeDtypeStruct((B, S, PROJ_GROUP), BF16)
    gain = const((1, PROJ_GROUP))
    return pl.pallas_call(
        _in_proj_kernel,
        grid=(B, S // tm),
        in_specs=[row, mod, mod, const((1, D)), const(w_bf.shape), const(bd.shape),
                  gain, gain, gain, gain],
        out_specs=[out] * 3 + [res(4)] * 3 + [res(16)] * 3 + [out] * 3 + [out_t],
        out_shape=[nat_shape] * 3 + [res_shape(4)] * 3 + [res_shape(16)] * 3 + [nat_shape] * 3
                  + [jax.ShapeDtypeStruct((B, PROJ_GROUP, S), BF16)],
        scratch_shapes=[pltpu.VMEM((tm, w_bf.shape[1]), F32),
                        pltpu.VMEM((3, WIDTH_A // LANES, tm, LANES), F32),
                        pltpu.VMEM((3, 4, WIDTH_A // LANES, tm // 4, LANES), F32)],
        compiler_params=_params("arbitrary", "arbitrary"),
        name="in_proj",
    )(x, shift, scale, g, w_bf, bd, gqa, gka, gqb, gkb)


def _bias_from_rel(rel, tbl_ref, h):
    n = jnp.abs(rel)
    vneg = jnp.full(rel.shape, tbl_ref[0, h], F32)
    vpos = jnp.full(rel.shape, tbl_ref[NUM_BUCKETS // 2, h], F32)
    for i, th in enumerate(_BUCKET_THRESHOLDS):
        ge = n >= th
        vneg = jnp.where(ge, tbl_ref[i + 1, h], vneg)
        vpos = jnp.where(ge, tbl_ref[NUM_BUCKETS // 2 + i + 1, h], vpos)
    return jnp.where(rel > 0, vpos, vneg)


def _diff_bias_kernel(tbl_ref, o_ref, omax_ref, sub):
    h = pl.program_id(0)
    nk, nq = DIFF_TK // LANES, DIFF_TQ // LANES
    base = (DIFF_E_LO - 1) * nq
    lo_u = base - (nq - 1)
    n_sub = (DIFF_NT - 1) * nq + nk + nq - 1
    k = lax.broadcasted_iota(jnp.int32, (LANES, LANES), 0)
    q = lax.broadcasted_iota(jnp.int32, (LANES, LANES), 1)
    def one_offset(n, carry):
        sub[n] = _bias_from_rel(k - q + (lo_u + n) * LANES, tbl_ref, N_HEADS_A + h) * LOG2E
        return carry

    lax.fori_loop(0, n_sub, one_offset, 0)
    for t in range(DIFF_NT):
        for b in range(nq):
            cols = slice(b * LANES, (b + 1) * LANES)
            for a in range(nk):
                o_ref[t, a * LANES:(a + 1) * LANES, cols] = sub[base + t * nq + a - b - lo_u]
            omax_ref[t, :, cols] = jnp.max(o_ref[t, :, cols], axis=0, keepdims=True)


def _diff_bias(rel_bias):
    n_sub = (DIFF_NT - 1) * (DIFF_TQ // LANES) + DIFF_TK // LANES + DIFF_TQ // LANES - 1
    return pl.pallas_call(
        _diff_bias_kernel,
        grid=(N_HEADS_B,),
        in_specs=[pl.BlockSpec(memory_space=pltpu.SMEM)],
        out_specs=[pl.BlockSpec((None, DIFF_NT, DIFF_TK, DIFF_TQ), lambda h: (h, 0, 0, 0)),
                   pl.BlockSpec((None, DIFF_NT, 1, DIFF_TQ), lambda h: (h, 0, 0, 0))],
        out_shape=[jax.ShapeDtypeStruct((N_HEADS_B, DIFF_NT, DIFF_TK, DIFF_TQ), F32),
                   jax.ShapeDtypeStruct((N_HEADS_B, DIFF_NT, 1, DIFF_TQ), F32)],
        scratch_shapes=[pltpu.VMEM((n_sub, LANES, LANES), F32)],
        compiler_params=_params("arbitrary"),
        name="diff_bias",
    )(rel_bias)


def _dil_bias_kernel(tbl_ref, o_ref):
    p = pl.program_id(0)
    hp = pl.program_id(1)
    shape = (DIL_K, DIL_Q)
    k = lax.broadcasted_iota(jnp.int32, shape, 0)
    q = lax.broadcasted_iota(jnp.int32, shape, 1)
    steps = k - HALF_WIN - q
    dil = lax.shift_left(jnp.int32(1), 2 * p)
    inside = jnp.abs(steps) <= HALF_WIN
    for par in range(2):
        bias = jnp.where(inside, _bias_from_rel(steps * dil, tbl_ref, 2 * hp + par) * LOG2E, NEG)
        ls = slice(par * DIL_Q, (par + 1) * DIL_Q)
        o_ref[0, :, ls] = bias
        o_ref[1, :, ls] = jnp.where(k >= HALF_WIN, bias, NEG)
        o_ref[2, :, ls] = jnp.where(k < DIL_K - HALF_WIN, bias, NEG)


def _dil_bias(rel_bias):
    n_pat = len(DILATED_PATTERNS)
    n_pair = N_HEADS_A // 2
    return pl.pallas_call(
        _dil_bias_kernel,
        grid=(n_pat, n_pair),
        in_specs=[pl.BlockSpec(memory_space=pltpu.SMEM)],
        out_specs=pl.BlockSpec((None, 3, None, DIL_K, 2 * DIL_Q), lambda p, hp: (p, 0, hp, 0, 0)),
        out_shape=jax.ShapeDtypeStruct((n_pat, 3, n_pair, DIL_K, 2 * DIL_Q), F32),
        compiler_params=_params("arbitrary", "arbitrary"),
        name="dil_bias",
    )(rel_bias)


def _dil_kernel(q_ref, kp_ref, kc_ref, kn_ref, vp_ref, vc_ref, vn_ref, bias_ref, o_ref, lse_ref, kbuf, vtbuf):
    tl = q_ref.shape[0]
    i = pl.program_id(2)
    n_sub = tl // DIL_Q
    pairs_per_res = N_HEADS_A // 2
    n_pair = q_ref.shape[1] // LANES
    kbuf[0:HALF_WIN, :] = kp_ref[...]
    kbuf[HALF_WIN:HALF_WIN + tl, :] = kc_ref[...]
    kbuf[HALF_WIN + tl:, :] = kn_ref[...]
    for hp in range(n_pair):
        ls = slice(hp * LANES, (hp + 1) * LANES)
        v = jnp.concatenate([vp_ref[:, ls], vc_ref[:, ls], vn_ref[:, ls]], axis=0)
        vtbuf[hp] = v.astype(F32).T.astype(BF16)
    lane = lax.broadcasted_iota(jnp.int32, (1, LANES), 1)
    even = lane < HEAD_DIM
    contract_last = (((1,), (1,)), ((), ()))
    first_step = i == 0
    last_step = i == pl.num_programs(2) - 1

    def scores(sb, hp):
        r0 = sb * DIL_Q
        ls = slice(hp * LANES, (hp + 1) * LANES)
        if sb == 0:
            variant = jnp.where(first_step, 1, 0)
        elif sb == n_sub - 1:
            variant = jnp.where(last_step, 2, 0)
        else:
            variant = 0
        q = q_ref[r0:r0 + DIL_Q, ls]
        zero = jnp.zeros_like(q)
        qcat = jnp.concatenate([jnp.where(even, q, zero), jnp.where(even, zero, q)], axis=0)
        s = lax.dot_general(kbuf[r0:r0 + DIL_K, ls], qcat, contract_last, preferred_element_type=F32)
        return s + bias_ref[variant, hp % pairs_per_res]

    def finish(sb, hp, s):
        r0 = sb * DIL_Q
        ls = slice(hp * LANES, (hp + 1) * LANES)
        m = jnp.max(s, axis=0, keepdims=True)
        p = jnp.exp2(s - m)
        l = jnp.sum(p, axis=0, keepdims=True)
        ot = jnp.dot(vtbuf[hp, :, r0:r0 + DIL_K], p.astype(BF16), preferred_element_type=F32) / l
        ot = jnp.concatenate([ot[:HEAD_DIM, :DIL_Q], ot[HEAD_DIM:, DIL_Q:]], axis=0)
        o_ref[r0:r0 + DIL_Q, ls] = ot.T.astype(o_ref.dtype)
        lse = (m + jnp.log2(l)) * (1.0 / LOG2E)
        lse_rows.extend([lse[:, :DIL_Q], lse[:, DIL_Q:]])
        if hp % pairs_per_res == pairs_per_res - 1:
            res = hp // pairs_per_res
            tile = jnp.concatenate([jnp.broadcast_to(row, (LSE_REP, DIL_Q)) for row in lse_rows], axis=0)
            lse_ref[r0:r0 + DIL_Q, res * LANES:(res + 1) * LANES] = tile.T
            lse_rows.clear()

    lse_rows = []
    bodies = [(sb, hp) for sb in range(n_sub) for hp in range(n_pair)]
    staged = [scores(*b) for b in bodies[:DIL_AHEAD]]
    for n, b in enumerate(bodies):
        if n + DIL_AHEAD < len(bodies):
            staged.append(scores(*bodies[n + DIL_AHEAD]))
        finish(*b, staged[n])


def _dilated(q, k, v, bias, dil):
    B, L, _ = q.shape
    tl = min(DIL_TL, L)
    rps = min(dil, DIL_TL // tl)
    W = rps * WIDTH_A
    lse_spec = pl.BlockSpec((None, tl, rps * LANES), lambda b, r, i: (b, i, r))
    nhalo = L // HALF_WIN
    cur = pl.BlockSpec((None, tl, W), lambda b, r, i: (b, i, r))
    prev = pl.BlockSpec((None, HALF_WIN, W),
                        lambda b, r, i: (b, jnp.maximum(i * (tl // HALF_WIN) - 1, 0), r))
    nxt = pl.BlockSpec((None, HALF_WIN, W),
                       lambda b, r, i: (b, jnp.minimum((i + 1) * (tl // HALF_WIN), nhalo - 1), r))
    bias_spec = pl.BlockSpec(bias.shape, lambda b, r, i: (0,) * bias.ndim)
    return pl.pallas_call(
        _dil_kernel,
        grid=(B, dil // rps, L // tl),
        in_specs=[cur, prev, cur, nxt, prev, cur, nxt, bias_spec],
        out_specs=[cur, lse_spec],
        out_shape=[jax.ShapeDtypeStruct((B, L, dil * WIDTH_A), BF16),
                   jax.ShapeDtypeStruct((B, L, dil * LANES), F32)],
        scratch_shapes=[pltpu.VMEM((tl + 2 * HALF_WIN, W), BF16),
                        pltpu.VMEM((W // LANES, LANES, tl + 2 * HALF_WIN), BF16)],
        compiler_params=_params("arbitrary", "arbitrary", "arbitrary"),
        name=f"dilated_{dil}",
    )(q, k, k, k, v, v, v, bias)


def _diff_kernel(tbl_ref, q_ref, qn_ref, k1_ref, k2_ref, vt_ref, band_ref, bmax_ref, lq1_ref, lk1_ref, lq2_ref,
                 lk2_ref, g_ref, o_ref, m1, l1, a1, m2, l2, a2, sa, sb, mxa, mxb, *, lambda_init):
    h = pl.program_id(1)
    i = pl.program_id(2)
    T = DIFF_TK
    nk = k1_ref.shape[0] // T
    streams = ((k1_ref, m1, l1, a1), (k2_ref, m2, l2, a2))
    for _, m, l, a in streams:
        m[...] = jnp.full(m.shape, NEG, F32)
        l[...] = jnp.zeros(l.shape, F32)
        a[...] = jnp.zeros(a.shape, F32)
    q = q_ref[...]
    contract_last = (((1,), (1,)), ((), ()))

    def scores(j, sbuf, mxbuf, st, wraps=False):
        if wraps:
            wrap = j == nk
            j = jnp.where(wrap, 0, j)
            qq = jnp.where(wrap, qn_ref[...], q)
        else:
            qq = q
        r0 = pl.multiple_of(j * T, T)
        s = lax.dot_general(streams[st][0][pl.ds(r0, T), :], qq, contract_last, preferred_element_type=F32)
        sbuf[st] = s
        mxbuf[st] = jnp.max(s, axis=0, keepdims=True)

    def softmax_pv(j, sbuf, mxbuf, st, near, shift):
        r0 = pl.multiple_of(j * T, T)
        _, m, l, a = streams[st]
        s = sbuf[st]
        if near:
            t = jnp.clip(DIFF_RATIO * j - i, DIFF_E_LO - 1, DIFF_E_HI + 1) - (DIFF_E_LO - 1)
            s = s + band_ref[t]
            bound = mxbuf[st] + bmax_ref[t]
        else:
            bound = mxbuf[st] + shift
        m_prev = m[...]
        m_new = jnp.maximum(m_prev, bound)
        alpha = jnp.exp2(m_prev - m_new)
        p = jnp.exp2(s - (m_new - shift))
        l[...] = alpha * l[...] + jnp.sum(p, axis=0, keepdims=True)
        a[...] = alpha * a[...] + jnp.dot(vt_ref[:, pl.ds(r0, T)], p.astype(BF16), preferred_element_type=F32)
        m[...] = m_new

    def group(near, shift):
        def body(g, carry):
            j0 = DIFF_GROUP * g
            bufs = ((sa, mxa), (sb, mxb))
            for u in range(DIFF_GROUP):
                cur, nxt = bufs[u % 2], bufs[(u + 1) % 2]
                for st in range(2):
                    scores(j0 + u + 1, *nxt, st, wraps=u == DIFF_GROUP - 1)
                    softmax_pv(j0 + u, *cur, st, near, shift)
            return carry
        return body

    @pl.when(i == 0)
    def _():
        scores(0, sa, mxa, 0)
        scores(0, sa, mxa, 1)

    j_lo = (i + DIFF_E_LO + DIFF_RATIO - 1) // DIFF_RATIO
    j_hi = (i + DIFF_E_HI) // DIFF_RATIO
    lo = jnp.maximum(j_lo // DIFF_GROUP, 0)
    hi = jnp.minimum(j_hi // DIFF_GROUP + 1, nk // DIFF_GROUP)
    c_neg = tbl_ref[NUM_BUCKETS // 2 - 1, N_HEADS_A + h] * LOG2E
    c_pos = tbl_ref[NUM_BUCKETS - 1, N_HEADS_A + h] * LOG2E
    lax.fori_loop(0, lo, group(False, c_neg), 0)
    lax.fori_loop(lo, hi, group(True, 0.0), 0)
    lax.fori_loop(hi, nk // DIFF_GROUP, group(False, c_pos), 0)

    lam = (jnp.exp(jnp.sum(lq1_ref[...] * lk1_ref[...], axis=-1, keepdims=True))
           - jnp.exp(jnp.sum(lq2_ref[...] * lk2_ref[...], axis=-1, keepdims=True)) + lambda_init)
    o = a1[...] / l1[...] - lam * (a2[...] / l2[...])
    ms = jnp.mean(o * o, axis=0, keepdims=True)
    o = o * lax.rsqrt(ms + EPS) * (g_ref[...] * (1.0 - lambda_init))
    o_ref[...] = o.T.astype(o_ref.dtype)


def _diff_attention(rel_bias, qb, k1, k2, vbt, band, bmax, lq1, lk1, lq2, lk2, g_col, lambda_init):
    B, S, W = qb.shape
    T, TK = DIFF_TQ, DIFF_TK
    qspec = pl.BlockSpec((None, T, LANES), lambda b, h, i: (b, i, h))
    qnext = pl.BlockSpec((None, T, LANES), lambda b, h, i: (b, jnp.minimum(i + 1, S // T - 1), h))
    kspec = pl.BlockSpec((None, S, LANES), lambda b, h, i: (b, 0, h))
    vspec = pl.BlockSpec((None, LANES, S), lambda b, h, i: (b, h, 0))
    band_spec = pl.BlockSpec((None, DIFF_NT, TK, T), lambda b, h, i: (h, 0, 0, 0))
    bmax_spec = pl.BlockSpec((None, DIFF_NT, 1, T), lambda b, h, i: (h, 0, 0, 0))
    vec = lambda n: pl.BlockSpec((1, n), lambda b, h, i: (0, 0))
    stat = pltpu.VMEM((1, T), F32)
    acc = pltpu.VMEM((LANES, T), F32)
    return pl.pallas_call(
        functools.partial(_diff_kernel, lambda_init=lambda_init),
        grid=(B, N_HEADS_B, S // T),
        in_specs=[pl.BlockSpec(memory_space=pltpu.SMEM), qspec, qnext, kspec, kspec, vspec, band_spec, bmax_spec,
                  vec(HEAD_DIM), vec(HEAD_DIM), vec(HEAD_DIM), vec(HEAD_DIM),
                  pl.BlockSpec((LANES, 1), lambda b, h, i: (0, 0))],
        out_specs=qspec,
        out_shape=jax.ShapeDtypeStruct((B, S, W), BF16),
        scratch_shapes=[stat, stat, acc, stat, stat, acc, pltpu.VMEM((2, TK, T), F32), pltpu.VMEM((2, TK, T), F32),
                        pltpu.VMEM((2, 1, T), F32), pltpu.VMEM((2, 1, T), F32)],
        compiler_params=_params("arbitrary", "arbitrary", "arbitrary"),
        name="diff_attn",
    )(rel_bias, qb, qb, k1, k2, vbt, band, bmax, lq1, lk1, lq2, lk2, g_col)


def _out_proj_kernel(o1_ref, o4_ref, o16_ref, s1_ref, s4_ref, s16_ref, ob_ref, x_ref, gate_ref, shift_ref,
                     scale_ref, g_ref, wa_ref, wb_ref, sel_ref, h_ref, hn_ref, nat, nat_lse, oa_buf):
    tm = x_ref.shape[0]
    for n, (dil, o_ref, s_ref) in enumerate(((4, o4_ref, s4_ref), (16, o16_ref, s16_ref))):
        rows = tm // dil
        for r in range(dil):
            nat_lse[n, pl.ds(r, rows, stride=dil), :] = s_ref[:, r * LANES:(r + 1) * LANES]
            for s in range(WIDTH_A // LANES):
                lo = r * WIDTH_A + s * LANES
                nat[n, s, pl.ds(r, rows, stride=dil), :] = o_ref[:, lo:lo + LANES].astype(F32)
    s1, s2, s3 = s1_ref[...], nat_lse[0], nat_lse[1]
    mx = jnp.maximum(jnp.maximum(s1, s2), s3)
    e1, e2, e3 = jnp.exp(s1 - mx), jnp.exp(s2 - mx), jnp.exp(s3 - mx)
    inv = 1.0 / (e1 + e2 + e3)
    def expand(w):
        hi = w.astype(BF16)
        lo = (w - hi.astype(F32)).astype(BF16)
        sel = sel_ref[...]
        return (jnp.dot(hi, sel, preferred_element_type=F32) + jnp.dot(lo, sel, preferred_element_type=F32))

    w1, w2, w3 = expand(e1 * inv), expand(e2 * inv), expand(e3 * inv)
    for s in range(WIDTH_A // LANES):
        sl = slice(s * LANES, (s + 1) * LANES)
        oa = w1[:, sl] * o1_ref[:, sl].astype(F32) + w2[:, sl] * nat[0, s] + w3[:, sl] * nat[1, s]
        oa_buf[:, sl] = oa.astype(BF16)
    mixed = (jnp.dot(oa_buf[...], wa_ref[...], preferred_element_type=F32)
             + jnp.dot(ob_ref[...], wb_ref[...], preferred_element_type=F32))
    h = x_ref[...] + gate_ref[...] * mixed
    h_ref[...] = h
    ms = jnp.mean(h * h, axis=-1, keepdims=True)
    hn = h * lax.rsqrt(ms + EPS) * g_ref[...]
    hn_ref[...] = (hn * (1.0 + scale_ref[...]) + shift_ref[...]).astype(hn_ref.dtype)


def _head_select():
    src = jnp.arange(LANES)[:, None]
    head = jnp.arange(WIDTH_A)[None, :] // HEAD_DIM
    return (src == head * LSE_REP).astype(BF16)


def _out_proj(oas, lses, ob, x, gate, shift, scale, g, wa, wb):
    B, S, D = x.shape
    tm = OUT_TM
    sel = _head_select()
    half = pl.BlockSpec((None, tm, WIDTH_B), lambda b, i: (b, i, 0))
    row = pl.BlockSpec((None, tm, D), lambda b, i: (b, i, 0))
    mod = pl.BlockSpec((None, 1, D), lambda b, i: (b, 0, 0))
    const = lambda shape: pl.BlockSpec(shape, lambda b, i: (0,) * len(shape))
    res = lambda dil, w: pl.BlockSpec((None, tm // dil, dil * w), lambda b, i: (b, i, 0))
    outs = [res(1, WIDTH_A), res(4, WIDTH_A), res(16, WIDTH_A)]
    lse_specs = [res(1, LANES), res(4, LANES), res(16, LANES)]
    return pl.pallas_call(
        _out_proj_kernel,
        grid=(B, S // tm),
        in_specs=outs + lse_specs + [half, row, mod, mod, mod, const((1, D)), const(wa.shape), const(wb.shape),
                                     const(sel.shape)],
        out_specs=[row, row],
        out_shape=[jax.ShapeDtypeStruct((B, S, D), F32), jax.ShapeDtypeStruct((B, S, D), BF16)],
        scratch_shapes=[pltpu.VMEM((2, WIDTH_A // LANES, tm, LANES), F32), pltpu.VMEM((2, tm, LANES), F32),
                        pltpu.VMEM((tm, WIDTH_A), BF16)],
        compiler_params=_params("arbitrary", "arbitrary"),
        name="out_proj",
    )(*oas, *lses, ob, x, gate, shift, scale, g, wa, wb, sel)


def _ffn_kernel(hp_ref, hc_ref, hn_ref, h_ref, gate_ref, wu_ref, cw_ref, cb_ref, wd_ref, o_ref, lhs, act):
    i = pl.program_id(1)
    tm = hc_ref.shape[0]
    n = tm + 2 * FFN_HALO
    lhs[0:FFN_HALO, :] = jnp.where(i > 0, hp_ref[...], jnp.zeros_like(hp_ref))
    lhs[FFN_HALO:FFN_HALO + tm, :] = hc_ref[...]
    lhs[FFN_HALO + tm:, :] = jnp.where(i < pl.num_programs(1) - 1, hn_ref[...], jnp.zeros_like(hn_ref))
    x = lhs[...]

    def conv(u, lo):
        cw = cw_ref[:, lo:lo + FFN_FC]
        mid = slice(FFN_HALO, FFN_HALO + tm)
        below = pltpu.roll(u, 1, 0)[mid]
        above = pltpu.roll(u, n - 1, 0)[mid]
        return cw[0:1] * below + cw[1:2] * u[mid] + cw[2:3] * above + cb_ref[:, lo:lo + FFN_FC]

    for c in range(D_FF // FFN_FC):
        lo_v = c * FFN_FC
        lo_g = D_FF + c * FFN_FC
        val = conv(jnp.dot(x, wu_ref[:, lo_v:lo_v + FFN_FC], preferred_element_type=F32), lo_v)
        gt = conv(jnp.dot(x, wu_ref[:, lo_g:lo_g + FFN_FC], preferred_element_type=F32), lo_g)
        act[:, lo_v:lo_v + FFN_FC] = (gt / (1.0 + jnp.exp(-gt)) * val).astype(BF16)
    down = jnp.dot(act[...], wd_ref[...], preferred_element_type=F32)
    o_ref[...] = h_ref[...] + gate_ref[...] * down


def _ffn(hn, h, gate, wu, cw, cb, wd):
    B, S, D = h.shape
    tm = FFN_TM
    per = tm // FFN_HALO
    nh = S // FFN_HALO
    row = lambda dt: pl.BlockSpec((None, tm, D), lambda b, i: (b, i, 0))
    prev = pl.BlockSpec((None, FFN_HALO, D), lambda b, i: (b, jnp.maximum(i * per - 1, 0), 0))
    nxt = pl.BlockSpec((None, FFN_HALO, D), lambda b, i: (b, jnp.minimum((i + 1) * per, nh - 1), 0))
    mod = pl.BlockSpec((None, 1, D), lambda b, i: (b, 0, 0))
    const = lambda shape: pl.BlockSpec(shape, lambda b, i: (0,) * len(shape), pipeline_mode=pl.Buffered(1))
    return pl.pallas_call(
        _ffn_kernel,
        grid=(B, S // tm),
        in_specs=[prev, row(BF16), nxt, row(F32), mod, const(wu.shape), const(cw.shape), const(cb.shape),
                  const(wd.shape)],
        out_specs=row(F32),
        out_shape=jax.ShapeDtypeStruct((B, S, D), F32),
        scratch_shapes=[pltpu.VMEM((tm + 2 * FFN_HALO, D), BF16), pltpu.VMEM((tm, D_FF), BF16)],
        compiler_params=_params("arbitrary", "arbitrary"),
        name="conv_ffn",
    )(hn, hn, hn, h, gate, wu, cw, cb, wd)


def _block_diag_ones():
    r = jnp.arange(MXU_DIM) // HEAD_DIM
    return (r[:, None] == r[None, :]).astype(BF16)


def kernel(x, c, w_ada, b_ada, norm1_g, w_in, q_norm_a, k_norm_a, q_norm_b, k_norm_b, rel_bias, lambda_q1,
           lambda_k1, lambda_q2, lambda_k2, subln_g, w_out, norm2_g, w_up, conv_w, conv_b, w_down):
    B, S, D = x.shape
    depth = w_ada.shape[0]
    h = x.astype(F32)
    c8 = jnp.pad(c.astype(F32), ((0, 8 - B), (0, 0)))
    bd = _block_diag_ones()
    rel_bias = rel_bias.astype(F32)
    band_b, bmax_b = _diff_bias(rel_bias)
    band_a = _dil_bias(rel_bias)
    qscale = HEAD_DIM ** -0.5
    tile8 = lambda g: jnp.tile(g.astype(F32), WIDTH_A // HEAD_DIM)[None, :]
    row = lambda v: v.astype(F32)[None, :]

    for layer in range(depth):
        lambda_init = 0.8 - 0.6 * math.exp(-0.3 * layer)
        mod = _ada(c8, w_ada[layer].astype(F32), row(b_ada[layer]))[:B]
        shift1, scale1, gate1, shift2, scale2, gate2 = [m[:, None, :] for m in jnp.split(mod, 6, axis=-1)]

        qa, ka, va, qa4, ka4, va4, qa16, ka16, va16, qb, k1, k2, vbt = _in_proj(
            h, shift1, scale1, row(norm1_g[layer]), w_in[layer].astype(BF16), bd,
            tile8(q_norm_a[layer]) * (qscale * LOG2E), tile8(k_norm_a[layer]),
            tile8(q_norm_b[layer]) * (qscale * LOG2E), tile8(k_norm_b[layer]))

        oas, lses = [], []
        qkv = {1: (qa, ka, va), 4: (qa4, ka4, va4), 16: (qa16, ka16, va16)}
        for p, (_, dil) in enumerate(DILATED_PATTERNS):
            o, lse = _dilated(*qkv[dil], band_a[p], dil)
            oas.append(o)
            lses.append(lse)

        ob = _diff_attention(rel_bias, qb, k1, k2, vbt, band_b, bmax_b, row(lambda_q1[layer]), row(lambda_k1[layer]),
                             row(lambda_q2[layer]), row(lambda_k2[layer]),
                             subln_g[layer].astype(F32)[:, None], lambda_init)

        w_o = w_out[layer].astype(BF16)
        h, hn = _out_proj(oas, lses, ob, h, gate1, shift2, scale2, row(norm2_g[layer]),
                          w_o[:WIDTH_A], w_o[WIDTH_A:])
        h = _ffn(hn, h, gate2, w_up[layer].astype(BF16), conv_w[layer].astype(F32), row(conv_b[layer]),
                 w_down[layer].astype(BF16))

    return h.astype(x.dtype)
```
